```python
import math
import jax
import jax.numpy as jnp
from jax import lax
import numpy as np

D_MODEL = 2048
BATCH = 2
SEQ = 4096
DEPTH = 2
DEC_BATCH = 128
DEC_SEQ = 8
PAST_LEN = 8192
PAGE_SIZE = 128

HEAD_DIM = 128
N_HEADS_A = 8
N_KV_A = 2
G_A = N_HEADS_A // N_KV_A
WINDOW = 128
REL_BUCKETS = 32
REL_MAX_DIST = 128
N_HEADS_B = 8
DK_B = 128
DV_B = 128
CHUNK_B = 64
N_HEADS_C = 16
N_KV_C = 4
G_C = N_HEADS_C // N_KV_C
FOX_BLOCK = 128
D_FF = 5632
N_EXPERTS = 8
TOP_K = 2
EPS = 1e-6
SCALE = HEAD_DIM ** -0.5

N_EVEN = (DEPTH + 1) // 2
N_ODD = DEPTH // 2
D_A = N_HEADS_A * HEAD_DIM
KVD_A = N_KV_A * HEAD_DIM
QK_B = N_HEADS_B * DK_B
D_B = N_HEADS_B * DV_B
D_MIX_E = D_A + D_B
SPLIT_E = [D_A, D_A + KVD_A, D_A + 2 * KVD_A, D_A + 2 * KVD_A + QK_B, D_A + 2 * KVD_A + 2 * QK_B, D_A + 2 * KVD_A + 2 * QK_B + D_B]
IN_E = SPLIT_E[-1] + D_B
D_C = N_HEADS_C * HEAD_DIM
KVD_C = N_KV_C * HEAD_DIM
SPLIT_O = [D_C, D_C + KVD_C, D_C + 2 * KVD_C]
IN_O = D_C + 2 * KVD_C + N_HEADS_C
F32 = jnp.float32

kernel_name = 'hybrid_swa_hgrn2_fox_moe_step'


def rmsnorm(x, g):
    xf = x.astype(F32)
    y = xf * lax.rsqrt(jnp.mean(xf * xf, axis=-1, keepdims=True) + EPS)
    return (y * g.astype(F32)).astype(x.dtype)


def swiglu(h, w_gate, w_up, w_down):
    return (jax.nn.silu(h @ w_gate) * (h @ w_up)) @ w_down


def t5_bucket(dist):
    d = jnp.maximum(dist, 0)
    max_exact = REL_BUCKETS // 2
    ratio = jnp.log(jnp.maximum(d, 1).astype(F32) / max_exact) / math.log(REL_MAX_DIST / max_exact)
    large = jnp.minimum(max_exact + (ratio * (REL_BUCKETS - max_exact)).astype(jnp.int32), REL_BUCKETS - 1)
    return jnp.where(d < max_exact, d, large)


def rel_bias_heads(rel_bias, dist):
    b = rel_bias[t5_bucket(dist)].astype(F32)
    return jnp.transpose(b, (2, 0, 1)).reshape(N_KV_A, G_A, *dist.shape)


def sink_softmax(scores, mask, sinks):
    s = jnp.where(mask, scores, -jnp.inf)
    sk = sinks.astype(F32).reshape(N_KV_A, G_A, 1, 1)
    m = jnp.maximum(jnp.max(s, axis=-1, keepdims=True), sk)
    e = jnp.exp(s - m)
    return e / (jnp.sum(e, axis=-1, keepdims=True) + jnp.exp(sk - m))


def even_project(h, w_in, q_norm, k_norm, lb):
    lead = h.shape[:-1]
    qa, ka, va, qb, fb, ib, gb = jnp.split(h @ w_in, SPLIT_E, axis=-1)
    qa = rmsnorm(qa.reshape(*lead, N_HEADS_A, HEAD_DIM), q_norm)
    ka = rmsnorm(ka.reshape(*lead, N_KV_A, HEAD_DIM), k_norm)
    va = va.reshape(*lead, N_KV_A, HEAD_DIM)
    f = (lb + (1.0 - lb) * jax.nn.sigmoid(fb.astype(F32))).reshape(*lead, N_HEADS_B, DK_B)
    qb = jax.nn.silu(qb).reshape(*lead, N_HEADS_B, DK_B)
    vb = ib.reshape(*lead, N_HEADS_B, DV_B)
    return qa, ka, va, qb, 1.0 - f, vb, jnp.log(f), gb


def even_merge(oa, ob, gb, o_norm, w_out):
    lead = oa.shape[:-1]
    obn = rmsnorm(ob, o_norm).reshape(*lead, D_B).astype(oa.dtype) * jax.nn.silu(gb)
    return jnp.concatenate([oa, obn], axis=-1) @ w_out


def swa_prompt(q, k, v, rel_bias, sinks):
    B, S = q.shape[:2]
    nb = S // WINDOW
    qb = q.reshape(B, nb, WINDOW, N_KV_A, G_A, HEAD_DIM)

    def band(t):
        tb = t.reshape(B, nb, WINDOW, N_KV_A, HEAD_DIM)
        prev = jnp.concatenate([jnp.zeros_like(tb[:, :1]), tb[:, :-1]], axis=1)
        return jnp.concatenate([prev, tb], axis=2)

    kb, vb = band(k), band(v)
    i = jnp.arange(WINDOW)[:, None]
    j = jnp.arange(2 * WINDOW)[None, :]
    dist = WINDOW + i - j
    in_win = (dist >= 0) & (dist < WINDOW)
    valid = in_win[None] & ((jnp.arange(nb) > 0)[:, None, None] | (j >= WINDOW)[None])
    mask = valid[None, :, None, None]
    scores = jnp.einsum('bnqkgd,bnskd->bnkgqs', qb, kb).astype(F32) * SCALE + rel_bias_heads(rel_bias, dist)
    p = sink_softmax(scores, mask, sinks)
    o = jnp.einsum('bnkgqs,bnskd->bnqkgd', p.astype(v.dtype), vb)
    return o.reshape(B, S, D_A)


def swa_sample(q, k, v, buf_k, buf_v, rel_bias, sinks):
    DB, DS = q.shape[:2]
    W = buf_k.shape[1]
    kk = jnp.concatenate([buf_k, k], axis=1)
    vv = jnp.concatenate([buf_v, v], axis=1)
    dist = W + jnp.arange(DS)[:, None] - jnp.arange(W + DS)[None, :]
    mask = (dist >= 0) & (dist < WINDOW)
    qg = q.reshape(DB, DS, N_KV_A, G_A, HEAD_DIM)
    scores = jnp.einsum('bqkgd,bskd->bkgqs', qg, kk).astype(F32) * SCALE + rel_bias_heads(rel_bias, dist)
    p = sink_softmax(scores, mask, sinks)
    o = jnp.einsum('bkgqs,bskd->bqkgd', p.astype(v.dtype), vv).reshape(DB, DS, D_A)
    return o, kk[:, DS:], vv[:, DS:]


def hgrn2_chunk(S0, q, k, v, logf):
    C = q.shape[1]
    b = jnp.cumsum(logf.astype(F32), axis=1)
    bt = jnp.transpose(b, (0, 2, 1, 3))
    causal = (jnp.arange(C)[:, None] >= jnp.arange(C)[None, :])[None, None, :, :, None]
    decay = jnp.exp(jnp.where(causal, bt[:, :, :, None, :] - bt[:, :, None, :, :], -jnp.inf))
    A = jnp.einsum('bthd,bshd,bhtsd->bhts', q, k, decay)
    o = jnp.einsum('bthd,bhde->bthe', q * jnp.exp(b), S0) + jnp.einsum('bhts,bshe->bthe', A, v)
    b_last = b[:, -1]
    S_new = jnp.exp(b_last)[..., None] * S0 + jnp.einsum('bshd,bshe->bhde', k * jnp.exp(b_last[:, None] - b), v)
    return S_new, o


def hgrn2_prompt(q, k, v, logf):
    B, S = q.shape[:2]
    nc = S // CHUNK_B

    def to_chunks(t):
        return jnp.moveaxis(t.reshape(B, nc, CHUNK_B, *t.shape[2:]), 1, 0)

    S0 = jnp.zeros((B, N_HEADS_B, DK_B, DV_B), F32)
    S_fin, o = lax.scan(lambda s, xs: hgrn2_chunk(s, *xs), S0,
                        (to_chunks(q), to_chunks(k), to_chunks(v), to_chunks(logf)))
    return jnp.moveaxis(o, 0, 1).reshape(B, S, N_HEADS_B, DV_B), S_fin


def odd_project(h, w_in, q_norm, k_norm, f_bias):
    lead = h.shape[:-1]
    qc, kc, vc, fc = jnp.split(h @ w_in, SPLIT_O, axis=-1)
    qc = rmsnorm(qc.reshape(*lead, N_HEADS_C, HEAD_DIM), q_norm)
    kc = rmsnorm(kc.reshape(*lead, N_KV_C, HEAD_DIM), k_norm)
    vc = vc.reshape(*lead, N_KV_C, HEAD_DIM)
    logf = jax.nn.log_sigmoid(fc.astype(F32) + f_bias.astype(F32))
    return qc, kc, vc, logf


def fox_prompt(q, k, v, logf):
    B, S = q.shape[:2]
    nq = S // FOX_BLOCK
    c = jnp.cumsum(logf, axis=1)
    c_k = jnp.transpose(c.reshape(B, S, N_KV_C, G_C), (0, 2, 3, 1))[:, :, :, None, :]
    kpos = jnp.arange(S)

    def block(xs):
        qb, cb, qpos = xs
        s = jnp.einsum('bqkgd,bskd->bkgqs', qb, k).astype(F32) * SCALE
        cq = jnp.transpose(cb.reshape(B, FOX_BLOCK, N_KV_C, G_C), (0, 2, 3, 1))[..., None]
        s = jnp.where(kpos[None, :] <= qpos[:, None], s + cq - c_k, -jnp.inf)
        p = jax.nn.softmax(s, axis=-1)
        return jnp.einsum('bkgqs,bskd->bqkgd', p.astype(v.dtype), v)

    qb = jnp.moveaxis(q.reshape(B, nq, FOX_BLOCK, N_KV_C, G_C, HEAD_DIM), 1, 0)
    cb = jnp.moveaxis(c.reshape(B, nq, FOX_BLOCK, N_HEADS_C), 1, 0)
    qpos = jnp.arange(S).reshape(nq, FOX_BLOCK)
    o = lax.map(block, (qb, cb, qpos))
    return jnp.moveaxis(o, 0, 1).reshape(B, S, D_C)


def fox_sample(q, k, v, logf, cache_k, cache_v, cache_logf, layer, page_table):
    DB, DS = q.shape[:2]
    n_pages = page_table.shape[1]
    page = cache_k.shape[2]
    qg = q.reshape(DB, DS, N_KV_C, G_C, HEAD_DIM)

    def heads_last(t):
        return jnp.transpose(t.reshape(DB, t.shape[1], N_KV_C, G_C), (0, 2, 3, 1))

    cn = jnp.cumsum(logf, axis=1)
    cq = heads_last(cn)[..., None]
    s = jnp.einsum('bqkgd,bskd->bkgqs', qg, k).astype(F32) * SCALE + cq - heads_last(cn)[..., None, :]
    causal = jnp.arange(DS)[:, None] >= jnp.arange(DS)[None, :]
    s = jnp.where(causal, s, -jnp.inf)
    m0 = jnp.max(s, axis=-1)
    e0 = jnp.exp(s - m0[..., None])
    l0 = jnp.sum(e0, axis=-1)
    acc0 = jnp.einsum('bkgqs,bskd->bkgqd', e0, v.astype(F32))
    lf = cache_logf[layer, page_table].reshape(DB, n_pages * page, N_HEADS_C).astype(F32)
    rc = lax.cumsum(lf, axis=1, reverse=True)
    dex = jnp.concatenate([rc[:, 1:], jnp.zeros_like(rc[:, :1])], axis=1)
    dex_pages = jnp.moveaxis(dex.reshape(DB, n_pages, page, N_HEADS_C), 1, 0)

    def step(carry, xs):
        m, l, acc = carry
        phys, dp = xs
        kp = cache_k[layer, phys]
        vp = cache_v[layer, phys]
        sp = jnp.einsum('bqkgd,bskd->bkgqs', qg, kp).astype(F32) * SCALE + cq + heads_last(dp)[..., None, :]
        m_new = jnp.maximum(m, jnp.max(sp, axis=-1))
        alpha = jnp.exp(m - m_new)
        ep = jnp.exp(sp - m_new[..., None])
        l = l * alpha + jnp.sum(ep, axis=-1)
        acc = acc * alpha[..., None] + jnp.einsum('bkgqs,bskd->bkgqd', ep, vp.astype(F32))
        return (m_new, l, acc), None

    (m, l, acc), _ = lax.scan(step, (m0, l0, acc0), (page_table.T, dex_pages))
    o = acc / l[..., None]
    return jnp.transpose(o, (0, 3, 1, 2, 4)).reshape(DB, DS, D_C).astype(v.dtype)


def moe_ffn(h, router, w_gate, w_up, w_down):
    logits = (h @ router).astype(F32)
    top_v, top_i = lax.top_k(logits, TOP_K)
    w = jax.nn.softmax(top_v, axis=-1)
    gates = jnp.sum(jax.nn.one_hot(top_i, N_EXPERTS, dtype=F32) * w[..., None], axis=1)
    out = jnp.zeros(h.shape, F32)
    for e in range(N_EXPERTS):
        out = out + gates[:, e:e + 1] * swiglu(h, w_gate[e], w_up[e], w_down[e]).astype(F32)
    return out.astype(h.dtype)


def setup_inputs(seed: int = 0) -> dict:
    key = jax.random.key(seed)
    ks = iter(jax.random.split(key, 40))

    def nrm(shape, scale):
        return jax.random.normal(next(ks), shape, F32) * scale

    def gain(shape):
        return 1.0 + nrm(shape, 0.02)

    n_pages = PAST_LEN // PAGE_SIZE
    n_used = DEC_BATCH * n_pages
    n_pool = n_used + (n_used + 3) // 4
    win = min(WINDOW, PAST_LEN)
    page_table = jax.random.permutation(next(ks), n_pool)[:n_used].reshape(DEC_BATCH, n_pages).astype(jnp.int32)
    return {
        'x_prompt': nrm((BATCH, SEQ, D_MODEL), 1.0),
        'x_sample': nrm((DEC_BATCH, DEC_SEQ, D_MODEL), 1.0),
        'cache_k_win': nrm((N_EVEN, DEC_BATCH, win, N_KV_A, HEAD_DIM), 1.0),
        'cache_v_win': nrm((N_EVEN, DEC_BATCH, win, N_KV_A, HEAD_DIM), 1.0),
        'state_hgrn': nrm((N_EVEN, DEC_BATCH, N_HEADS_B, DK_B, DV_B), 0.3),
        'cache_k_fox': nrm((N_ODD, n_pool, PAGE_SIZE, N_KV_C, HEAD_DIM), 1.0),
        'cache_v_fox': nrm((N_ODD, n_pool, PAGE_SIZE, N_KV_C, HEAD_DIM), 1.0),
        'cache_logf_fox': jax.nn.log_sigmoid(4.0 + nrm((N_ODD, n_pool, PAGE_SIZE, N_HEADS_C), 1.5)),
        'page_table': page_table,
        'rel_bias': nrm((REL_BUCKETS, N_HEADS_A), 0.5),
        'norm_mix_e': gain((N_EVEN, D_MODEL)),
        'w_in_e': nrm((N_EVEN, D_MODEL, IN_E), D_MODEL ** -0.5),
        'q_norm_a': gain((N_EVEN, HEAD_DIM)),
        'k_norm_a': gain((N_EVEN, HEAD_DIM)),
        'sinks_a': nrm((N_EVEN, N_HEADS_A), 0.5),
        'lb_b': nrm((N_EVEN + 1, QK_B), 0.5),
        'o_norm_b': gain((N_EVEN, DV_B)),
        'w_out_e': nrm((N_EVEN, D_MIX_E, D_MODEL), D_MIX_E ** -0.5),
        'norm_ffn_e': gain((N_EVEN, D_MODEL)),
        'w_gate_e': nrm((N_EVEN, D_MODEL, D_FF), D_MODEL ** -0.5),
        'w_up_e': nrm((N_EVEN, D_MODEL, D_FF), D_MODEL ** -0.5),
        'w_down_e': nrm((N_EVEN, D_FF, D_MODEL), D_FF ** -0.5),
        'norm_mix_o': gain((N_ODD, D_MODEL)),
        'w_in_o': nrm((N_ODD, D_MODEL, IN_O), D_MODEL ** -0.5),
        'fgate_bias_c': jax.random.uniform(next(ks), (N_ODD, N_HEADS_C), F32, 2.0, 6.0),
        'q_norm_c': gain((N_ODD, HEAD_DIM)),
        'k_norm_c': gain((N_ODD, HEAD_DIM)),
        'w_out_o': nrm((N_ODD, D_C, D_MODEL), D_C ** -0.5),
        'norm_ffn_o': gain((N_ODD, D_MODEL)),
        'router_o': nrm((N_ODD, D_MODEL, N_EXPERTS), D_MODEL ** -0.5),
        'w_gate_x': nrm((N_ODD, N_EXPERTS, D_MODEL, D_FF), D_MODEL ** -0.5),
        'w_up_x': nrm((N_ODD, N_EXPERTS, D_MODEL, D_FF), D_MODEL ** -0.5),
        'w_down_x': nrm((N_ODD, N_EXPERTS, D_FF, D_MODEL), D_FF ** -0.5),
    }


def reference(x_prompt, x_sample, cache_k_win, cache_v_win, state_hgrn, cache_k_fox, cache_v_fox,
              cache_logf_fox, page_table, rel_bias, norm_mix_e, w_in_e, q_norm_a, k_norm_a, sinks_a,
              lb_b, o_norm_b, w_out_e, norm_ffn_e, w_gate_e, w_up_e, w_down_e, norm_mix_o, w_in_o,
              fgate_bias_c, q_norm_c, k_norm_c, w_out_o, norm_ffn_o, router_o, w_gate_x, w_up_x, w_down_x):
    yp, ys = x_prompt, x_sample
    lbs = jnp.cumsum(jax.nn.softmax(lb_b.astype(F32), axis=0), axis=0)
    wk_p, wv_p, wk_s, wv_s, hg_p, hg_s = [], [], [], [], [], []
    fk_p, fv_p, fl_p, fk_s, fv_s, fl_s = [], [], [], [], [], []
    for l in range(DEPTH):
        j = l // 2
        if l % 2 == 0:
            hp = rmsnorm(yp, norm_mix_e[j])
            qa, ka, va, qb, kb, vb, lf, gb = even_project(hp, w_in_e[j], q_norm_a[j], k_norm_a[j], lbs[j])
            oa = swa_prompt(qa, ka, va, rel_bias, sinks_a[j])
            ob, S_p = hgrn2_prompt(qb, kb, vb, lf)
            yp = yp + even_merge(oa, ob, gb, o_norm_b[j], w_out_e[j])
            wk_p.append(ka[:, -WINDOW:])
            wv_p.append(va[:, -WINDOW:])
            hg_p.append(S_p)

            hs = rmsnorm(ys, norm_mix_e[j])
            qa, ka, va, qb, kb, vb, lf, gb = even_project(hs, w_in_e[j], q_norm_a[j], k_norm_a[j], lbs[j])
            oa, nk, nv = swa_sample(qa, ka, va, cache_k_win[j], cache_v_win[j], rel_bias, sinks_a[j])
            S_s, ob = hgrn2_chunk(state_hgrn[j].astype(F32), qb, kb, vb, lf)
            ys = ys + even_merge(oa, ob, gb, o_norm_b[j], w_out_e[j])
            wk_s.append(nk)
            wv_s.append(nv)
            hg_s.append(S_s)

            yp = yp + swiglu(rmsnorm(yp, norm_ffn_e[j]), w_gate_e[j], w_up_e[j], w_down_e[j])
            ys = ys + swiglu(rmsnorm(ys, norm_ffn_e[j]), w_gate_e[j], w_up_e[j], w_down_e[j])
        else:
            hp = rmsnorm(yp, norm_mix_o[j])
            qc, kc, vc, lfc = odd_project(hp, w_in_o[j], q_norm_c[j], k_norm_c[j], fgate_bias_c[j])
            yp = yp + fox_prompt(qc, kc, vc, lfc) @ w_out_o[j]
            fk_p.append(kc)
            fv_p.append(vc)
            fl_p.append(lfc)

            hs = rmsnorm(ys, norm_mix_o[j])
            qc, kc, vc, lfc = odd_project(hs, w_in_o[j], q_norm_c[j], k_norm_c[j], fgate_bias_c[j])
            oc = fox_sample(qc, kc, vc, lfc, cache_k_fox, cache_v_fox, cache_logf_fox, j, page_table)
            ys = ys + oc @ w_out_o[j]
            fk_s.append(kc)
            fv_s.append(vc)
            fl_s.append(lfc)

            hp = rmsnorm(yp, norm_ffn_o[j]).reshape(-1, D_MODEL)
            yp = yp + moe_ffn(hp, router_o[j], w_gate_x[j], w_up_x[j], w_down_x[j]).reshape(yp.shape)
            hs = rmsnorm(ys, norm_ffn_o[j]).reshape(-1, D_MODEL)
            ys = ys + moe_ffn(hs, router_o[j], w_gate_x[j], w_up_x[j], w_down_x[j]).reshape(ys.shape)
    return (yp, ys,
            jnp.stack(wk_p), jnp.stack(wv_p), jnp.stack(wk_s), jnp.stack(wv_s),
            jnp.stack(hg_p), jnp.stack(hg_s),
            jnp.stack(fk_p), jnp.stack(fv_p), jnp.stack(fl_p),
            jnp.stack(fk_s), jnp.stack(fv_s), jnp.stack(fl_s))
```

```python
import functools
import math

import numpy as np
import jax
import jax.numpy as jnp
from jax import lax
from jax.experimental import pallas as pl
from jax.experimental.pallas import tpu as pltpu

F32 = jnp.float32
BF16 = jnp.bfloat16

HEAD_DIM = 128
N_HEADS_A, N_KV_A = 8, 2
G_A = N_HEADS_A // N_KV_A
WINDOW = 128
REL_BUCKETS, REL_MAX_DIST = 32, 128
N_HEADS_B = 8
N_HEADS_C, N_KV_C = 16, 4
G_C = N_HEADS_C // N_KV_C
N_EXPERTS, TOP_K = 8, 2
EPS = 1e-6
SCALE = HEAD_DIM ** -0.5
NEG = -1e30

D_A = N_HEADS_A * HEAD_DIM
KVD_A = N_KV_A * HEAD_DIM
D_B = N_HEADS_B * HEAD_DIM
D_C = N_HEADS_C * HEAD_DIM
KVD_C = N_KV_C * HEAD_DIM

E_QA, E_QB, E_F, E_VB, E_GB = 0, D_A, D_A + D_B, D_A + 2 * D_B, D_A + 3 * D_B
E_KA = D_A + 4 * D_B
E_VA = E_KA + KVD_A
E_OUT = E_VA + KVD_A
O_Q, O_K, O_V = 0, D_C, D_C + KVD_C
O_OUT = D_C + 2 * KVD_C

VMEM_LIMIT = 56 * 1024 * 1024


def _cparams(n_axes):
    return pltpu.CompilerParams(dimension_semantics=("arbitrary",) * n_axes,
                                vmem_limit_bytes=VMEM_LIMIT)


def _rms(x, gain):
    return x * lax.rsqrt(jnp.mean(x * x, axis=-1, keepdims=True) + EPS) * gain


def _silu(x):
    return x * (1.0 / (1.0 + jnp.exp(-x)))


def _dot(a, b):
    return jnp.dot(a, b, preferred_element_type=F32)


def _dot_nt(a, b):
    return lax.dot_general(a, b, (((1,), (1,)), ((), ())), preferred_element_type=F32)


def _dot_tn(a, b):
    return lax.dot_general(a, b, (((0,), (0,)), ((), ())), preferred_element_type=F32)


def _dot_exact01(m01, x):
    x1 = x.astype(BF16)
    r1 = x - x1.astype(F32)
    x2 = r1.astype(BF16)
    x3 = (r1 - x2.astype(F32)).astype(BF16)
    m = m01.astype(BF16)
    return _dot(m, x1) + _dot(m, x2) + _dot(m, x3)


def _dot_exact01_r(x, m01):
    x1 = x.astype(BF16)
    r1 = x - x1.astype(F32)
    x2 = r1.astype(BF16)
    x3 = (r1 - x2.astype(F32)).astype(BF16)
    m = m01.astype(BF16)
    return _dot(x1, m) + _dot(x2, m) + _dot(x3, m)


_TN = 256


def _even_out_tile(j):
    return jnp.where(j < 4, j, jnp.where(j < 6, j + 16, j - 2))


def _even_proj_kernel(x_ref, g_ref, w_ref, qn_ref, kn_ref, lb_ref, o_ref, hn_ref):
    j = pl.program_id(1)

    @pl.when(j == 0)
    def _():
        hn_ref[...] = _rms(x_ref[...], g_ref[...]).astype(BF16)

    acc = _dot(hn_ref[...], w_ref[...].astype(BF16))

    def head_norm(gain):
        for h in range(_TN // HEAD_DIM):
            sl = slice(h * HEAD_DIM, (h + 1) * HEAD_DIM)
            o_ref[:, sl] = _rms(acc[:, sl], gain)

    @pl.when(j < 4)
    def _():
        head_norm(qn_ref[...])

    @pl.when(j == 4)
    def _():
        head_norm(kn_ref[...])

    @pl.when((j == 5) | (j >= 14))
    def _():
        o_ref[...] = acc

    @pl.when((j >= 6) & (j < 10))
    def _():
        o_ref[...] = _silu(acc)

    @pl.when((j >= 10) & (j < 14))
    def _():
        lb = lb_ref[...]
        e = jnp.exp(lb - jnp.max(lb, axis=0, keepdims=True))
        lb0 = e[0:1, :] / jnp.sum(e, axis=0, keepdims=True)
        o_ref[...] = lb0 + (1.0 - lb0) * (1.0 / (1.0 + jnp.exp(-acc)))


def even_project(x, g, w_in, q_norm, k_norm, lb_b, tm):
    n, d = x.shape
    n_tiles = w_in.shape[1] // _TN
    return pl.pallas_call(
        _even_proj_kernel,
        grid=(n // tm, n_tiles),
        in_specs=[
            pl.BlockSpec((tm, d), lambda i, j: (i, 0)),
            pl.BlockSpec((1, d), lambda i, j: (0, 0)),
            pl.BlockSpec((d, _TN), lambda i, j: (0, j)),
            pl.BlockSpec((1, HEAD_DIM), lambda i, j: (0, 0)),
            pl.BlockSpec((1, HEAD_DIM), lambda i, j: (0, 0)),
            pl.BlockSpec((lb_b.shape[0], _TN), lambda i, j: (0, jnp.clip(j - 10, 0, 3))),
        ],
        out_specs=pl.BlockSpec((tm, _TN), lambda i, j: (i, _even_out_tile(j))),
        out_shape=jax.ShapeDtypeStruct((n, E_OUT), F32),
        scratch_shapes=[pltpu.VMEM((tm, d), BF16)],
        compiler_params=_cparams(2),
        name="even_project",
    )(x, g.reshape(1, d), w_in, q_norm.reshape(1, -1), k_norm.reshape(1, -1), lb_b)


def _even_merge_kernel(oa_ref, ob_ref, gb_ref, on_ref, w_ref, x_ref, o_ref, lhs_ref):
    j = pl.program_id(1)

    @pl.when(j == 0)
    def _():
        lhs_ref[:, :D_A] = oa_ref[...].astype(BF16)
        for h in range(N_HEADS_B):
            sl = slice(h * HEAD_DIM, (h + 1) * HEAD_DIM)
            obn = _rms(ob_ref[:, sl], on_ref[...]) * _silu(gb_ref[:, sl])
            lhs_ref[:, D_A + h * HEAD_DIM:D_A + (h + 1) * HEAD_DIM] = obn.astype(BF16)

    o_ref[...] = x_ref[...] + _dot(lhs_ref[...], w_ref[...].astype(BF16))


def even_merge(oa, ob, proj, o_norm, w_out, x, tm, tn=512):
    n, d = x.shape
    return pl.pallas_call(
        _even_merge_kernel,
        grid=(n // tm, d // tn),
        in_specs=[
            pl.BlockSpec((tm, D_A), lambda i, j: (i, 0)),
            pl.BlockSpec((tm, D_B), lambda i, j: (i, 0)),
            pl.BlockSpec((tm, D_B), lambda i, j: (i, E_GB // D_B)),
            pl.BlockSpec((1, HEAD_DIM), lambda i, j: (0, 0)),
            pl.BlockSpec((D_A + D_B, tn), lambda i, j: (0, j)),
            pl.BlockSpec((tm, tn), lambda i, j: (i, j)),
        ],
        out_specs=pl.BlockSpec((tm, tn), lambda i, j: (i, j)),
        out_shape=jax.ShapeDtypeStruct((n, d), F32),
        scratch_shapes=[pltpu.VMEM((tm, D_A + D_B), BF16)],
        compiler_params=_cparams(2),
        name="even_merge",
    )(oa, ob, proj, o_norm.reshape(1, -1), w_out, x)


def _dense_ffn_kernel(x_ref, g_ref, wg_ref, wu_ref, wd_ref, o_ref, hn_ref, acc_ref):
    f = pl.program_id(1)

    @pl.when(f == 0)
    def _():
        hn_ref[...] = _rms(x_ref[...], g_ref[...]).astype(BF16)
        acc_ref[...] = jnp.zeros_like(acc_ref)

    hn = hn_ref[...]
    a = _dot(hn, wg_ref[...].astype(BF16))
    u = _dot(hn, wu_ref[...].astype(BF16))
    act = (_silu(a) * u).astype(BF16)
    acc_ref[...] += _dot(act, wd_ref[...].astype(BF16))

    @pl.when(f == pl.num_programs(1) - 1)
    def _():
        o_ref[...] = x_ref[...] + acc_ref[...]


def dense_ffn(x, g, w_gate, w_up, w_down, tm, tf=256):
    n, d = x.shape
    d_ff = w_gate.shape[1]
    return pl.pallas_call(
        _dense_ffn_kernel,
        grid=(n // tm, d_ff // tf),
        in_specs=[
            pl.BlockSpec((tm, d), lambda i, f: (i, 0)),
            pl.BlockSpec((1, d), lambda i, f: (0, 0)),
            pl.BlockSpec((d, tf), lambda i, f: (0, f)),
            pl.BlockSpec((d, tf), lambda i, f: (0, f)),
            pl.BlockSpec((tf, d), lambda i, f: (f, 0)),
        ],
        out_specs=pl.BlockSpec((tm, d), lambda i, f: (i, 0)),
        out_shape=jax.ShapeDtypeStruct((n, d), F32),
        scratch_shapes=[pltpu.VMEM((tm, d), BF16), pltpu.VMEM((tm, d), F32)],
        compiler_params=_cparams(2),
        name="dense_ffn",
    )(x, g.reshape(1, d), w_gate, w_up, w_down)


def _odd_proj_kernel(x_ref, g_ref, w_ref, wf_ref, qn_ref, kn_ref, fb_ref, o_ref, lf_ref, hn_ref):
    j = pl.program_id(1)
    nq = D_C // _TN
    nk = KVD_C // _TN

    @pl.when(j == 0)
    def _():
        hn = _rms(x_ref[...], g_ref[...]).astype(BF16)
        hn_ref[...] = hn
        z = _dot(hn, wf_ref[...].astype(BF16)) + fb_ref[...]
        lf_ref[...] = jnp.minimum(z, 0.0) - jnp.log(1.0 + jnp.exp(-jnp.abs(z)))

    acc = _dot(hn_ref[...], w_ref[...].astype(BF16))

    def head_norm(gain):
        for h in range(_TN // HEAD_DIM):
            sl = slice(h * HEAD_DIM, (h + 1) * HEAD_DIM)
            o_ref[:, sl] = _rms(acc[:, sl], gain)

    @pl.when(j < nq)
    def _():
        head_norm(qn_ref[...])

    @pl.when((j >= nq) & (j < nq + nk))
    def _():
        head_norm(kn_ref[...])

    @pl.when(j >= nq + nk)
    def _():
        o_ref[...] = acc


def odd_project(x, g, w_in, q_norm, k_norm, f_bias, tm):
    n, d = x.shape
    w_f = w_in[:, O_OUT:]
    return pl.pallas_call(
        _odd_proj_kernel,
        grid=(n // tm, O_OUT // _TN),
        in_specs=[
            pl.BlockSpec((tm, d), lambda i, j: (i, 0)),
            pl.BlockSpec((1, d), lambda i, j: (0, 0)),
            pl.BlockSpec((d, _TN), lambda i, j: (0, j)),
            pl.BlockSpec((d, N_HEADS_C), lambda i, j: (0, 0)),
            pl.BlockSpec((1, HEAD_DIM), lambda i, j: (0, 0)),
            pl.BlockSpec((1, HEAD_DIM), lambda i, j: (0, 0)),
            pl.BlockSpec((1, N_HEADS_C), lambda i, j: (0, 0)),
        ],
        out_specs=[
            pl.BlockSpec((tm, _TN), lambda i, j: (i, j)),
            pl.BlockSpec((tm, N_HEADS_C), lambda i, j: (i, 0)),
        ],
        out_shape=[jax.ShapeDtypeStruct((n, O_OUT), F32),
                   jax.ShapeDtypeStruct((n, N_HEADS_C), F32)],
        scratch_shapes=[pltpu.VMEM((tm, d), BF16)],
        compiler_params=_cparams(2),
        name="odd_project",
    )(x, g.reshape(1, d), w_in, w_f, q_norm.reshape(1, -1), k_norm.reshape(1, -1),
      f_bias.reshape(1, -1))


def _out_proj_kernel(a_ref, w_ref, x_ref, o_ref, lhs_ref):
    @pl.when(pl.program_id(1) == 0)
    def _():
        lhs_ref[...] = a_ref[...].astype(BF16)

    o_ref[...] = x_ref[...] + _dot(lhs_ref[...], w_ref[...].astype(BF16))


def out_project(a, w, x, tm, tn=512):
    n, d = x.shape
    k = a.shape[1]
    return pl.pallas_call(
        _out_proj_kernel,
        grid=(n // tm, d // tn),
        in_specs=[
            pl.BlockSpec((tm, k), lambda i, j: (i, 0)),
            pl.BlockSpec((k, tn), lambda i, j: (0, j)),
            pl.BlockSpec((tm, tn), lambda i, j: (i, j)),
        ],
        out_specs=pl.BlockSpec((tm, tn), lambda i, j: (i, j)),
        out_shape=jax.ShapeDtypeStruct((n, d), F32),
        scratch_shapes=[pltpu.VMEM((tm, k), BF16)],
        compiler_params=_cparams(2),
        name="out_project",
    )(a, w, x)


def _t5_bucket(dist):
    d = jnp.maximum(dist, 0)
    max_exact = REL_BUCKETS // 2
    ratio = jnp.log(jnp.maximum(d, 1).astype(F32) / max_exact) / math.log(REL_MAX_DIST / max_exact)
    large = jnp.minimum(max_exact + (ratio * (REL_BUCKETS - max_exact)).astype(jnp.int32),
                        REL_BUCKETS - 1)
    return jnp.where(d < max_exact, d, large)


def _bias_from_buckets(bucket, rel_ref, h):
    out = jnp.zeros(bucket.shape, F32)
    for b in range(REL_BUCKETS):
        out = jnp.where(bucket == b, rel_ref[b, h], out)
    return out


def _swa_prompt_kernel(q_ref, kp_ref, kc_ref, vp_ref, vc_ref, bkt_ref, rel_ref, sink_ref,
                       o_ref, bias_ref):
    n = pl.program_id(1)
    kv = pl.program_id(2)

    @pl.when((pl.program_id(0) == 0) & (n == 0) & (kv == 0))
    def _():
        for h in range(N_HEADS_A):
            bias_ref[h] = _bias_from_buckets(bkt_ref[...], rel_ref, h)

    kk = jnp.concatenate([kp_ref[...], kc_ref[...]], axis=0).astype(BF16)
    vv = jnp.concatenate([vp_ref[...], vc_ref[...]], axis=0).astype(BF16)
    row = lax.broadcasted_iota(jnp.int32, (WINDOW, 2 * WINDOW), 0)
    col = lax.broadcasted_iota(jnp.int32, (WINDOW, 2 * WINDOW), 1)
    dist = WINDOW + row - col
    valid = (dist >= 0) & (dist < WINDOW) & ((n > 0) | (col >= WINDOW))
    for g in range(G_A):
        h = kv * G_A + g
        sl = slice(g * HEAD_DIM, (g + 1) * HEAD_DIM)
        s = _dot_nt(q_ref[:, sl].astype(BF16), kk) * SCALE + bias_ref[h]
        s = jnp.where(valid, s, NEG)
        sk = sink_ref[h]
        m = jnp.maximum(jnp.max(s, axis=-1, keepdims=True), sk)
        e = jnp.exp(s - m)
        p = e / (jnp.sum(e, axis=-1, keepdims=True) + jnp.exp(sk - m))
        o_ref[:, sl] = _dot(p.astype(BF16), vv)


def swa_prompt(proj, rel_bias, sinks, batch, seq):
    nb = seq // WINDOW
    i = jnp.arange(WINDOW)[:, None]
    j = jnp.arange(2 * WINDOW)[None, :]
    bucket = _t5_bucket(WINDOW + i - j).astype(jnp.int32)
    kcol, vcol = E_KA // HEAD_DIM, E_VA // HEAD_DIM
    smem = pl.BlockSpec(memory_space=pltpu.SMEM)
    return pl.pallas_call(
        _swa_prompt_kernel,
        grid=(batch, nb, N_KV_A),
        in_specs=[
            pl.BlockSpec((WINDOW, G_A * HEAD_DIM), lambda b, n, k: (b * nb + n, k)),
            pl.BlockSpec((WINDOW, HEAD_DIM), lambda b, n, k: (b * nb + jnp.maximum(n - 1, 0), kcol + k)),
            pl.BlockSpec((WINDOW, HEAD_DIM), lambda b, n, k: (b * nb + n, kcol + k)),
            pl.BlockSpec((WINDOW, HEAD_DIM), lambda b, n, k: (b * nb + jnp.maximum(n - 1, 0), vcol + k)),
            pl.BlockSpec((WINDOW, HEAD_DIM), lambda b, n, k: (b * nb + n, vcol + k)),
            pl.BlockSpec((WINDOW, 2 * WINDOW), lambda b, n, k: (0, 0)),
            smem, smem,
        ],
        out_specs=pl.BlockSpec((WINDOW, G_A * HEAD_DIM), lambda b, n, k: (b * nb + n, k)),
        out_shape=jax.ShapeDtypeStruct((batch * seq, D_A), F32),
        scratch_shapes=[pltpu.VMEM((N_HEADS_A, WINDOW, 2 * WINDOW), F32)],
        compiler_params=_cparams(3),
        name="swa_prompt",
    )(proj, proj, proj, proj, proj, bucket, rel_bias, sinks)


_SWA_SB = 8


def _swa_sample_kernel(q_ref, kn_ref, vn_ref, bk_ref, bv_ref, bkt_ref, rel_ref, sink_ref,
                       o_ref, wk_ref, wv_ref, bias_ref, *, ds, win):
    rows = G_A * ds
    nkeys = win + ds

    @pl.when(pl.program_id(0) == 0)
    def _():
        for h in range(N_HEADS_A):
            kv, g = divmod(h, G_A)
            bias_ref[kv, g * ds:(g + 1) * ds, :] = _bias_from_buckets(bkt_ref[...], rel_ref, h)

    t = lax.broadcasted_iota(jnp.int32, (rows, nkeys), 0) % ds
    s_idx = lax.broadcasted_iota(jnp.int32, (rows, nkeys), 1)
    dist = win + t - s_idx
    valid = (dist >= 0) & (dist < WINDOW)
    g_of_row = lax.broadcasted_iota(jnp.int32, (rows, 1), 0) // ds
    for kv in range(N_KV_A):
        sk = jnp.zeros((rows, 1), F32)
        for g in range(G_A):
            sk = jnp.where(g_of_row == g, sink_ref[kv * G_A + g], sk)
        csl = slice(kv * HEAD_DIM, (kv + 1) * HEAD_DIM)
        for b in range(_SWA_SB):
            rsl = slice(b * ds, (b + 1) * ds)
            q = jnp.concatenate(
                [q_ref[rsl, (kv * G_A + g) * HEAD_DIM:(kv * G_A + g + 1) * HEAD_DIM] for g in range(G_A)],
                axis=0).astype(BF16)
            kk = jnp.concatenate([bk_ref[b, :, csl], kn_ref[rsl, csl]], axis=0).astype(BF16)
            vv = jnp.concatenate([bv_ref[b, :, csl], vn_ref[rsl, csl]], axis=0).astype(BF16)
            s = _dot_nt(q, kk) * SCALE + bias_ref[kv]
            s = jnp.where(valid, s, NEG)
            m = jnp.maximum(jnp.max(s, axis=-1, keepdims=True), sk)
            e = jnp.exp(s - m)
            p = e / (jnp.sum(e, axis=-1, keepdims=True) + jnp.exp(sk - m))
            o = _dot(p.astype(BF16), vv)
            for g in range(G_A):
                o_ref[rsl, (kv * G_A + g) * HEAD_DIM:(kv * G_A + g + 1) * HEAD_DIM] = o[g * ds:(g + 1) * ds]
    for b in range(_SWA_SB):
        rsl = slice(b * ds, (b + 1) * ds)
        wk_ref[b, :win - ds, :] = bk_ref[b, ds:, :]
        wk_ref[b, win - ds:, :] = kn_ref[rsl, :]
        wv_ref[b, :win - ds, :] = bv_ref[b, ds:, :]
        wv_ref[b, win - ds:, :] = vn_ref[rsl, :]


def swa_sample(proj, row0, cache_k, cache_v, rel_bias, sinks, db, ds):
    win = cache_k.shape[1]
    rb = _SWA_SB * ds
    r0 = row0 // rb
    bucket = _t5_bucket(win + jnp.arange(ds)[:, None] - jnp.arange(win + ds)[None, :]).astype(jnp.int32)
    smem = pl.BlockSpec(memory_space=pltpu.SMEM)
    return pl.pallas_call(
        functools.partial(_swa_sample_kernel, ds=ds, win=win),
        grid=(db // _SWA_SB,),
        in_specs=[
            pl.BlockSpec((rb, D_A), lambda i: (r0 + i, 0)),
            pl.BlockSpec((rb, KVD_A), lambda i: (r0 + i, E_KA // KVD_A)),
            pl.BlockSpec((rb, KVD_A), lambda i: (r0 + i, E_VA // KVD_A)),
            pl.BlockSpec((_SWA_SB, win, KVD_A), lambda i: (i, 0, 0)),
            pl.BlockSpec((_SWA_SB, win, KVD_A), lambda i: (i, 0, 0)),
            pl.BlockSpec((ds, win + ds), lambda i: (0, 0)),
            smem, smem,
        ],
        out_specs=[
            pl.BlockSpec((rb, D_A), lambda i: (i, 0)),
            pl.BlockSpec((_SWA_SB, win, KVD_A), lambda i: (i, 0, 0)),
            pl.BlockSpec((_SWA_SB, win, KVD_A), lambda i: (i, 0, 0)),
        ],
        out_shape=[jax.ShapeDtypeStruct((db * ds, D_A), F32),
                   jax.ShapeDtypeStruct(cache_k.shape, F32),
                   jax.ShapeDtypeStruct(cache_v.shape, F32)],
        scratch_shapes=[pltpu.VMEM((N_KV_A, G_A * ds, win + ds), F32)],
        compiler_params=_cparams(1),
        name="swa_sample",
    )(proj, proj, proj, cache_k, cache_v, bucket, rel_bias, sinks)


_HG_CHUNK = 128
_HG_LEVELS = int(math.log2(_HG_CHUNK))


def _hgrn_constants(c):
    levels = int(math.log2(c))
    t = np.arange(c)
    mats = [(t[None, :] <= t[:, None])]
    lvl = np.full((c, c), -1, np.int32)
    lvl[t, t] = levels
    for l in range(levels):
        m = c >> (l + 1)
        pivot = (t // (2 * m)) * (2 * m) + m
        mats.append(t[None, :] <= pivot[:, None])
        same = (t[:, None] // (2 * m)) == (t[None, :] // (2 * m))
        pair = same & ((t[:, None] % (2 * m)) >= m) & ((t[None, :] % (2 * m)) < m)
        lvl[pair] = l
    return np.concatenate(mats, axis=0).astype(np.float32), lvl


def _column_of(row):
    n = row.shape[1]
    r = lax.broadcasted_iota(jnp.int32, (n, n), 0)
    c = lax.broadcasted_iota(jnp.int32, (n, n), 1)
    return jnp.sum(jnp.where(r == c, jnp.broadcast_to(row, (n, n)), 0.0), axis=1, keepdims=True)


def _hgrn_prompt_kernel(q_ref, f_ref, v_ref, cm_ref, lvl_ref, o_ref, s_out_ref, s_ref):
    c = _HG_CHUNK
    ci = pl.program_id(2)

    @pl.when(ci == 0)
    def _():
        s_ref[...] = jnp.zeros_like(s_ref)

    q = q_ref[...]
    f = f_ref[...]
    v = v_ref[...].astype(BF16)
    k = 1.0 - f
    sums = _dot_exact01(cm_ref[...], jnp.log(f))
    b = sums[:c]
    lvl = lvl_ref[...]
    a = jnp.where(lvl == _HG_LEVELS, _dot_nt(q.astype(BF16), k.astype(BF16)), 0.0)
    for l in range(_HG_LEVELS):
        e = jnp.exp(-jnp.abs(b - sums[(l + 1) * c:(l + 2) * c]))
        a = a + jnp.where(lvl == l, _dot_nt((q * e).astype(BF16), (k * e).astype(BF16)), 0.0)
    s0 = s_ref[...]
    o_ref[...] = _dot((q * jnp.exp(b)).astype(BF16), s0.astype(BF16)) + _dot(a.astype(BF16), v)
    b_last = b[c - 1:c, :]
    khat = (k * jnp.exp(b_last - b)).astype(BF16)
    s_new = _column_of(jnp.exp(b_last)) * s0 + _dot_tn(khat, v)
    s_ref[...] = s_new

    @pl.when(ci == pl.num_programs(2) - 1)
    def _():
        s_out_ref[0, 0] = s_new


def hgrn_prompt(proj, batch, seq):
    c = _HG_CHUNK
    nc = seq // c
    cmat, lvl = _hgrn_constants(c)
    qc, fc, vc = E_QB // HEAD_DIM, E_F // HEAD_DIM, E_VB // HEAD_DIM
    return pl.pallas_call(
        _hgrn_prompt_kernel,
        grid=(batch, N_HEADS_B, nc),
        in_specs=[
            pl.BlockSpec((c, HEAD_DIM), lambda b, h, i: (b * nc + i, qc + h)),
            pl.BlockSpec((c, HEAD_DIM), lambda b, h, i: (b * nc + i, fc + h)),
            pl.BlockSpec((c, HEAD_DIM), lambda b, h, i: (b * nc + i, vc + h)),
            pl.BlockSpec(cmat.shape, lambda b, h, i: (0, 0)),
            pl.BlockSpec(lvl.shape, lambda b, h, i: (0, 0)),
        ],
        out_specs=[
            pl.BlockSpec((c, HEAD_DIM), lambda b, h, i: (b * nc + i, h)),
            pl.BlockSpec((1, 1, HEAD_DIM, HEAD_DIM), lambda b, h, i: (b, h, 0, 0)),
        ],
        out_shape=[jax.ShapeDtypeStruct((batch * seq, D_B), F32),
                   jax.ShapeDtypeStruct((batch, N_HEADS_B, HEAD_DIM, HEAD_DIM), F32)],
        scratch_shapes=[pltpu.VMEM((HEAD_DIM, HEAD_DIM), F32)],
        compiler_params=_cparams(3),
        name="hgrn_prompt",
    )(proj, proj, proj, jnp.asarray(cmat), jnp.asarray(lvl))


_HG_SB = 8


def _hgrn_sample_kernel(q_ref, f_ref, v_ref, st_ref, o_ref, s_out_ref, *, ds):
    s_idx = lax.broadcasted_iota(jnp.int32, (ds, 1), 0)
    for sb in range(_HG_SB):
        rsl = slice(sb * ds, (sb + 1) * ds)
        q = q_ref[rsl, :]
        f = f_ref[rsl, :]
        v = v_ref[rsl, :]
        k = 1.0 - f
        lf = jnp.log(f)
        rows = [lf[0:1]]
        for t in range(1, ds):
            rows.append(rows[-1] + lf[t:t + 1])
        b = jnp.concatenate(rows, axis=0)
        s0 = st_ref[sb, 0]
        o_inter = _dot((q * jnp.exp(b)).astype(BF16), s0.astype(BF16))
        outs = []
        for t in range(ds):
            w = q[t:t + 1] * k * jnp.exp(jnp.minimum(b[t:t + 1] - b, 0.0))
            a_t = jnp.where(s_idx <= t, jnp.sum(w, axis=-1, keepdims=True), 0.0)
            outs.append(jnp.sum(a_t * v, axis=0, keepdims=True))
        o_ref[rsl, :] = o_inter + jnp.concatenate(outs, axis=0)
        b_last = b[ds - 1:ds, :]
        khat = (k * jnp.exp(b_last - b)).astype(BF16)
        s_out_ref[sb, 0] = _column_of(jnp.exp(b_last)) * s0 + _dot_tn(khat, v.astype(BF16))


def hgrn_sample(proj, row0, state, db, ds):
    rb = _HG_SB * ds
    r0 = row0 // rb
    qc, fc, vc = E_QB // HEAD_DIM, E_F // HEAD_DIM, E_VB // HEAD_DIM
    return pl.pallas_call(
        functools.partial(_hgrn_sample_kernel, ds=ds),
        grid=(db // _HG_SB, N_HEADS_B),
        in_specs=[
            pl.BlockSpec((rb, HEAD_DIM), lambda i, h: (r0 + i, qc + h)),
            pl.BlockSpec((rb, HEAD_DIM), lambda i, h: (r0 + i, fc + h)),
            pl.BlockSpec((rb, HEAD_DIM), lambda i, h: (r0 + i, vc + h)),
            pl.BlockSpec((_HG_SB, 1, HEAD_DIM, HEAD_DIM), lambda i, h: (i, h, 0, 0)),
        ],
        out_specs=[
            pl.BlockSpec((rb, HEAD_DIM), lambda i, h: (i, h)),
            pl.BlockSpec((_HG_SB, 1, HEAD_DIM, HEAD_DIM), lambda i, h: (i, h, 0, 0)),
        ],
        out_shape=[jax.ShapeDtypeStruct((db * ds, D_B), F32),
                   jax.ShapeDtypeStruct(state.shape, F32)],
        compiler_params=_cparams(2),
        name="hgrn_sample",
    )(proj, proj, proj, state)


_CS_BLK = 128


def _cumsum_kernel(lf_ref, tri_ref, c_ref):
    nblk = lf_ref.shape[1] // _CS_BLK

    def body(i, carry):
        r = pl.ds(pl.multiple_of(i * _CS_BLK, _CS_BLK), _CS_BLK)
        c = _dot_exact01(tri_ref[...], lf_ref[0, r, :]) + carry
        c_ref[0, r, :] = c
        return c[_CS_BLK - 1:_CS_BLK, :]

    lax.fori_loop(0, nblk, body, jnp.zeros((1, lf_ref.shape[2]), F32))


def cumsum_rows(lf):
    b, s, h = lf.shape
    t = np.arange(_CS_BLK)
    tri = jnp.asarray((t[None, :] <= t[:, None]).astype(np.float32))
    return pl.pallas_call(
        _cumsum_kernel,
        grid=(b,),
        in_specs=[pl.BlockSpec((1, s, h), lambda i: (i, 0, 0)),
                  pl.BlockSpec((_CS_BLK, _CS_BLK), lambda i: (0, 0))],
        out_specs=pl.BlockSpec((1, s, h), lambda i: (i, 0, 0)),
        out_shape=jax.ShapeDtypeStruct(lf.shape, F32),
        compiler_params=_cparams(1),
        name="fox_cumsum",
    )(lf, tri)


_FOX_TQ = 256
_FOX_TK = 256


def _fox_prompt_kernel(q_ref, k_ref, v_ref, cq_ref, ck_ref, o_ref, m_ref, l_ref, acc_ref):
    qi = pl.program_id(2)
    ki = pl.program_id(3)

    @pl.when(ki == 0)
    def _():
        m_ref[...] = jnp.full_like(m_ref, NEG)
        l_ref[...] = jnp.zeros_like(l_ref)
        acc_ref[...] = jnp.zeros_like(acc_ref)

    @pl.when(ki <= qi)
    def _():
        k = k_ref[...].astype(BF16)
        v = v_ref[...].astype(BF16)
        rowpos = qi * _FOX_TQ + lax.broadcasted_iota(jnp.int32, (_FOX_TQ, _FOX_TK), 0)
        colpos = ki * _FOX_TK + lax.broadcasted_iota(jnp.int32, (_FOX_TQ, _FOX_TK), 1)
        causal = colpos <= rowpos
        for g in range(G_C):
            sl = slice(g * HEAD_DIM, (g + 1) * HEAD_DIM)
            s = _dot_nt(q_ref[:, sl].astype(BF16), k) * SCALE
            s = s + cq_ref[0, 0, :, g:g + 1] - ck_ref[0, 0, g:g + 1, :]
            s = jnp.where(causal, s, NEG)
            m_prev = m_ref[g]
            m_new = jnp.maximum(m_prev, jnp.max(s, axis=-1, keepdims=True))
            alpha = jnp.exp(m_prev - m_new)
            p = jnp.exp(s - m_new)
            l_ref[g] = alpha * l_ref[g] + jnp.sum(p, axis=-1, keepdims=True)
            acc_ref[g] = alpha * acc_ref[g] + _dot(p.astype(BF16), v)
            m_ref[g] = m_new

    @pl.when(ki == pl.num_programs(3) - 1)
    def _():
        for g in range(G_C):
            o_ref[:, g * HEAD_DIM:(g + 1) * HEAD_DIM] = acc_ref[g] / l_ref[g]


def fox_prompt(projo, c, batch, seq):
    nq, nk = seq // _FOX_TQ, seq // _FOX_TK
    c4 = c.reshape(batch, seq, N_KV_C, G_C)
    c_q = jnp.transpose(c4, (0, 2, 1, 3))
    c_k = jnp.transpose(c4, (0, 2, 3, 1))
    kcol, vcol = O_K // HEAD_DIM, O_V // HEAD_DIM
    return pl.pallas_call(
        _fox_prompt_kernel,
        grid=(batch, N_KV_C, nq, nk),
        in_specs=[
            pl.BlockSpec((_FOX_TQ, G_C * HEAD_DIM), lambda b, k, i, j: (b * nq + i, k)),
            pl.BlockSpec((_FOX_TK, HEAD_DIM), lambda b, k, i, j: (b * nk + jnp.minimum(j, i), kcol + k)),
            pl.BlockSpec((_FOX_TK, HEAD_DIM), lambda b, k, i, j: (b * nk + jnp.minimum(j, i), vcol + k)),
            pl.BlockSpec((1, 1, _FOX_TQ, G_C), lambda b, k, i, j: (b, k, i, 0)),
            pl.BlockSpec((1, 1, G_C, _FOX_TK), lambda b, k, i, j: (b, k, 0, jnp.minimum(j, i))),
        ],
        out_specs=pl.BlockSpec((_FOX_TQ, G_C * HEAD_DIM), lambda b, k, i, j: (b * nq + i, k)),
        out_shape=jax.ShapeDtypeStruct((batch * seq, D_C), F32),
        scratch_shapes=[pltpu.VMEM((G_C, _FOX_TQ, 1), F32), pltpu.VMEM((G_C, _FOX_TQ, 1), F32),
                        pltpu.VMEM((G_C, _FOX_TQ, HEAD_DIM), F32)],
        compiler_params=_cparams(4),
        name="fox_prompt",
    )(projo, projo, projo, c_q, c_k)


_FOX_PP = 8


def _diag_blocks(x, rows):
    return jnp.concatenate(
        [x[kv * rows:(kv + 1) * rows, kv * HEAD_DIM:(kv + 1) * HEAD_DIM] for kv in range(N_KV_C)], axis=0)


def _fox_sample_kernel(pt_ref, q_ref, kn_ref, vn_ref, lfn_ref, tri_ref, ut_ref, *refs, ds, page):
    del pt_ref
    pp = _FOX_PP
    k_refs, v_refs, lf_refs = refs[:pp], refs[pp:2 * pp], refs[2 * pp:3 * pp]
    o_ref, qbd_ref, cq_ref, m_ref, l_ref, acc_ref, carry_ref = refs[3 * pp:]
    nrow = N_HEADS_C * ds
    rows_kv = G_C * ds
    step = pl.program_id(1)

    @pl.when(step == 0)
    def _():
        q = q_ref[0]
        rkv = lax.broadcasted_iota(jnp.int32, (nrow, KVD_C), 0) // rows_kv
        ckv = lax.broadcasted_iota(jnp.int32, (nrow, KVD_C), 1) // HEAD_DIM
        qbd = jnp.where(rkv == ckv, jnp.concatenate([q] * N_KV_C, axis=1), 0.0).astype(BF16)
        qbd_ref[...] = qbd
        cn = _dot_exact01_r(lfn_ref[0], tri_ref[...])
        t_row = lax.broadcasted_iota(jnp.int32, (nrow, ds), 0) % ds
        s_col = lax.broadcasted_iota(jnp.int32, (nrow, ds), 1)
        cq = jnp.sum(jnp.where(s_col == t_row, cn, 0.0), axis=-1, keepdims=True)
        cq_ref[...] = cq
        s = _dot_nt(qbd, kn_ref[...].astype(BF16)) * SCALE + cq - cn
        s = jnp.where(s_col <= t_row, s, NEG)
        m = jnp.max(s, axis=-1, keepdims=True)
        e = jnp.exp(s - m)
        m_ref[...] = m
        l_ref[...] = jnp.sum(e, axis=-1, keepdims=True)
        acc_ref[...] = _diag_blocks(_dot(e.astype(BF16), vn_ref[...].astype(BF16)), rows_kv)
        carry_ref[...] = jnp.zeros_like(carry_ref)

    qbd = qbd_ref[...]
    cq = cq_ref[...]
    for i in range(pp):
        lft = lf_refs[i][0]
        dex = _dot_exact01_r(lft, ut_ref[...]) + carry_ref[...]
        carry_ref[...] += jnp.sum(lft, axis=-1, keepdims=True)
        bias = jnp.broadcast_to(dex[:, None, :], (N_HEADS_C, ds, page)).reshape(nrow, page)
        s = _dot_nt(qbd, k_refs[i][0].astype(BF16)) * SCALE + cq + bias
        m_prev = m_ref[...]
        m_new = jnp.maximum(m_prev, jnp.max(s, axis=-1, keepdims=True))
        alpha = jnp.exp(m_prev - m_new)
        p = jnp.exp(s - m_new)
        l_ref[...] = alpha * l_ref[...] + jnp.sum(p, axis=-1, keepdims=True)
        pv = _dot(p.astype(BF16), v_refs[i][0].astype(BF16))
        acc_ref[...] = alpha * acc_ref[...] + _diag_blocks(pv, rows_kv)
        m_ref[...] = m_new

    @pl.when(step == pl.num_programs(1) - 1)
    def _():
        o_ref[0] = acc_ref[...] / l_ref[...]


def fox_sample(projo, row0, lf_s, cache_k, cache_v, cache_lf, page_table, db, ds):
    pool, page, _ = cache_k.shape
    n_pages = page_table.shape[1]
    pp = _FOX_PP
    nrow = N_HEADS_C * ds
    q = projo[row0:row0 + db * ds, O_Q:O_Q + D_C].reshape(db, ds, N_HEADS_C, HEAD_DIM)
    q = jnp.transpose(q, (0, 2, 1, 3)).reshape(db, nrow, HEAD_DIM)
    lfn = jnp.transpose(lf_s.reshape(db, ds, N_HEADS_C), (0, 2, 1))
    lfn = jnp.broadcast_to(lfn[:, :, None, :], (db, N_HEADS_C, ds, ds)).reshape(db, nrow, ds)
    cache_lft = jnp.transpose(cache_lf, (0, 2, 1))
    t = np.arange(ds)
    tri = jnp.asarray((t[:, None] <= t[None, :]).astype(np.float32))
    u = np.arange(page)
    ut = jnp.asarray((u[:, None] > u[None, :]).astype(np.float32))
    r0 = row0 // ds

    def page_map(i):
        return lambda b, p, pt: (pt[b, n_pages - 1 - (p * pp + i)], 0, 0)

    in_specs = [
        pl.BlockSpec((1, nrow, HEAD_DIM), lambda b, p, pt: (b, 0, 0)),
        pl.BlockSpec((ds, KVD_C), lambda b, p, pt: (r0 + b, O_K // KVD_C)),
        pl.BlockSpec((ds, KVD_C), lambda b, p, pt: (r0 + b, O_V // KVD_C)),
        pl.BlockSpec((1, nrow, ds), lambda b, p, pt: (b, 0, 0)),
        pl.BlockSpec((ds, ds), lambda b, p, pt: (0, 0)),
        pl.BlockSpec((page, page), lambda b, p, pt: (0, 0)),
    ]
    in_specs += [pl.BlockSpec((1, page, KVD_C), page_map(i)) for i in range(pp)]
    in_specs += [pl.BlockSpec((1, page, KVD_C), page_map(i)) for i in range(pp)]
    in_specs += [pl.BlockSpec((1, N_HEADS_C, page), page_map(i)) for i in range(pp)]
    out = pl.pallas_call(
        functools.partial(_fox_sample_kernel, ds=ds, page=page),
        grid_spec=pltpu.PrefetchScalarGridSpec(
            num_scalar_prefetch=1,
            grid=(db, n_pages // pp),
            in_specs=in_specs,
            out_specs=pl.BlockSpec((1, nrow, HEAD_DIM), lambda b, p, pt: (b, 0, 0)),
            scratch_shapes=[pltpu.VMEM((nrow, KVD_C), BF16), pltpu.VMEM((nrow, 1), F32),
                            pltpu.VMEM((nrow, 1), F32), pltpu.VMEM((nrow, 1), F32),
                            pltpu.VMEM((nrow, HEAD_DIM), F32), pltpu.VMEM((N_HEADS_C, 1), F32)],
        ),
        out_shape=jax.ShapeDtypeStruct((db, nrow, HEAD_DIM), F32),
        compiler_params=_cparams(2),
        name="fox_sample",
    )(page_table, q, projo, projo, lfn, tri, ut, *([cache_k] * pp), *([cache_v] * pp),
      *([cache_lft] * pp))
    out = jnp.transpose(out.reshape(db, N_HEADS_C, ds, HEAD_DIM), (0, 2, 1, 3))
    return out.reshape(db * ds, D_C)


def _moe_route_kernel(x_ref, g_ref, r_ref, h_ref, idx_ref, w_ref):
    hn = _rms(x_ref[...], g_ref[...])
    h_ref[...] = hn.astype(BF16)
    logits = jnp.dot(hn, r_ref[...], preferred_element_type=F32, precision=lax.Precision.HIGHEST)
    lane = lax.broadcasted_iota(jnp.int32, logits.shape, 1)
    m1 = jnp.max(logits, axis=-1, keepdims=True)
    i1 = jnp.min(jnp.where(logits == m1, lane, N_EXPERTS), axis=-1, keepdims=True)
    rest = jnp.where(lane == i1, -jnp.inf, logits)
    m2 = jnp.max(rest, axis=-1, keepdims=True)
    i2 = jnp.min(jnp.where(rest == m2, lane, N_EXPERTS), axis=-1, keepdims=True)
    e = jnp.exp(m2 - m1)
    slot = lax.broadcasted_iota(jnp.int32, idx_ref.shape, 1)
    idx_ref[...] = jnp.where(slot == 0, i1, i2)
    w_ref[...] = jnp.where(slot == 0, 1.0 / (1.0 + e), e / (1.0 + e))


def moe_route(x, g, router, tm):
    n, d = x.shape
    return pl.pallas_call(
        _moe_route_kernel,
        grid=(n // tm,),
        in_specs=[pl.BlockSpec((tm, d), lambda i: (i, 0)),
                  pl.BlockSpec((1, d), lambda i: (0, 0)),
                  pl.BlockSpec((d, N_EXPERTS), lambda i: (0, 0))],
        out_specs=[pl.BlockSpec((tm, d), lambda i: (i, 0)),
                   pl.BlockSpec((tm, TOP_K), lambda i: (i, 0)),
                   pl.BlockSpec((tm, TOP_K), lambda i: (i, 0))],
        out_shape=[jax.ShapeDtypeStruct((n, d), BF16),
                   jax.ShapeDtypeStruct((n, TOP_K), jnp.int32),
                   jax.ShapeDtypeStruct((n, TOP_K), F32)],
        compiler_params=_cparams(1),
        name="moe_route",
    )(x, g.reshape(1, d), router)


def _moe_ffn_kernel(te_ref, nu_ref, x_ref, gate_ref, wg_ref, wu_ref, wd_ref, o_ref, acc_ref):
    del te_ref
    i = pl.program_id(0)
    f = pl.program_id(1)

    @pl.when(f == 0)
    def _():
        acc_ref[...] = jnp.zeros_like(acc_ref)

    @pl.when(i < nu_ref[0])
    def _():
        x = x_ref[...]
        a = _dot(x, wg_ref[0].astype(BF16))
        u = _dot(x, wu_ref[0].astype(BF16))
        act = (_silu(a) * u).astype(BF16)
        acc_ref[...] += _dot(act, wd_ref[0].astype(BF16))

    @pl.when(f == pl.num_programs(1) - 1)
    def _():
        o_ref[...] = acc_ref[...] * gate_ref[...]


def moe_ffn(x_sorted, gate_sorted, tile_expert, n_used, w_gate, w_up, w_down, tm, tf=256):
    p, d = x_sorted.shape
    d_ff = w_gate.shape[2]
    nf = d_ff // tf

    def f_eff(i, f, nu):
        return jnp.where(i < nu[0], f, nf - 1)

    return pl.pallas_call(
        _moe_ffn_kernel,
        grid_spec=pltpu.PrefetchScalarGridSpec(
            num_scalar_prefetch=2,
            grid=(p // tm, nf),
            in_specs=[
                pl.BlockSpec((tm, d), lambda i, f, te, nu: (i, 0)),
                pl.BlockSpec((tm, 1), lambda i, f, te, nu: (i, 0)),
                pl.BlockSpec((1, d, tf), lambda i, f, te, nu: (te[i], 0, f_eff(i, f, nu))),
                pl.BlockSpec((1, d, tf), lambda i, f, te, nu: (te[i], 0, f_eff(i, f, nu))),
                pl.BlockSpec((1, tf, d), lambda i, f, te, nu: (te[i], f_eff(i, f, nu), 0)),
            ],
            out_specs=pl.BlockSpec((tm, d), lambda i, f, te, nu: (i, 0)),
            scratch_shapes=[pltpu.VMEM((tm, d), F32)],
        ),
        out_shape=jax.ShapeDtypeStruct((p, d), F32),
        compiler_params=_cparams(2),
        name="moe_ffn",
    )(tile_expert, n_used, x_sorted, gate_sorted, w_gate, w_up, w_down)


def _route_plan(idx, w, tm):
    n = idx.shape[0]
    flat_e = idx.reshape(-1)
    onehot = (flat_e[:, None] == jnp.arange(N_EXPERTS, dtype=jnp.int32)[None, :]).astype(jnp.int32)
    csum = jnp.cumsum(onehot, axis=0)
    rank = jnp.take_along_axis(csum, flat_e[:, None], axis=1)[:, 0] - 1
    count = csum[-1]
    tiles_e = (count + tm - 1) // tm
    tile_end = jnp.cumsum(tiles_e)
    pos = (tile_end - tiles_e)[flat_e] * tm + rank
    n_tiles = (TOP_K * n) // tm + N_EXPERTS
    token = jnp.zeros((n_tiles * tm,), jnp.int32).at[pos].set(jnp.arange(TOP_K * n, dtype=jnp.int32) // TOP_K)
    gate = jnp.zeros((n_tiles * tm,), F32).at[pos].set(w.reshape(-1))
    tile_expert = jnp.minimum(
        jnp.searchsorted(tile_end, jnp.arange(n_tiles, dtype=jnp.int32), side="right"),
        N_EXPERTS - 1).astype(jnp.int32)
    return pos.reshape(n, TOP_K), token, gate.reshape(-1, 1), tile_expert, tile_end[-1:].astype(jnp.int32)


def moe_layer(x, g, router, w_gate, w_up, w_down, tm_route, tm):
    h, idx, w = moe_route(x, g, router, tm_route)
    pos, token, gate, tile_expert, n_used = _route_plan(idx, w, tm)
    y_sorted = moe_ffn(jnp.take(h, token, axis=0), gate, tile_expert, n_used, w_gate, w_up, w_down, tm)
    return x + jnp.take(y_sorted, pos[:, 0], axis=0) + jnp.take(y_sorted, pos[:, 1], axis=0)


def _row_tile(n, want):
    t = min(want, n)
    while n % t:
        t //= 2
    return t


def kernel(x_prompt, x_sample, cache_k_win, cache_v_win, state_hgrn, cache_k_fox, cache_v_fox, cache_logf_fox, page_table, rel_bias, norm_mix_e, w_in_e, q_norm_a, k_norm_a, sinks_a, lb_b, o_norm_b, w_out_e, norm_ffn_e, w_gate_e, w_up_e, w_down_e, norm_mix_o, w_in_o, fgate_bias_c, q_norm_c, k_norm_c, w_out_o, norm_ffn_o, router_o, w_gate_x, w_up_x, w_down_x):
    batch, seq, d = x_prompt.shape
    db, ds, _ = x_sample.shape
    n_p, n_s = batch * seq, db * ds
    n = n_p + n_s
    win = cache_k_win.shape[2]
    pool, page = cache_k_fox.shape[1], cache_k_fox.shape[2]
    tm_big = _row_tile(n, 1024)
    tm_ffn = _row_tile(n, 512)

    x = jnp.concatenate([x_prompt.reshape(n_p, d), x_sample.reshape(n_s, d)], axis=0)

    proj = even_project(x, norm_mix_e[0], w_in_e[0], q_norm_a[0], k_norm_a[0], lb_b, tm_big)
    oa_p = swa_prompt(proj, rel_bias, sinks_a[0], batch, seq)
    oa_s, wk_s, wv_s = swa_sample(proj, n_p, cache_k_win[0].reshape(db, win, KVD_A),
                                  cache_v_win[0].reshape(db, win, KVD_A), rel_bias, sinks_a[0], db, ds)
    ob_p, hg_p = hgrn_prompt(proj, batch, seq)
    ob_s, hg_s = hgrn_sample(proj, n_p, state_hgrn[0], db, ds)
    oa = jnp.concatenate([oa_p, oa_s], axis=0)
    ob = jnp.concatenate([ob_p, ob_s], axis=0)
    y = even_merge(oa, ob, proj, o_norm_b[0], w_out_e[0], x, tm_big)
    y = dense_ffn(y, norm_ffn_e[0], w_gate_e[0], w_up_e[0], w_down_e[0], tm_ffn)

    projo, lf = odd_project(y, norm_mix_o[0], w_in_o[0], q_norm_c[0], k_norm_c[0], fgate_bias_c[0], tm_big)
    c = cumsum_rows(lf[:n_p].reshape(batch, seq, N_HEADS_C))
    oc_p = fox_prompt(projo, c, batch, seq)
    oc_s = fox_sample(projo, n_p, lf[n_p:], cache_k_fox[0].reshape(pool, page, KVD_C),
                      cache_v_fox[0].reshape(pool, page, KVD_C), cache_logf_fox[0], page_table, db, ds)
    y = out_project(jnp.concatenate([oc_p, oc_s], axis=0), w_out_o[0], y, tm_big)
    y = moe_layer(y, norm_ffn_o[0], router_o[0], w_gate_x[0], w_up_x[0], w_down_x[0], tm_big, tm_ffn)

    def prompt_tail(col, width, heads):
        t = proj[:n_p, col:col + width].reshape(batch, seq, heads, HEAD_DIM)
        return t[:, seq - WINDOW:][None]

    return (
        y[:n_p].reshape(batch, seq, d),
        y[n_p:].reshape(db, ds, d),
        prompt_tail(E_KA, KVD_A, N_KV_A),
        prompt_tail(E_VA, KVD_A, N_KV_A),
        wk_s.reshape(1, db, win, N_KV_A, HEAD_DIM),
        wv_s.reshape(1, db, win, N_KV_A, HEAD_DIM),
        hg_p[None],
        hg_s[None],
        projo[:n_p, O_K:O_K + KVD_C].reshape(1, batch, seq, N_KV_C, HEAD_DIM),
        projo[:n_p, O_V:O_V + KVD_C].reshape(1, batch, seq, N_KV_C, HEAD_DIM),
        lf[:n_p].reshape(1, batch, seq, N_HEADS_C),
        projo[n_p:, O_K:O_K + KVD_C].reshape(1, db, ds, N_KV_C, HEAD_DIM),
        projo[n_p:, O_V:O_V + KVD_C].reshape(1, db, ds, N_KV_C, HEAD_DIM),
        lf[n_p:].reshape(1, db, ds, N_HEADS_C),
    )
```

```python
import functools
import math

import numpy as np
import jax
import jax.numpy as jnp
from jax import lax
from jax.experimental import pallas as pl
from jax.experimental.pallas import tpu as pltpu

F32 = jnp.float32
BF16 = jnp.bfloat16

HEAD_DIM = 128
N_HEADS_A, N_KV_A = 8, 2
G_A = N_HEADS_A // N_KV_A
WINDOW = 128
REL_BUCKETS, REL_MAX_DIST = 32, 128
N_HEADS_B = 8
N_HEADS_C, N_KV_C = 16, 4
G_C = N_HEADS_C // N_KV_C
N_EXPERTS, TOP_K = 8, 2
EPS = 1e-6
SCALE = HEAD_DIM ** -0.5
NEG = -1e30

D_A = N_HEADS_A * HEAD_DIM
KVD_A = N_KV_A * HEAD_DIM
D_B = N_HEADS_B * HEAD_DIM
D_C = N_HEADS_C * HEAD_DIM
KVD_C = N_KV_C * HEAD_DIM

E_QA, E_QB, E_F, E_VB, E_GB = 0, D_A, D_A + D_B, D_A + 2 * D_B, D_A + 3 * D_B
E_KA = D_A + 4 * D_B
E_VA = E_KA + KVD_A
E_OUT = E_VA + KVD_A
O_Q, O_K, O_V = 0, D_C, D_C + KVD_C
O_OUT = D_C + 2 * KVD_C

VMEM_LIMIT = 56 * 1024 * 1024


def _cparams(n_axes):
    return pltpu.CompilerParams(dimension_semantics=("arbitrary",) * n_axes,
                                vmem_limit_bytes=VMEM_LIMIT)


def _rms(x, gain):
    return x * lax.rsqrt(jnp.mean(x * x, axis=-1, keepdims=True) + EPS) * gain


def _silu(x):
    return x * (1.0 / (1.0 + jnp.exp(-x)))


def _dot(a, b):
    return jnp.dot(a, b, preferred_element_type=F32)


def _dot_nt(a, b):
    return lax.dot_general(a, b, (((1,), (1,)), ((), ())), preferred_element_type=F32)


def _dot_tn(a, b):
    return lax.dot_general(a, b, (((0,), (0,)), ((), ())), preferred_element_type=F32)


def _dot_exact01(m01, x):
    x1 = x.astype(BF16)
    r1 = x - x1.astype(F32)
    x2 = r1.astype(BF16)
    x3 = (r1 - x2.astype(F32)).astype(BF16)
    m = m01.astype(BF16)
    return _dot(m, x1) + _dot(m, x2) + _dot(m, x3)


def _dot_exact01_r(x, m01):
    x1 = x.astype(BF16)
    r1 = x - x1.astype(F32)
    x2 = r1.astype(BF16)
    x3 = (r1 - x2.astype(F32)).astype(BF16)
    m = m01.astype(BF16)
    return _dot(x1, m) + _dot(x2, m) + _dot(x3, m)


_TN = 256


def _even_out_tile(j):
    return jnp.where(j < 4, j, jnp.where(j < 6, j + 16, j - 2))


def _even_proj_kernel(x_ref, g_ref, w_ref, qn_ref, kn_ref, lb_ref, o_ref, hn_ref):
    j = pl.program_id(1)

    @pl.when(j == 0)
    def _():
        hn_ref[...] = _rms(x_ref[...], g_ref[...]).astype(BF16)

    acc = _dot(hn_ref[...], w_ref[...].astype(BF16))

    def head_norm(gain):
        for h in range(_TN // HEAD_DIM):
            sl = slice(h * HEAD_DIM, (h + 1) * HEAD_DIM)
            o_ref[:, sl] = _rms(acc[:, sl], gain)

    @pl.when(j < 4)
    def _():
        head_norm(qn_ref[...])

    @pl.when(j == 4)
    def _():
        head_norm(kn_ref[...])

    @pl.when((j == 5) | (j >= 14))
    def _():
        o_ref[...] = acc

    @pl.when((j >= 6) & (j < 10))
    def _():
        o_ref[...] = _silu(acc)

    @pl.when((j >= 10) & (j < 14))
    def _():
        lb = lb_ref[...]
        e = jnp.exp(lb - jnp.max(lb, axis=0, keepdims=True))
        lb0 = e[0:1, :] / jnp.sum(e, axis=0, keepdims=True)
        o_ref[...] = lb0 + (1.0 - lb0) * (1.0 / (1.0 + jnp.exp(-acc)))


def even_project(x, g, w_in, q_norm, k_norm, lb_b, tm):
    n, d = x.shape
    n_tiles = w_in.shape[1] // _TN
    return pl.pallas_call(
        _even_proj_kernel,
        grid=(n // tm, n_tiles),
        in_specs=[
            pl.BlockSpec((tm, d), lambda i, j: (i, 0)),
            pl.BlockSpec((1, d), lambda i, j: (0, 0)),
            pl.BlockSpec((d, _TN), lambda i, j: (0, j)),
            pl.BlockSpec((1, HEAD_DIM), lambda i, j: (0, 0)),
            pl.BlockSpec((1, HEAD_DIM), lambda i, j: (0, 0)),
            pl.BlockSpec((lb_b.shape[0], _TN), lambda i, j: (0, jnp.clip(j - 10, 0, 3))),
        ],
        out_specs=pl.BlockSpec((tm, _TN), lambda i, j: (i, _even_out_tile(j))),
        out_shape=jax.ShapeDtypeStruct((n, E_OUT), F32),
        scratch_shapes=[pltpu.VMEM((tm, d), BF16)],
        compiler_params=_cparams(2),
        name="even_project",
    )(x, g.reshape(1, d), w_in, q_norm.reshape(1, -1), k_norm.reshape(1, -1), lb_b)


def _even_merge_kernel(oa_ref, ob_ref, gb_ref, on_ref, w_ref, x_ref, o_ref, lhs_ref):
    j = pl.program_id(1)

    @pl.when(j == 0)
    def _():
        lhs_ref[:, :D_A] = oa_ref[...].astype(BF16)
        for h in range(N_HEADS_B):
            sl = slice(h * HEAD_DIM, (h + 1) * HEAD_DIM)
            obn = _rms(ob_ref[:, sl], on_ref[...]) * _silu(gb_ref[:, sl])
            lhs_ref[:, D_A + h * HEAD_DIM:D_A + (h + 1) * HEAD_DIM] = obn.astype(BF16)

    o_ref[...] = x_ref[...] + _dot(lhs_ref[...], w_ref[...].astype(BF16))


def even_merge(oa, ob, proj, o_norm, w_out, x, tm, tn=512):
    n, d = x.shape
    return pl.pallas_call(
        _even_merge_kernel,
        grid=(n // tm, d // tn),
        in_specs=[
            pl.BlockSpec((tm, D_A), lambda i, j: (i, 0)),
            pl.BlockSpec((tm, D_B), lambda i, j: (i, 0)),
            pl.BlockSpec((tm, D_B), lambda i, j: (i, E_GB // D_B)),
            pl.BlockSpec((1, HEAD_DIM), lambda i, j: (0, 0)),
            pl.BlockSpec((D_A + D_B, tn), lambda i, j: (0, j)),
            pl.BlockSpec((tm, tn), lambda i, j: (i, j)),
        ],
        out_specs=pl.BlockSpec((tm, tn), lambda i, j: (i, j)),
        out_shape=jax.ShapeDtypeStruct((n, d), F32),
        scratch_shapes=[pltpu.VMEM((tm, D_A + D_B), BF16)],
        compiler_params=_cparams(2),
        name="even_merge",
    )(oa, ob, proj, o_norm.reshape(1, -1), w_out, x)


_DOWN_CHUNK = 512


def _accumulate_down(o_ref, rows, act, wd):
    for c in range(0, o_ref.shape[1], _DOWN_CHUNK):
        o_ref[rows, c:c + _DOWN_CHUNK] += _dot(act, wd[:, c:c + _DOWN_CHUNK])


def _dense_ffn_kernel(x_ref, g_ref, wg_ref, wu_ref, wd_ref, o_ref, hn_ref):
    @pl.when(pl.program_id(1) == 0)
    def _():
        hn_ref[...] = _rms(x_ref[...], g_ref[...]).astype(BF16)
        o_ref[...] = x_ref[...]

    hn = hn_ref[...]
    a = _dot(hn, wg_ref[...].astype(BF16))
    u = _dot(hn, wu_ref[...].astype(BF16))
    act = (_silu(a) * u).astype(BF16)
    _accumulate_down(o_ref, slice(None), act, wd_ref[...].astype(BF16))


def dense_ffn(x, g, w_gate, w_up, w_down, tm, tf=256):
    n, d = x.shape
    d_ff = w_gate.shape[1]
    return pl.pallas_call(
        _dense_ffn_kernel,
        grid=(n // tm, d_ff // tf),
        in_specs=[
            pl.BlockSpec((tm, d), lambda i, f: (i, 0)),
            pl.BlockSpec((1, d), lambda i, f: (0, 0)),
            pl.BlockSpec((d, tf), lambda i, f: (0, f)),
            pl.BlockSpec((d, tf), lambda i, f: (0, f)),
            pl.BlockSpec((tf, d), lambda i, f: (f, 0)),
        ],
        out_specs=pl.BlockSpec((tm, d), lambda i, f: (i, 0)),
        out_shape=jax.ShapeDtypeStruct((n, d), F32),
        scratch_shapes=[pltpu.VMEM((tm, d), BF16)],
        compiler_params=_cparams(2),
        name="dense_ffn",
    )(x, g.reshape(1, d), w_gate, w_up, w_down)


def _odd_proj_kernel(x_ref, g_ref, w_ref, wf_ref, qn_ref, kn_ref, fb_ref, o_ref, lf_ref, hn_ref):
    j = pl.program_id(1)
    nq = D_C // _TN
    nk = KVD_C // _TN

    @pl.when(j == 0)
    def _():
        hn = _rms(x_ref[...], g_ref[...]).astype(BF16)
        hn_ref[...] = hn
        z = _dot(hn, wf_ref[...].astype(BF16)) + fb_ref[...]
        lf_ref[...] = jnp.minimum(z, 0.0) - jnp.log(1.0 + jnp.exp(-jnp.abs(z)))

    acc = _dot(hn_ref[...], w_ref[...].astype(BF16))

    def head_norm(gain):
        for h in range(_TN // HEAD_DIM):
            sl = slice(h * HEAD_DIM, (h + 1) * HEAD_DIM)
            o_ref[:, sl] = _rms(acc[:, sl], gain)

    @pl.when(j < nq)
    def _():
        head_norm(qn_ref[...])

    @pl.when((j >= nq) & (j < nq + nk))
    def _():
        head_norm(kn_ref[...])

    @pl.when(j >= nq + nk)
    def _():
        o_ref[...] = acc


def odd_project(x, g, w_in, q_norm, k_norm, f_bias, tm):
    n, d = x.shape
    w_f = w_in[:, O_OUT:]
    return pl.pallas_call(
        _odd_proj_kernel,
        grid=(n // tm, O_OUT // _TN),
        in_specs=[
            pl.BlockSpec((tm, d), lambda i, j: (i, 0)),
            pl.BlockSpec((1, d), lambda i, j: (0, 0)),
            pl.BlockSpec((d, _TN), lambda i, j: (0, j)),
            pl.BlockSpec((d, N_HEADS_C), lambda i, j: (0, 0)),
            pl.BlockSpec((1, HEAD_DIM), lambda i, j: (0, 0)),
            pl.BlockSpec((1, HEAD_DIM), lambda i, j: (0, 0)),
            pl.BlockSpec((1, N_HEADS_C), lambda i, j: (0, 0)),
        ],
        out_specs=[
            pl.BlockSpec((tm, _TN), lambda i, j: (i, j)),
            pl.BlockSpec((tm, N_HEADS_C), lambda i, j: (i, 0)),
        ],
        out_shape=[jax.ShapeDtypeStruct((n, O_OUT), F32),
                   jax.ShapeDtypeStruct((n, N_HEADS_C), F32)],
        scratch_shapes=[pltpu.VMEM((tm, d), BF16)],
        compiler_params=_cparams(2),
        name="odd_project",
    )(x, g.reshape(1, d), w_in, w_f, q_norm.reshape(1, -1), k_norm.reshape(1, -1),
      f_bias.reshape(1, -1))


def _out_proj_kernel(a_ref, w_ref, x_ref, o_ref, lhs_ref):
    @pl.when(pl.program_id(1) == 0)
    def _():
        lhs_ref[...] = a_ref[...].astype(BF16)

    o_ref[...] = x_ref[...] + _dot(lhs_ref[...], w_ref[...].astype(BF16))


def out_project(a, w, x, tm, tn=512):
    n, d = x.shape
    k = a.shape[1]
    return pl.pallas_call(
        _out_proj_kernel,
        grid=(n // tm, d // tn),
        in_specs=[
            pl.BlockSpec((tm, k), lambda i, j: (i, 0)),
            pl.BlockSpec((k, tn), lambda i, j: (0, j)),
            pl.BlockSpec((tm, tn), lambda i, j: (i, j)),
        ],
        out_specs=pl.BlockSpec((tm, tn), lambda i, j: (i, j)),
        out_shape=jax.ShapeDtypeStruct((n, d), F32),
        scratch_shapes=[pltpu.VMEM((tm, k), BF16)],
        compiler_params=_cparams(2),
        name="out_project",
    )(a, w, x)


def _t5_bucket(dist):
    d = jnp.maximum(dist, 0)
    max_exact = REL_BUCKETS // 2
    ratio = jnp.log(jnp.maximum(d, 1).astype(F32) / max_exact) / math.log(REL_MAX_DIST / max_exact)
    large = jnp.minimum(max_exact + (ratio * (REL_BUCKETS - max_exact)).astype(jnp.int32),
                        REL_BUCKETS - 1)
    return jnp.where(d < max_exact, d, large)


def _bias_from_buckets(bucket, rel_ref, h):
    out = jnp.zeros(bucket.shape, F32)
    for b in range(REL_BUCKETS):
        out = jnp.where(bucket == b, rel_ref[b, h], out)
    return out


def _swa_prompt_kernel(q_ref, kp_ref, kc_ref, vp_ref, vc_ref, bkt_ref, rel_ref, sink_ref,
                       o_ref, bias_ref):
    n = pl.program_id(1)
    kv = pl.program_id(2)

    @pl.when((pl.program_id(0) == 0) & (n == 0) & (kv == 0))
    def _():
        for h in range(N_HEADS_A):
            bias_ref[h] = _bias_from_buckets(bkt_ref[...], rel_ref, h)

    kk = jnp.concatenate([kp_ref[...], kc_ref[...]], axis=0).astype(BF16)
    vv = jnp.concatenate([vp_ref[...], vc_ref[...]], axis=0).astype(BF16)
    row = lax.broadcasted_iota(jnp.int32, (WINDOW, 2 * WINDOW), 0)
    col = lax.broadcasted_iota(jnp.int32, (WINDOW, 2 * WINDOW), 1)
    dist = WINDOW + row - col
    valid = (dist >= 0) & (dist < WINDOW) & ((n > 0) | (col >= WINDOW))
    for g in range(G_A):
        h = kv * G_A + g
        sl = slice(g * HEAD_DIM, (g + 1) * HEAD_DIM)
        s = _dot_nt(q_ref[:, sl].astype(BF16), kk) * SCALE + bias_ref[h]
        s = jnp.where(valid, s, NEG)
        sk = sink_ref[h]
        m = jnp.maximum(jnp.max(s, axis=-1, keepdims=True), sk)
        e = jnp.exp(s - m)
        p = e / (jnp.sum(e, axis=-1, keepdims=True) + jnp.exp(sk - m))
        o_ref[:, sl] = _dot(p.astype(BF16), vv)


def swa_prompt(proj, rel_bias, sinks, batch, seq):
    nb = seq // WINDOW
    i = jnp.arange(WINDOW)[:, None]
    j = jnp.arange(2 * WINDOW)[None, :]
    bucket = _t5_bucket(WINDOW + i - j).astype(jnp.int32)
    kcol, vcol = E_KA // HEAD_DIM, E_VA // HEAD_DIM
    smem = pl.BlockSpec(memory_space=pltpu.SMEM)
    return pl.pallas_call(
        _swa_prompt_kernel,
        grid=(batch, nb, N_KV_A),
        in_specs=[
            pl.BlockSpec((WINDOW, G_A * HEAD_DIM), lambda b, n, k: (b * nb + n, k)),
            pl.BlockSpec((WINDOW, HEAD_DIM), lambda b, n, k: (b * nb + jnp.maximum(n - 1, 0), kcol + k)),
            pl.BlockSpec((WINDOW, HEAD_DIM), lambda b, n, k: (b * nb + n, kcol + k)),
            pl.BlockSpec((WINDOW, HEAD_DIM), lambda b, n, k: (b * nb + jnp.maximum(n - 1, 0), vcol + k)),
            pl.BlockSpec((WINDOW, HEAD_DIM), lambda b, n, k: (b * nb + n, vcol + k)),
            pl.BlockSpec((WINDOW, 2 * WINDOW), lambda b, n, k: (0, 0)),
            smem, smem,
        ],
        out_specs=pl.BlockSpec((WINDOW, G_A * HEAD_DIM), lambda b, n, k: (b * nb + n, k)),
        out_shape=jax.ShapeDtypeStruct((batch * seq, D_A), F32),
        scratch_shapes=[pltpu.VMEM((N_HEADS_A, WINDOW, 2 * WINDOW), F32)],
        compiler_params=_cparams(3),
        name="swa_prompt",
    )(proj, proj, proj, proj, proj, bucket, rel_bias, sinks)


_SWA_SB = 8


def _swa_sample_kernel(q_ref, kn_ref, vn_ref, bk_ref, bv_ref, bkt_ref, rel_ref, sink_ref,
                       o_ref, wk_ref, wv_ref, bias_ref, *, ds, win):
    rows = G_A * ds
    nkeys = win + ds

    @pl.when(pl.program_id(0) == 0)
    def _():
        for h in range(N_HEADS_A):
            kv, g = divmod(h, G_A)
            bias_ref[kv, g * ds:(g + 1) * ds, :] = _bias_from_buckets(bkt_ref[...], rel_ref, h)

    t = lax.broadcasted_iota(jnp.int32, (rows, nkeys), 0) % ds
    s_idx = lax.broadcasted_iota(jnp.int32, (rows, nkeys), 1)
    dist = win + t - s_idx
    valid = (dist >= 0) & (dist < WINDOW)
    g_of_row = lax.broadcasted_iota(jnp.int32, (rows, 1), 0) // ds
    for kv in range(N_KV_A):
        sk = jnp.zeros((rows, 1), F32)
        for g in range(G_A):
            sk = jnp.where(g_of_row == g, sink_ref[kv * G_A + g], sk)
        csl = slice(kv * HEAD_DIM, (kv + 1) * HEAD_DIM)
        for b in range(_SWA_SB):
            rsl = slice(b * ds, (b + 1) * ds)
            q = jnp.concatenate(
                [q_ref[rsl, (kv * G_A + g) * HEAD_DIM:(kv * G_A + g + 1) * HEAD_DIM] for g in range(G_A)],
                axis=0).astype(BF16)
            kk = jnp.concatenate([bk_ref[b, :, csl], kn_ref[rsl, csl]], axis=0).astype(BF16)
            vv = jnp.concatenate([bv_ref[b, :, csl], vn_ref[rsl, csl]], axis=0).astype(BF16)
            s = _dot_nt(q, kk) * SCALE + bias_ref[kv]
            s = jnp.where(valid, s, NEG)
            m = jnp.maximum(jnp.max(s, axis=-1, keepdims=True), sk)
            e = jnp.exp(s - m)
            p = e / (jnp.sum(e, axis=-1, keepdims=True) + jnp.exp(sk - m))
            o = _dot(p.astype(BF16), vv)
            for g in range(G_A):
                o_ref[rsl, (kv * G_A + g) * HEAD_DIM:(kv * G_A + g + 1) * HEAD_DIM] = o[g * ds:(g + 1) * ds]
    for b in range(_SWA_SB):
        rsl = slice(b * ds, (b + 1) * ds)
        wk_ref[b, :win - ds, :] = bk_ref[b, ds:, :]
        wk_ref[b, win - ds:, :] = kn_ref[rsl, :]
        wv_ref[b, :win - ds, :] = bv_ref[b, ds:, :]
        wv_ref[b, win - ds:, :] = vn_ref[rsl, :]


def swa_sample(proj, row0, cache_k, cache_v, rel_bias, sinks, db, ds):
    win = cache_k.shape[1]
    rb = _SWA_SB * ds
    r0 = row0 // rb
    bucket = _t5_bucket(win + jnp.arange(ds)[:, None] - jnp.arange(win + ds)[None, :]).astype(jnp.int32)
    smem = pl.BlockSpec(memory_space=pltpu.SMEM)
    return pl.pallas_call(
        functools.partial(_swa_sample_kernel, ds=ds, win=win),
        grid=(db // _SWA_SB,),
        in_specs=[
            pl.BlockSpec((rb, D_A), lambda i: (r0 + i, 0)),
            pl.BlockSpec((rb, KVD_A), lambda i: (r0 + i, E_KA // KVD_A)),
            pl.BlockSpec((rb, KVD_A), lambda i: (r0 + i, E_VA // KVD_A)),
            pl.BlockSpec((_SWA_SB, win, KVD_A), lambda i: (i, 0, 0)),
            pl.BlockSpec((_SWA_SB, win, KVD_A), lambda i: (i, 0, 0)),
            pl.BlockSpec((ds, win + ds), lambda i: (0, 0)),
            smem, smem,
        ],
        out_specs=[
            pl.BlockSpec((rb, D_A), lambda i: (i, 0)),
            pl.BlockSpec((_SWA_SB, win, KVD_A), lambda i: (i, 0, 0)),
            pl.BlockSpec((_SWA_SB, win, KVD_A), lambda i: (i, 0, 0)),
        ],
        out_shape=[jax.ShapeDtypeStruct((db * ds, D_A), F32),
                   jax.ShapeDtypeStruct(cache_k.shape, F32),
                   jax.ShapeDtypeStruct(cache_v.shape, F32)],
        scratch_shapes=[pltpu.VMEM((N_KV_A, G_A * ds, win + ds), F32)],
        compiler_params=_cparams(1),
        name="swa_sample",
    )(proj, proj, proj, cache_k, cache_v, bucket, rel_bias, sinks)


_HG_CHUNK = 128
_HG_LEVELS = int(math.log2(_HG_CHUNK))


def _hgrn_constants(c):
    levels = int(math.log2(c))
    t = np.arange(c)
    mats = [(t[None, :] <= t[:, None])]
    lvl = np.full((c, c), -1, np.int32)
    lvl[t, t] = levels
    for l in range(levels):
        m = c >> (l + 1)
        pivot = (t // (2 * m)) * (2 * m) + m
        mats.append(t[None, :] <= pivot[:, None])
        same = (t[:, None] // (2 * m)) == (t[None, :] // (2 * m))
        pair = same & ((t[:, None] % (2 * m)) >= m) & ((t[None, :] % (2 * m)) < m)
        lvl[pair] = l
    return np.concatenate(mats, axis=0).astype(np.float32), lvl


def _column_of(row):
    n = row.shape[1]
    r = lax.broadcasted_iota(jnp.int32, (n, n), 0)
    c = lax.broadcasted_iota(jnp.int32, (n, n), 1)
    return jnp.sum(jnp.where(r == c, jnp.broadcast_to(row, (n, n)), 0.0), axis=1, keepdims=True)


def _hgrn_prompt_kernel(q_ref, f_ref, v_ref, cm_ref, lvl_ref, o_ref, s_out_ref, s_ref):
    c = _HG_CHUNK
    ci = pl.program_id(2)

    @pl.when(ci == 0)
    def _():
        s_ref[...] = jnp.zeros_like(s_ref)

    q = q_ref[...]
    f = f_ref[...]
    v = v_ref[...].astype(BF16)
    k = 1.0 - f
    sums = _dot_exact01(cm_ref[...], jnp.log(f))
    b = sums[:c]
    lvl = lvl_ref[...]
    a = jnp.where(lvl == _HG_LEVELS, _dot_nt(q.astype(BF16), k.astype(BF16)), 0.0)
    for l in range(_HG_LEVELS):
        e = jnp.exp(-jnp.abs(b - sums[(l + 1) * c:(l + 2) * c]))
        a = a + jnp.where(lvl == l, _dot_nt((q * e).astype(BF16), (k * e).astype(BF16)), 0.0)
    s0 = s_ref[...]
    o_ref[...] = _dot((q * jnp.exp(b)).astype(BF16), s0.astype(BF16)) + _dot(a.astype(BF16), v)
    b_last = b[c - 1:c, :]
    khat = (k * jnp.exp(b_last - b)).astype(BF16)
    s_new = _column_of(jnp.exp(b_last)) * s0 + _dot_tn(khat, v)
    s_ref[...] = s_new

    @pl.when(ci == pl.num_programs(2) - 1)
    def _():
        s_out_ref[0, 0] = s_new


def hgrn_prompt(proj, batch, seq):
    c = _HG_CHUNK
    nc = seq // c
    cmat, lvl = _hgrn_constants(c)
    qc, fc, vc = E_QB // HEAD_DIM, E_F // HEAD_DIM, E_VB // HEAD_DIM
    return pl.pallas_call(
        _hgrn_prompt_kernel,
        grid=(batch, N_HEADS_B, nc),
        in_specs=[
            pl.BlockSpec((c, HEAD_DIM), lambda b, h, i: (b * nc + i, qc + h)),
            pl.BlockSpec((c, HEAD_DIM), lambda b, h, i: (b * nc + i, fc + h)),
            pl.BlockSpec((c, HEAD_DIM), lambda b, h, i: (b * nc + i, vc + h)),
            pl.BlockSpec(cmat.shape, lambda b, h, i: (0, 0)),
            pl.BlockSpec(lvl.shape, lambda b, h, i: (0, 0)),
        ],
        out_specs=[
            pl.BlockSpec((c, HEAD_DIM), lambda b, h, i: (b * nc + i, h)),
            pl.BlockSpec((1, 1, HEAD_DIM, HEAD_DIM), lambda b, h, i: (b, h, 0, 0)),
        ],
        out_shape=[jax.ShapeDtypeStruct((batch * seq, D_B), F32),
                   jax.ShapeDtypeStruct((batch, N_HEADS_B, HEAD_DIM, HEAD_DIM), F32)],
        scratch_shapes=[pltpu.VMEM((HEAD_DIM, HEAD_DIM), F32)],
        compiler_params=_cparams(3),
        name="hgrn_prompt",
    )(proj, proj, proj, jnp.asarray(cmat), jnp.asarray(lvl))


_HG_SB = 8


def _hgrn_sample_kernel(q_ref, f_ref, v_ref, st_ref, o_ref, s_out_ref, *, ds):
    s_idx = lax.broadcasted_iota(jnp.int32, (ds, 1), 0)
    for sb in range(_HG_SB):
        rsl = slice(sb * ds, (sb + 1) * ds)
        q = q_ref[rsl, :]
        f = f_ref[rsl, :]
        v = v_ref[rsl, :]
        k = 1.0 - f
        lf = jnp.log(f)
        rows = [lf[0:1]]
        for t in range(1, ds):
            rows.append(rows[-1] + lf[t:t + 1])
        b = jnp.concatenate(rows, axis=0)
        s0 = st_ref[sb, 0]
        o_inter = _dot((q * jnp.exp(b)).astype(BF16), s0.astype(BF16))
        outs = []
        for t in range(ds):
            w = q[t:t + 1] * k * jnp.exp(jnp.minimum(b[t:t + 1] - b, 0.0))
            a_t = jnp.where(s_idx <= t, jnp.sum(w, axis=-1, keepdims=True), 0.0)
            outs.append(jnp.sum(a_t * v, axis=0, keepdims=True))
        o_ref[rsl, :] = o_inter + jnp.concatenate(outs, axis=0)
        b_last = b[ds - 1:ds, :]
        khat = (k * jnp.exp(b_last - b)).astype(BF16)
        s_out_ref[sb, 0] = _column_of(jnp.exp(b_last)) * s0 + _dot_tn(khat, v.astype(BF16))


def hgrn_sample(proj, row0, state, db, ds):
    rb = _HG_SB * ds
    r0 = row0 // rb
    qc, fc, vc = E_QB // HEAD_DIM, E_F // HEAD_DIM, E_VB // HEAD_DIM
    return pl.pallas_call(
        functools.partial(_hgrn_sample_kernel, ds=ds),
        grid=(db // _HG_SB, N_HEADS_B),
        in_specs=[
            pl.BlockSpec((rb, HEAD_DIM), lambda i, h: (r0 + i, qc + h)),
            pl.BlockSpec((rb, HEAD_DIM), lambda i, h: (r0 + i, fc + h)),
            pl.BlockSpec((rb, HEAD_DIM), lambda i, h: (r0 + i, vc + h)),
            pl.BlockSpec((_HG_SB, 1, HEAD_DIM, HEAD_DIM), lambda i, h: (i, h, 0, 0)),
        ],
        out_specs=[
            pl.BlockSpec((rb, HEAD_DIM), lambda i, h: (i, h)),
            pl.BlockSpec((_HG_SB, 1, HEAD_DIM, HEAD_DIM), lambda i, h: (i, h, 0, 0)),
        ],
        out_shape=[jax.ShapeDtypeStruct((db * ds, D_B), F32),
                   jax.ShapeDtypeStruct(state.shape, F32)],
        compiler_params=_cparams(2),
        name="hgrn_sample",
    )(proj, proj, proj, state)


_CS_BLK = 128


def _split3(x):
    x1 = x.astype(BF16).astype(F32)
    r = x - x1
    x2 = r.astype(BF16).astype(F32)
    return x1, x2, r - x2


_AUG_ONES = 3 * N_HEADS_C
_INV_SCALE = HEAD_DIM ** 0.5
_LOG2E = 1.4426950408889634


def _cumsum_kernel(lf_ref, tri_ref, place_ref, c_ref, aug_ref):
    nblk = lf_ref.shape[1] // _CS_BLK
    lane = lax.broadcasted_iota(jnp.int32, (_CS_BLK, HEAD_DIM), 1)
    ones = jnp.where((lane >= _AUG_ONES) & (lane < _AUG_ONES + 3), 1.0, 0.0)

    def body(i, carry):
        r = pl.ds(pl.multiple_of(i * _CS_BLK, _CS_BLK), _CS_BLK)
        c = _dot_exact01(tri_ref[...], lf_ref[0, r, :]) + carry
        c_ref[0, r, :] = c
        aug = ones
        for j, piece in enumerate(_split3(c * _INV_SCALE)):
            aug = aug + _dot(piece.astype(BF16), place_ref[j].astype(BF16))
        aug_ref[0, r, :] = aug.astype(BF16)
        return c[_CS_BLK - 1:_CS_BLK, :]

    lax.fori_loop(0, nblk, body, jnp.zeros((1, lf_ref.shape[2]), F32))


def cumsum_rows(lf):
    b, s, h = lf.shape
    t = np.arange(_CS_BLK)
    tri = jnp.asarray((t[None, :] <= t[:, None]).astype(np.float32))
    place = np.zeros((3, h, HEAD_DIM), np.float32)
    for j in range(3):
        place[j, np.arange(h), j * h + np.arange(h)] = 1.0
    return pl.pallas_call(
        _cumsum_kernel,
        grid=(b,),
        in_specs=[pl.BlockSpec((1, s, h), lambda i: (i, 0, 0)),
                  pl.BlockSpec((_CS_BLK, _CS_BLK), lambda i: (0, 0)),
                  pl.BlockSpec((3, h, HEAD_DIM), lambda i: (0, 0, 0))],
        out_specs=[pl.BlockSpec((1, s, h), lambda i: (i, 0, 0)),
                   pl.BlockSpec((1, s, HEAD_DIM), lambda i: (i, 0, 0))],
        out_shape=[jax.ShapeDtypeStruct(lf.shape, F32),
                   jax.ShapeDtypeStruct((b, s, HEAD_DIM), BF16)],
        compiler_params=_cparams(1),
        name="fox_cumsum",
    )(lf, tri, jnp.asarray(place))


_FOX_TQ = 512
_FOX_TK = 512


def _fox_prompt_kernel(q_ref, k_ref, v_ref, cq_ref, ca_ref, o_ref, qa_ref, m_ref, l_ref, acc_ref):
    kv = pl.program_id(1)
    qi = pl.program_id(2)
    ki = pl.program_id(3)
    tq, tk = q_ref.shape[0], k_ref.shape[0]

    @pl.when(ki == 0)
    def _():
        m_ref[...] = jnp.full_like(m_ref, NEG)
        l_ref[...] = jnp.zeros_like(l_ref)
        acc_ref[...] = jnp.zeros_like(acc_ref)
        lane = lax.broadcasted_iota(jnp.int32, (tq, HEAD_DIM), 1)
        for g in range(G_C):
            h = kv * G_C + g
            own = (lane == h) | (lane == N_HEADS_C + h) | (lane == 2 * N_HEADS_C + h)
            extra = jnp.where(own, -1.0, 0.0)
            for j, piece in enumerate(_split3(cq_ref[0, 0, :, g:g + 1] * _INV_SCALE)):
                extra = jnp.where(lane == _AUG_ONES + j, piece, extra)
            qa_ref[g, :, :HEAD_DIM] = q_ref[:, g * HEAD_DIM:(g + 1) * HEAD_DIM].astype(BF16)
            qa_ref[g, :, HEAD_DIM:] = extra.astype(BF16)

    def update(masked):
        kaug = jnp.concatenate([k_ref[...].astype(BF16), ca_ref[0]], axis=1)
        v = v_ref[...].astype(BF16)
        if masked:
            causal = (lax.broadcasted_iota(jnp.int32, (tq, tk), 1)
                      <= lax.broadcasted_iota(jnp.int32, (tq, tk), 0))
        for g in range(G_C):
            s = _dot_nt(qa_ref[g], kaug) * (SCALE * _LOG2E)
            if masked:
                s = jnp.where(causal, s, NEG)
            m_prev = m_ref[g]
            m_new = jnp.maximum(m_prev, jnp.max(s, axis=-1, keepdims=True))
            alpha = jnp.exp2(m_prev - m_new)
            p = jnp.exp2(s - m_new)
            l_ref[g] = alpha * l_ref[g] + jnp.sum(p, axis=-1, keepdims=True)
            acc_ref[g] = alpha * acc_ref[g] + _dot(p.astype(BF16), v)
            m_ref[g] = m_new

    @pl.when(ki < qi)
    def _():
        update(False)

    @pl.when(ki == qi)
    def _():
        update(True)

    @pl.when(ki == pl.num_programs(3) - 1)
    def _():
        for g in range(G_C):
            o_ref[:, g * HEAD_DIM:(g + 1) * HEAD_DIM] = acc_ref[g] / l_ref[g]


def fox_prompt(projo, c, c_aug, batch, seq):
    tq = tk = min(_FOX_TQ, seq)
    nq, nk = seq // tq, seq // tk
    c_q = jnp.transpose(c.reshape(batch, seq, N_KV_C, G_C), (0, 2, 1, 3))
    kcol, vcol = O_K // HEAD_DIM, O_V // HEAD_DIM
    return pl.pallas_call(
        _fox_prompt_kernel,
        grid=(batch, N_KV_C, nq, nk),
        in_specs=[
            pl.BlockSpec((tq, G_C * HEAD_DIM), lambda b, k, i, j: (b * nq + i, k)),
            pl.BlockSpec((tk, HEAD_DIM), lambda b, k, i, j: (b * nk + jnp.minimum(j, i), kcol + k)),
            pl.BlockSpec((tk, HEAD_DIM), lambda b, k, i, j: (b * nk + jnp.minimum(j, i), vcol + k)),
            pl.BlockSpec((1, 1, tq, G_C), lambda b, k, i, j: (b, k, i, 0)),
            pl.BlockSpec((1, tk, HEAD_DIM), lambda b, k, i, j: (b, jnp.minimum(j, i), 0)),
        ],
        out_specs=pl.BlockSpec((tq, G_C * HEAD_DIM), lambda b, k, i, j: (b * nq + i, k)),
        out_shape=jax.ShapeDtypeStruct((batch * seq, D_C), F32),
        scratch_shapes=[pltpu.VMEM((G_C, tq, 2 * HEAD_DIM), BF16),
                        pltpu.VMEM((G_C, tq, 1), F32), pltpu.VMEM((G_C, tq, 1), F32),
                        pltpu.VMEM((G_C, tq, HEAD_DIM), F32)],
        compiler_params=_cparams(4),
        name="fox_prompt",
    )(projo, projo, projo, c_q, c_aug)


_FOX_PP = 16


def _fox_sample_kernel(pt_ref, q_ref, kn_ref, vn_ref, lfn_ref, tri_ref, ut1_ref, *refs, ds, page):
    del pt_ref
    pp = _FOX_PP
    k_refs, v_refs, lf_refs = refs[:pp], refs[pp:2 * pp], refs[2 * pp:3 * pp]
    o_ref, qb_ref, cq_ref, m_ref, l_ref, acc_ref, carry_ref = refs[3 * pp:]
    nrow = N_HEADS_C * ds
    rows_kv = G_C * ds
    step = pl.program_id(1)

    @pl.when(step == 0)
    def _():
        qb = q_ref[0].astype(BF16)
        qb_ref[...] = qb
        cn = _dot_exact01_r(lfn_ref[0], tri_ref[...])
        t_row = lax.broadcasted_iota(jnp.int32, (nrow, ds), 0) % ds
        s_col = lax.broadcasted_iota(jnp.int32, (nrow, ds), 1)
        cq = jnp.sum(jnp.where(s_col == t_row, cn, 0.0), axis=-1, keepdims=True)
        cq_ref[...] = cq
        causal = (lax.broadcasted_iota(jnp.int32, (rows_kv, ds), 1)
                  <= lax.broadcasted_iota(jnp.int32, (rows_kv, ds), 0) % ds)
        for kv in range(N_KV_C):
            rsl = slice(kv * rows_kv, (kv + 1) * rows_kv)
            csl = slice(kv * HEAD_DIM, (kv + 1) * HEAD_DIM)
            s = _dot_nt(qb_ref[rsl, :], kn_ref[:, csl].astype(BF16)) * SCALE + cq_ref[rsl, :] - cn[rsl]
            s = jnp.where(causal, s, NEG)
            m = jnp.max(s, axis=-1, keepdims=True)
            e = jnp.exp(s - m)
            m_ref[rsl, :] = m
            l_ref[rsl, :] = jnp.sum(e, axis=-1, keepdims=True)
            acc_ref[rsl, :] = _dot(e.astype(BF16), vn_ref[:, csl].astype(BF16))
        carry_ref[...] = jnp.zeros_like(carry_ref)

    lf_stack = jnp.concatenate([lf_refs[i][0] for i in range(pp)], axis=0)
    sums = _dot_exact01_r(lf_stack, ut1_ref[...])
    carry = carry_ref[...]
    dex = []
    for i in range(pp):
        hs = slice(i * N_HEADS_C, (i + 1) * N_HEADS_C)
        dex.append(sums[hs, :page] + carry)
        carry = carry + sums[hs, page:]
    carry_ref[...] = carry

    def kv_rows(refs_, kv):
        strided = pl.ds(kv, page, stride=N_KV_C)
        return jnp.concatenate([refs_[i][0, strided, :] for i in range(pp)], axis=0).astype(BF16)

    qk = jnp.concatenate(
        [_dot_nt(qb_ref[kv * rows_kv:(kv + 1) * rows_kv, :], kv_rows(k_refs, kv)) for kv in range(N_KV_C)],
        axis=0)
    bias = jnp.concatenate(
        [jnp.broadcast_to(d[:, None, :], (N_HEADS_C, ds, page)).reshape(nrow, page) for d in dex], axis=1)
    s = qk * SCALE + cq_ref[...] + bias
    m_prev = m_ref[...]
    m_new = jnp.maximum(m_prev, jnp.max(s, axis=-1, keepdims=True))
    alpha = jnp.exp(m_prev - m_new)
    p = jnp.exp(s - m_new)
    l_ref[...] = alpha * l_ref[...] + jnp.sum(p, axis=-1, keepdims=True)
    m_ref[...] = m_new
    pb = p.astype(BF16)
    pv = jnp.concatenate(
        [_dot(pb[kv * rows_kv:(kv + 1) * rows_kv, :], kv_rows(v_refs, kv)) for kv in range(N_KV_C)], axis=0)
    acc_ref[...] = alpha * acc_ref[...] + pv

    @pl.when(step == pl.num_programs(1) - 1)
    def _():
        o_ref[0] = acc_ref[...] / l_ref[...]


def fox_sample(projo, row0, lf_s, cache_k, cache_v, cache_lf, page_table, db, ds):
    pool, page = cache_lf.shape[0], cache_lf.shape[1]
    n_pages = page_table.shape[1]
    pp = _FOX_PP
    nrow = N_HEADS_C * ds
    q = projo[row0:row0 + db * ds, O_Q:O_Q + D_C].reshape(db, ds, N_HEADS_C, HEAD_DIM)
    q = jnp.transpose(q, (0, 2, 1, 3)).reshape(db, nrow, HEAD_DIM)
    lfn = jnp.transpose(lf_s.reshape(db, ds, N_HEADS_C), (0, 2, 1))
    lfn = jnp.broadcast_to(lfn[:, :, None, :], (db, N_HEADS_C, ds, ds)).reshape(db, nrow, ds)
    cache_lft = jnp.transpose(cache_lf, (0, 2, 1))
    t = np.arange(ds)
    tri = jnp.asarray((t[:, None] <= t[None, :]).astype(np.float32))
    u = np.arange(page)
    ut1 = jnp.asarray(np.concatenate([(u[:, None] > u[None, :]), np.ones((page, page), bool)],
                                     axis=1).astype(np.float32))
    r0 = row0 // ds

    def page_map(i):
        return lambda b, p, pt: (pt[b, n_pages - 1 - (p * pp + i)], 0, 0)

    in_specs = [
        pl.BlockSpec((1, nrow, HEAD_DIM), lambda b, p, pt: (b, 0, 0)),
        pl.BlockSpec((ds, KVD_C), lambda b, p, pt: (r0 + b, O_K // KVD_C)),
        pl.BlockSpec((ds, KVD_C), lambda b, p, pt: (r0 + b, O_V // KVD_C)),
        pl.BlockSpec((1, nrow, ds), lambda b, p, pt: (b, 0, 0)),
        pl.BlockSpec((ds, ds), lambda b, p, pt: (0, 0)),
        pl.BlockSpec((page, 2 * page), lambda b, p, pt: (0, 0)),
    ]
    in_specs += [pl.BlockSpec((1, page * N_KV_C, HEAD_DIM), page_map(i)) for i in range(pp)]
    in_specs += [pl.BlockSpec((1, page * N_KV_C, HEAD_DIM), page_map(i)) for i in range(pp)]
    in_specs += [pl.BlockSpec((1, N_HEADS_C, page), page_map(i)) for i in range(pp)]
    out = pl.pallas_call(
        functools.partial(_fox_sample_kernel, ds=ds, page=page),
        grid_spec=pltpu.PrefetchScalarGridSpec(
            num_scalar_prefetch=1,
            grid=(db, n_pages // pp),
            in_specs=in_specs,
            out_specs=pl.BlockSpec((1, nrow, HEAD_DIM), lambda b, p, pt: (b, 0, 0)),
            scratch_shapes=[pltpu.VMEM((nrow, HEAD_DIM), BF16), pltpu.VMEM((nrow, 1), F32),
                            pltpu.VMEM((nrow, 1), F32), pltpu.VMEM((nrow, 1), F32),
                            pltpu.VMEM((nrow, HEAD_DIM), F32), pltpu.VMEM((N_HEADS_C, page), F32)],
        ),
        out_shape=jax.ShapeDtypeStruct((db, nrow, HEAD_DIM), F32),
        compiler_params=_cparams(2),
        name="fox_sample",
    )(page_table, q, projo, projo, lfn, tri, ut1, *([cache_k] * pp), *([cache_v] * pp),
      *([cache_lft] * pp))
    out = jnp.transpose(out.reshape(db, N_HEADS_C, ds, HEAD_DIM), (0, 2, 1, 3))
    return out.reshape(db * ds, D_C)


def _moe_route_kernel(x_ref, g_ref, r_ref, h_ref, idx_ref, w_ref):
    hn = _rms(x_ref[...], g_ref[...])
    h_ref[...] = hn.astype(BF16)
    logits = jnp.dot(hn, r_ref[...], preferred_element_type=F32, precision=lax.Precision.HIGHEST)
    lane = lax.broadcasted_iota(jnp.int32, logits.shape, 1)
    m1 = jnp.max(logits, axis=-1, keepdims=True)
    i1 = jnp.min(jnp.where(logits == m1, lane, N_EXPERTS), axis=-1, keepdims=True)
    rest = jnp.where(lane == i1, -jnp.inf, logits)
    m2 = jnp.max(rest, axis=-1, keepdims=True)
    i2 = jnp.min(jnp.where(rest == m2, lane, N_EXPERTS), axis=-1, keepdims=True)
    e = jnp.exp(m2 - m1)
    slot = lax.broadcasted_iota(jnp.int32, idx_ref.shape, 1)
    idx_ref[...] = jnp.where(slot == 0, i1, i2)
    w_ref[...] = jnp.where(slot == 0, 1.0 / (1.0 + e), e / (1.0 + e))


def moe_route(x, g, router, tm):
    n, d = x.shape
    return pl.pallas_call(
        _moe_route_kernel,
        grid=(n // tm,),
        in_specs=[pl.BlockSpec((tm, d), lambda i: (i, 0)),
                  pl.BlockSpec((1, d), lambda i: (0, 0)),
                  pl.BlockSpec((d, N_EXPERTS), lambda i: (0, 0))],
        out_specs=[pl.BlockSpec((tm, d), lambda i: (i, 0)),
                   pl.BlockSpec((tm, TOP_K), lambda i: (i, 0)),
                   pl.BlockSpec((tm, TOP_K), lambda i: (i, 0))],
        out_shape=[jax.ShapeDtypeStruct((n, d), BF16),
                   jax.ShapeDtypeStruct((n, TOP_K), jnp.int32),
                   jax.ShapeDtypeStruct((n, TOP_K), F32)],
        compiler_params=_cparams(1),
        name="moe_route",
    )(x, g.reshape(1, d), router)


_MOE_SUB = 256


def _moe_ffn_kernel(te_ref, nv_ref, x_ref, gate_ref, wg_ref, wu_ref, wd_ref, o_ref,
                    wg_s, wu_s, wd_s):
    del te_ref
    valid = nv_ref[pl.program_id(0)]
    f = pl.program_id(1)

    @pl.when(f == 0)
    def _():
        o_ref[...] = jnp.zeros_like(o_ref)

    @pl.when(valid > 0)
    def _():
        wg_s[...] = wg_ref[0].astype(BF16)
        wu_s[...] = wu_ref[0].astype(BF16)
        wd_s[...] = wd_ref[0].astype(BF16)

    sub = min(_MOE_SUB, x_ref.shape[0])
    for s in range(x_ref.shape[0] // sub):
        @pl.when(s * sub < valid)
        def _():
            rows = slice(s * sub, (s + 1) * sub)
            x = x_ref[rows, :]
            act = (_silu(_dot(x, wg_s[...])) * _dot(x, wu_s[...])).astype(BF16)
            _accumulate_down(o_ref, rows, act, wd_s[...])

    @pl.when(f == pl.num_programs(1) - 1)
    def _():
        o_ref[...] = o_ref[...] * gate_ref[...]


def moe_ffn(x_sorted, gate_sorted, tile_expert, tile_valid, w_gate, w_up, w_down, tm, tf=256):
    p, d = x_sorted.shape
    d_ff = w_gate.shape[2]
    nf = d_ff // tf

    def f_eff(i, f, nv):
        return jnp.where(nv[i] > 0, f, nf - 1)

    return pl.pallas_call(
        _moe_ffn_kernel,
        grid_spec=pltpu.PrefetchScalarGridSpec(
            num_scalar_prefetch=2,
            grid=(p // tm, nf),
            in_specs=[
                pl.BlockSpec((tm, d), lambda i, f, te, nv: (i, 0)),
                pl.BlockSpec((tm, 1), lambda i, f, te, nv: (i, 0)),
                pl.BlockSpec((1, d, tf), lambda i, f, te, nv: (te[i], 0, f_eff(i, f, nv))),
                pl.BlockSpec((1, d, tf), lambda i, f, te, nv: (te[i], 0, f_eff(i, f, nv))),
                pl.BlockSpec((1, tf, d), lambda i, f, te, nv: (te[i], f_eff(i, f, nv), 0)),
            ],
            out_specs=pl.BlockSpec((tm, d), lambda i, f, te, nv: (i, 0)),
            scratch_shapes=[pltpu.VMEM((d, tf), BF16), pltpu.VMEM((d, tf), BF16),
                            pltpu.VMEM((tf, d), BF16)],
        ),
        out_shape=jax.ShapeDtypeStruct((p, d), F32),
        compiler_params=_cparams(2),
        name="moe_ffn",
    )(tile_expert, tile_valid, x_sorted, gate_sorted, w_gate, w_up, w_down)


def _route_plan(idx, w, tm):
    n = idx.shape[0]
    flat_e = idx.reshape(-1)
    onehot = (flat_e[:, None] == jnp.arange(N_EXPERTS, dtype=jnp.int32)[None, :]).astype(jnp.int32)
    csum = jnp.cumsum(onehot, axis=0)
    rank = jnp.take_along_axis(csum, flat_e[:, None], axis=1)[:, 0] - 1
    count = csum[-1]
    tiles_e = (count + tm - 1) // tm
    tile_end = jnp.cumsum(tiles_e)
    tile_start = tile_end - tiles_e
    pos = tile_start[flat_e] * tm + rank
    n_tiles = (TOP_K * n) // tm + N_EXPERTS
    token = jnp.zeros((n_tiles * tm,), jnp.int32).at[pos].set(jnp.arange(TOP_K * n, dtype=jnp.int32) // TOP_K)
    gate = jnp.zeros((n_tiles * tm,), F32).at[pos].set(w.reshape(-1))
    tile = jnp.arange(n_tiles, dtype=jnp.int32)
    tile_expert = jnp.minimum(jnp.searchsorted(tile_end, tile, side="right"), N_EXPERTS - 1).astype(jnp.int32)
    tile_valid = jnp.clip(count[tile_expert] - (tile - tile_start[tile_expert]) * tm, 0, tm)
    tile_valid = jnp.where(tile < tile_end[-1], tile_valid, 0).astype(jnp.int32)
    return pos.reshape(n, TOP_K), token, gate.reshape(-1, 1), tile_expert, tile_valid


def moe_layer(x, g, router, w_gate, w_up, w_down, tm_route, tm):
    h, idx, w = moe_route(x, g, router, tm_route)
    pos, token, gate, tile_expert, tile_valid = _route_plan(idx, w, tm)
    y_sorted = moe_ffn(jnp.take(h, token, axis=0), gate, tile_expert, tile_valid, w_gate, w_up, w_down, tm)
    return x + jnp.take(y_sorted, pos[:, 0], axis=0) + jnp.take(y_sorted, pos[:, 1], axis=0)


def _row_tile(n, want):
    t = min(want, n)
    while n % t:
        t //= 2
    return t


def kernel(x_prompt, x_sample, cache_k_win, cache_v_win, state_hgrn, cache_k_fox, cache_v_fox, cache_logf_fox, page_table, rel_bias, norm_mix_e, w_in_e, q_norm_a, k_norm_a, sinks_a, lb_b, o_norm_b, w_out_e, norm_ffn_e, w_gate_e, w_up_e, w_down_e, norm_mix_o, w_in_o, fgate_bias_c, q_norm_c, k_norm_c, w_out_o, norm_ffn_o, router_o, w_gate_x, w_up_x, w_down_x):
    batch, seq, d = x_prompt.shape
    db, ds, _ = x_sample.shape
    n_p, n_s = batch * seq, db * ds
    n = n_p + n_s
    win = cache_k_win.shape[2]
    pool, page = cache_k_fox.shape[1], cache_k_fox.shape[2]
    tm_big = _row_tile(n, 1024)
    tm_ffn = 768 if n % 768 == 0 else _row_tile(n, 512)

    x = jnp.concatenate([x_prompt.reshape(n_p, d), x_sample.reshape(n_s, d)], axis=0)

    proj = even_project(x, norm_mix_e[0], w_in_e[0], q_norm_a[0], k_norm_a[0], lb_b, tm_big)
    oa_p = swa_prompt(proj, rel_bias, sinks_a[0], batch, seq)
    oa_s, wk_s, wv_s = swa_sample(proj, n_p, cache_k_win[0].reshape(db, win, KVD_A),
                                  cache_v_win[0].reshape(db, win, KVD_A), rel_bias, sinks_a[0], db, ds)
    ob_p, hg_p = hgrn_prompt(proj, batch, seq)
    ob_s, hg_s = hgrn_sample(proj, n_p, state_hgrn[0], db, ds)
    oa = jnp.concatenate([oa_p, oa_s], axis=0)
    ob = jnp.concatenate([ob_p, ob_s], axis=0)
    y = even_merge(oa, ob, proj, o_norm_b[0], w_out_e[0], x, tm_big)
    y = dense_ffn(y, norm_ffn_e[0], w_gate_e[0], w_up_e[0], w_down_e[0], tm_ffn)

    projo, lf = odd_project(y, norm_mix_o[0], w_in_o[0], q_norm_c[0], k_norm_c[0], fgate_bias_c[0], tm_big)
    c, c_aug = cumsum_rows(lf[:n_p].reshape(batch, seq, N_HEADS_C))
    oc_p = fox_prompt(projo, c, c_aug, batch, seq)
    oc_s = fox_sample(projo, n_p, lf[n_p:], cache_k_fox[0].reshape(pool, page * N_KV_C, HEAD_DIM),
                      cache_v_fox[0].reshape(pool, page * N_KV_C, HEAD_DIM), cache_logf_fox[0],
                      page_table, db, ds)
    y = out_project(jnp.concatenate([oc_p, oc_s], axis=0), w_out_o[0], y, tm_big)
    y = moe_layer(y, norm_ffn_o[0], router_o[0], w_gate_x[0], w_up_x[0], w_down_x[0], tm_big, tm_big)

    def prompt_tail(col, width, heads):
        t = proj[:n_p, col:col + width].reshape(batch, seq, heads, HEAD_DIM)
        return t[:, seq - WINDOW:][None]

    return (
        y[:n_p].reshape(batch, seq, d),
        y[n_p:].reshape(db, ds, d),
        prompt_tail(E_KA, KVD_A, N_KV_A),
        prompt_tail(E_VA, KVD_A, N_KV_A),
        wk_s.reshape(1, db, win, N_KV_A, HEAD_DIM),
        wv_s.reshape(1, db, win, N_KV_A, HEAD_DIM),
        hg_p[None],
        hg_s[None],
        projo[:n_p, O_K:O_K + KVD_C].reshape(1, batch, seq, N_KV_C, HEAD_DIM),
        projo[:n_p, O_V:O_V + KVD_C].reshape(1, batch, seq, N_KV_C, HEAD_DIM),
        lf[:n_p].reshape(1, batch, seq, N_HEADS_C),
        projo[n_p:, O_K:O_K + KVD_C].reshape(1, db, ds, N_KV_C, HEAD_DIM),
        projo[n_p:, O_V:O_V + KVD_C].reshape(1, db, ds, N_KV_C, HEAD_DIM),
        lf[n_p:].reshape(1, db, ds, N_HEADS_C),
    )
```

```python
import functools
import math

import numpy as np
import jax
import jax.numpy as jnp
from jax import lax
from jax.experimental import pallas as pl
from jax.experimental.pallas import tpu as pltpu

F32 = jnp.float32
BF16 = jnp.bfloat16

HEAD_DIM = 128
N_HEADS_A, N_KV_A = 8, 2
G_A = N_HEADS_A // N_KV_A
WINDOW = 128
REL_BUCKETS, REL_MAX_DIST = 32, 128
N_HEADS_B = 8
N_HEADS_C, N_KV_C = 16, 4
G_C = N_HEADS_C // N_KV_C
N_EXPERTS, TOP_K = 8, 2
EPS = 1e-6
SCALE = HEAD_DIM ** -0.5
NEG = -1e30

D_A = N_HEADS_A * HEAD_DIM
KVD_A = N_KV_A * HEAD_DIM
D_B = N_HEADS_B * HEAD_DIM
D_C = N_HEADS_C * HEAD_DIM
KVD_C = N_KV_C * HEAD_DIM

E_QA, E_QB, E_F, E_VB, E_GB = 0, D_A, D_A + D_B, D_A + 2 * D_B, D_A + 3 * D_B
E_KA = D_A + 4 * D_B
E_VA = E_KA + KVD_A
E_OUT = E_VA + KVD_A
O_Q, O_K, O_V = 0, D_C, D_C + KVD_C
O_OUT = D_C + 2 * KVD_C

VMEM_LIMIT = 56 * 1024 * 1024


def _cparams(n_axes):
    return pltpu.CompilerParams(dimension_semantics=("arbitrary",) * n_axes,
                                vmem_limit_bytes=VMEM_LIMIT)


def _rms(x, gain):
    return x * lax.rsqrt(jnp.mean(x * x, axis=-1, keepdims=True) + EPS) * gain


def _silu(x):
    return x * (1.0 / (1.0 + jnp.exp(-x)))


def _dot(a, b):
    return jnp.dot(a, b, preferred_element_type=F32)


def _dot_nt(a, b):
    return lax.dot_general(a, b, (((1,), (1,)), ((), ())), preferred_element_type=F32)


def _dot_tn(a, b):
    return lax.dot_general(a, b, (((0,), (0,)), ((), ())), preferred_element_type=F32)


def _dot_exact01(m01, x):
    x1 = x.astype(BF16)
    r1 = x - x1.astype(F32)
    x2 = r1.astype(BF16)
    x3 = (r1 - x2.astype(F32)).astype(BF16)
    m = m01.astype(BF16)
    return _dot(m, x1) + _dot(m, x2) + _dot(m, x3)


def _dot_exact01_r(x, m01):
    x1 = x.astype(BF16)
    r1 = x - x1.astype(F32)
    x2 = r1.astype(BF16)
    x3 = (r1 - x2.astype(F32)).astype(BF16)
    m = m01.astype(BF16)
    return _dot(x1, m) + _dot(x2, m) + _dot(x3, m)


_TN = 256


def _even_out_tile(j):
    return jnp.where(j < 4, j, jnp.where(j < 6, j + 16, j - 2))


def _even_proj_kernel(x_ref, g_ref, w_ref, qn_ref, kn_ref, lb_ref, o_ref, hn_ref):
    j = pl.program_id(1)

    @pl.when(j == 0)
    def _():
        hn_ref[...] = _rms(x_ref[...], g_ref[...]).astype(BF16)

    acc = _dot(hn_ref[...], w_ref[...].astype(BF16))

    def head_norm(gain):
        for h in range(_TN // HEAD_DIM):
            sl = slice(h * HEAD_DIM, (h + 1) * HEAD_DIM)
            o_ref[:, sl] = _rms(acc[:, sl], gain)

    @pl.when(j < 4)
    def _():
        head_norm(qn_ref[...])

    @pl.when(j == 4)
    def _():
        head_norm(kn_ref[...])

    @pl.when((j == 5) | (j >= 14))
    def _():
        o_ref[...] = acc

    @pl.when((j >= 6) & (j < 10))
    def _():
        o_ref[...] = _silu(acc)

    @pl.when((j >= 10) & (j < 14))
    def _():
        lb = lb_ref[...]
        e = jnp.exp(lb - jnp.max(lb, axis=0, keepdims=True))
        lb0 = e[0:1, :] / jnp.sum(e, axis=0, keepdims=True)
        o_ref[...] = lb0 + (1.0 - lb0) * (1.0 / (1.0 + jnp.exp(-acc)))


def even_project(x, g, w_in, q_norm, k_norm, lb_b, tm):
    n, d = x.shape
    n_tiles = w_in.shape[1] // _TN
    return pl.pallas_call(
        _even_proj_kernel,
        grid=(n // tm, n_tiles),
        in_specs=[
            pl.BlockSpec((tm, d), lambda i, j: (i, 0)),
            pl.BlockSpec((1, d), lambda i, j: (0, 0)),
            pl.BlockSpec((d, _TN), lambda i, j: (0, j)),
            pl.BlockSpec((1, HEAD_DIM), lambda i, j: (0, 0)),
            pl.BlockSpec((1, HEAD_DIM), lambda i, j: (0, 0)),
            pl.BlockSpec((lb_b.shape[0], _TN), lambda i, j: (0, jnp.clip(j - 10, 0, 3))),
        ],
        out_specs=pl.BlockSpec((tm, _TN), lambda i, j: (i, _even_out_tile(j))),
        out_shape=jax.ShapeDtypeStruct((n, E_OUT), F32),
        scratch_shapes=[pltpu.VMEM((tm, d), BF16)],
        compiler_params=_cparams(2),
        name="even_project",
    )(x, g.reshape(1, d), w_in, q_norm.reshape(1, -1), k_norm.reshape(1, -1), lb_b)


def _even_merge_kernel(oa_ref, ob_ref, gb_ref, on_ref, w_ref, x_ref, o_ref, lhs_ref):
    j = pl.program_id(1)

    @pl.when(j == 0)
    def _():
        lhs_ref[:, :D_A] = oa_ref[...].astype(BF16)
        for h in range(N_HEADS_B):
            sl = slice(h * HEAD_DIM, (h + 1) * HEAD_DIM)
            obn = _rms(ob_ref[:, sl], on_ref[...]) * _silu(gb_ref[:, sl])
            lhs_ref[:, D_A + h * HEAD_DIM:D_A + (h + 1) * HEAD_DIM] = obn.astype(BF16)

    o_ref[...] = x_ref[...] + _dot(lhs_ref[...], w_ref[...].astype(BF16))


def even_merge(oa, ob, proj, o_norm, w_out, x, tm, tn=512):
    n, d = x.shape
    return pl.pallas_call(
        _even_merge_kernel,
        grid=(n // tm, d // tn),
        in_specs=[
            pl.BlockSpec((tm, D_A), lambda i, j: (i, 0)),
            pl.BlockSpec((tm, D_B), lambda i, j: (i, 0)),
            pl.BlockSpec((tm, D_B), lambda i, j: (i, E_GB // D_B)),
            pl.BlockSpec((1, HEAD_DIM), lambda i, j: (0, 0)),
            pl.BlockSpec((D_A + D_B, tn), lambda i, j: (0, j)),
            pl.BlockSpec((tm, tn), lambda i, j: (i, j)),
        ],
        out_specs=pl.BlockSpec((tm, tn), lambda i, j: (i, j)),
        out_shape=jax.ShapeDtypeStruct((n, d), F32),
        scratch_shapes=[pltpu.VMEM((tm, D_A + D_B), BF16)],
        compiler_params=_cparams(2),
        name="even_merge",
    )(oa, ob, proj, o_norm.reshape(1, -1), w_out, x)


_FFN_TM = 512
_FFN_TF = 512
_FFN_TN = 512


def _norm_cast_kernel(x_ref, g_ref, o_ref):
    o_ref[...] = _rms(x_ref[...], g_ref[...]).astype(BF16)


def norm_cast(x, g, tm):
    n, d = x.shape
    return pl.pallas_call(
        _norm_cast_kernel,
        grid=(n // tm,),
        in_specs=[pl.BlockSpec((tm, d), lambda i: (i, 0)), pl.BlockSpec((1, d), lambda i: (0, 0))],
        out_specs=pl.BlockSpec((tm, d), lambda i: (i, 0)),
        out_shape=jax.ShapeDtypeStruct((n, d), BF16),
        compiler_params=_cparams(1),
        name="norm_cast",
    )(x, g.reshape(1, d))


def _group_start(te_ref, i):
    return (i == 0) | (te_ref[i] != te_ref[jnp.maximum(i - 1, 0)])


def _ffn_up_kernel(te_ref, nv_ref, x_ref, wg_ref, wu_ref, o_ref, wg_s, wu_s):
    i = pl.program_id(1)

    @pl.when(_group_start(te_ref, i))
    def _():
        wg_s[...] = wg_ref[0].astype(BF16)
        wu_s[...] = wu_ref[0].astype(BF16)

    @pl.when(nv_ref[i] > 0)
    def _():
        x = x_ref[...]
        o_ref[...] = (_silu(_dot(x, wg_s[...])) * _dot(x, wu_s[...])).astype(BF16)

    @pl.when(nv_ref[i] == 0)
    def _():
        o_ref[...] = jnp.zeros_like(o_ref)


def ffn_up(x, tile_expert, tile_valid, w_gate, w_up):
    p, d = x.shape
    d_ff = w_gate.shape[2]
    tm, tf = min(_FFN_TM, p), _FFN_TF
    return pl.pallas_call(
        _ffn_up_kernel,
        grid_spec=pltpu.PrefetchScalarGridSpec(
            num_scalar_prefetch=2,
            grid=(d_ff // tf, p // tm),
            in_specs=[
                pl.BlockSpec((tm, d), lambda f, i, te, nv: (i, 0)),
                pl.BlockSpec((1, d, tf), lambda f, i, te, nv: (te[i], 0, f)),
                pl.BlockSpec((1, d, tf), lambda f, i, te, nv: (te[i], 0, f)),
            ],
            out_specs=pl.BlockSpec((tm, tf), lambda f, i, te, nv: (i, f)),
            scratch_shapes=[pltpu.VMEM((d, tf), BF16), pltpu.VMEM((d, tf), BF16)],
        ),
        out_shape=jax.ShapeDtypeStruct((p, d_ff), BF16),
        compiler_params=_cparams(2),
        name="ffn_up",
    )(tile_expert, tile_valid, x, w_gate, w_up)


def _ffn_down_kernel(te_ref, nv_ref, a_ref, wd_ref, *rest, has_gate, has_res):
    rest = list(rest)
    gate_ref = rest.pop(0) if has_gate else None
    res_ref = rest.pop(0) if has_res else None
    o_ref, wd_s = rest
    i = pl.program_id(1)

    @pl.when(_group_start(te_ref, i))
    def _():
        wd_s[...] = wd_ref[0].astype(BF16)

    @pl.when(nv_ref[i] > 0)
    def _():
        y = _dot(a_ref[...], wd_s[...])
        if has_gate:
            y = y * gate_ref[...]
        if has_res:
            y = res_ref[...] + y
        o_ref[...] = y

    @pl.when(nv_ref[i] == 0)
    def _():
        o_ref[...] = jnp.zeros_like(o_ref)


def ffn_down(act, tile_expert, tile_valid, w_down, gate=None, res=None):
    p, d_ff = act.shape
    d = w_down.shape[2]
    tm, tn = min(_FFN_TM, p), _FFN_TN
    in_specs = [
        pl.BlockSpec((tm, d_ff), lambda n, i, te, nv: (i, 0)),
        pl.BlockSpec((1, d_ff, tn), lambda n, i, te, nv: (te[i], 0, n)),
    ]
    args = [act, w_down]
    if gate is not None:
        in_specs.append(pl.BlockSpec((tm, 1), lambda n, i, te, nv: (i, 0)))
        args.append(gate)
    if res is not None:
        in_specs.append(pl.BlockSpec((tm, tn), lambda n, i, te, nv: (i, n)))
        args.append(res)
    return pl.pallas_call(
        functools.partial(_ffn_down_kernel, has_gate=gate is not None, has_res=res is not None),
        grid_spec=pltpu.PrefetchScalarGridSpec(
            num_scalar_prefetch=2,
            grid=(d // tn, p // tm),
            in_specs=in_specs,
            out_specs=pl.BlockSpec((tm, tn), lambda n, i, te, nv: (i, n)),
            scratch_shapes=[pltpu.VMEM((d_ff, tn), BF16)],
        ),
        out_shape=jax.ShapeDtypeStruct((p, d), F32),
        compiler_params=_cparams(2),
        name="ffn_down",
    )(tile_expert, tile_valid, *args)


def dense_ffn(x, g, w_gate, w_up, w_down):
    n = x.shape[0]
    tm = min(_FFN_TM, n)
    tiles = n // tm
    te = jnp.zeros((tiles,), jnp.int32)
    nv = jnp.full((tiles,), tm, jnp.int32)
    act = ffn_up(norm_cast(x, g, tm), te, nv, w_gate[None], w_up[None])
    return ffn_down(act, te, nv, w_down[None], res=x)


def _odd_proj_kernel(x_ref, g_ref, w_ref, wf_ref, qn_ref, kn_ref, fb_ref, o_ref, lf_ref, hn_ref):
    j = pl.program_id(1)
    nq = D_C // _TN
    nk = KVD_C // _TN

    @pl.when(j == 0)
    def _():
        hn = _rms(x_ref[...], g_ref[...]).astype(BF16)
        hn_ref[...] = hn
        z = _dot(hn, wf_ref[...].astype(BF16)) + fb_ref[...]
        lf_ref[...] = jnp.minimum(z, 0.0) - jnp.log(1.0 + jnp.exp(-jnp.abs(z)))

    acc = _dot(hn_ref[...], w_ref[...].astype(BF16))

    def head_norm(gain):
        for h in range(_TN // HEAD_DIM):
            sl = slice(h * HEAD_DIM, (h + 1) * HEAD_DIM)
            o_ref[:, sl] = _rms(acc[:, sl], gain)

    @pl.when(j < nq)
    def _():
        head_norm(qn_ref[...])

    @pl.when((j >= nq) & (j < nq + nk))
    def _():
        head_norm(kn_ref[...])

    @pl.when(j >= nq + nk)
    def _():
        o_ref[...] = acc


def odd_project(x, g, w_in, q_norm, k_norm, f_bias, tm):
    n, d = x.shape
    w_f = w_in[:, O_OUT:]
    return pl.pallas_call(
        _odd_proj_kernel,
        grid=(n // tm, O_OUT // _TN),
        in_specs=[
            pl.BlockSpec((tm, d), lambda i, j: (i, 0)),
            pl.BlockSpec((1, d), lambda i, j: (0, 0)),
            pl.BlockSpec((d, _TN), lambda i, j: (0, j)),
            pl.BlockSpec((d, N_HEADS_C), lambda i, j: (0, 0)),
            pl.BlockSpec((1, HEAD_DIM), lambda i, j: (0, 0)),
            pl.BlockSpec((1, HEAD_DIM), lambda i, j: (0, 0)),
            pl.BlockSpec((1, N_HEADS_C), lambda i, j: (0, 0)),
        ],
        out_specs=[
            pl.BlockSpec((tm, _TN), lambda i, j: (i, j)),
            pl.BlockSpec((tm, N_HEADS_C), lambda i, j: (i, 0)),
        ],
        out_shape=[jax.ShapeDtypeStruct((n, O_OUT), F32),
                   jax.ShapeDtypeStruct((n, N_HEADS_C), F32)],
        scratch_shapes=[pltpu.VMEM((tm, d), BF16)],
        compiler_params=_cparams(2),
        name="odd_project",
    )(x, g.reshape(1, d), w_in, w_f, q_norm.reshape(1, -1), k_norm.reshape(1, -1),
      f_bias.reshape(1, -1))


def _out_proj_kernel(a_ref, w_ref, x_ref, o_ref, lhs_ref):
    @pl.when(pl.program_id(1) == 0)
    def _():
        lhs_ref[...] = a_ref[...].astype(BF16)

    o_ref[...] = x_ref[...] + _dot(lhs_ref[...], w_ref[...].astype(BF16))


def out_project(a, w, x, tm, tn=512):
    n, d = x.shape
    k = a.shape[1]
    return pl.pallas_call(
        _out_proj_kernel,
        grid=(n // tm, d // tn),
        in_specs=[
            pl.BlockSpec((tm, k), lambda i, j: (i, 0)),
            pl.BlockSpec((k, tn), lambda i, j: (0, j)),
            pl.BlockSpec((tm, tn), lambda i, j: (i, j)),
        ],
        out_specs=pl.BlockSpec((tm, tn), lambda i, j: (i, j)),
        out_shape=jax.ShapeDtypeStruct((n, d), F32),
        scratch_shapes=[pltpu.VMEM((tm, k), BF16)],
        compiler_params=_cparams(2),
        name="out_project",
    )(a, w, x)


def _t5_bucket(dist):
    d = jnp.maximum(dist, 0)
    max_exact = REL_BUCKETS // 2
    ratio = jnp.log(jnp.maximum(d, 1).astype(F32) / max_exact) / math.log(REL_MAX_DIST / max_exact)
    large = jnp.minimum(max_exact + (ratio * (REL_BUCKETS - max_exact)).astype(jnp.int32),
                        REL_BUCKETS - 1)
    return jnp.where(d < max_exact, d, large)


def _bias_from_buckets(bucket, rel_ref, h):
    out = jnp.zeros(bucket.shape, F32)
    for b in range(REL_BUCKETS):
        out = jnp.where(bucket == b, rel_ref[b, h], out)
    return out


def _swa_prompt_kernel(q_ref, kp_ref, kc_ref, vp_ref, vc_ref, bkt_ref, rel_ref, sink_ref,
                       o_ref, bias_ref):
    n = pl.program_id(1)
    kv = pl.program_id(2)

    @pl.when((pl.program_id(0) == 0) & (n == 0) & (kv == 0))
    def _():
        for h in range(N_HEADS_A):
            bias_ref[h] = _bias_from_buckets(bkt_ref[...], rel_ref, h)

    kk = jnp.concatenate([kp_ref[...], kc_ref[...]], axis=0).astype(BF16)
    vv = jnp.concatenate([vp_ref[...], vc_ref[...]], axis=0).astype(BF16)
    row = lax.broadcasted_iota(jnp.int32, (WINDOW, 2 * WINDOW), 0)
    col = lax.broadcasted_iota(jnp.int32, (WINDOW, 2 * WINDOW), 1)
    dist = WINDOW + row - col
    valid = (dist >= 0) & (dist < WINDOW) & ((n > 0) | (col >= WINDOW))
    for g in range(G_A):
        h = kv * G_A + g
        sl = slice(g * HEAD_DIM, (g + 1) * HEAD_DIM)
        s = _dot_nt(q_ref[:, sl].astype(BF16), kk) * SCALE + bias_ref[h]
        s = jnp.where(valid, s, NEG)
        sk = sink_ref[h]
        m = jnp.maximum(jnp.max(s, axis=-1, keepdims=True), sk)
        e = jnp.exp(s - m)
        p = e / (jnp.sum(e, axis=-1, keepdims=True) + jnp.exp(sk - m))
        o_ref[:, sl] = _dot(p.astype(BF16), vv)


def swa_prompt(proj, rel_bias, sinks, batch, seq):
    nb = seq // WINDOW
    i = jnp.arange(WINDOW)[:, None]
    j = jnp.arange(2 * WINDOW)[None, :]
    bucket = _t5_bucket(WINDOW + i - j).astype(jnp.int32)
    kcol, vcol = E_KA // HEAD_DIM, E_VA // HEAD_DIM
    smem = pl.BlockSpec(memory_space=pltpu.SMEM)
    return pl.pallas_call(
        _swa_prompt_kernel,
        grid=(batch, nb, N_KV_A),
        in_specs=[
            pl.BlockSpec((WINDOW, G_A * HEAD_DIM), lambda b, n, k: (b * nb + n, k)),
            pl.BlockSpec((WINDOW, HEAD_DIM), lambda b, n, k: (b * nb + jnp.maximum(n - 1, 0), kcol + k)),
            pl.BlockSpec((WINDOW, HEAD_DIM), lambda b, n, k: (b * nb + n, kcol + k)),
            pl.BlockSpec((WINDOW, HEAD_DIM), lambda b, n, k: (b * nb + jnp.maximum(n - 1, 0), vcol + k)),
            pl.BlockSpec((WINDOW, HEAD_DIM), lambda b, n, k: (b * nb + n, vcol + k)),
            pl.BlockSpec((WINDOW, 2 * WINDOW), lambda b, n, k: (0, 0)),
            smem, smem,
        ],
        out_specs=pl.BlockSpec((WINDOW, G_A * HEAD_DIM), lambda b, n, k: (b * nb + n, k)),
        out_shape=jax.ShapeDtypeStruct((batch * seq, D_A), F32),
        scratch_shapes=[pltpu.VMEM((N_HEADS_A, WINDOW, 2 * WINDOW), F32)],
        compiler_params=_cparams(3),
        name="swa_prompt",
    )(proj, proj, proj, proj, proj, bucket, rel_bias, sinks)


_SWA_SB = 8


def _swa_sample_kernel(q_ref, kn_ref, vn_ref, bk_ref, bv_ref, bkt_ref, rel_ref, sink_ref,
                       o_ref, wk_ref, wv_ref, bias_ref, *, ds, win):
    rows = G_A * ds
    nkeys = win + ds

    @pl.when(pl.program_id(0) == 0)
    def _():
        for h in range(N_HEADS_A):
            kv, g = divmod(h, G_A)
            bias_ref[kv, g * ds:(g + 1) * ds, :] = _bias_from_buckets(bkt_ref[...], rel_ref, h)

    t = lax.broadcasted_iota(jnp.int32, (rows, nkeys), 0) % ds
    s_idx = lax.broadcasted_iota(jnp.int32, (rows, nkeys), 1)
    dist = win + t - s_idx
    valid = (dist >= 0) & (dist < WINDOW)
    g_of_row = lax.broadcasted_iota(jnp.int32, (rows, 1), 0) // ds
    for kv in range(N_KV_A):
        sk = jnp.zeros((rows, 1), F32)
        for g in range(G_A):
            sk = jnp.where(g_of_row == g, sink_ref[kv * G_A + g], sk)
        csl = slice(kv * HEAD_DIM, (kv + 1) * HEAD_DIM)
        for b in range(_SWA_SB):
            rsl = slice(b * ds, (b + 1) * ds)
            q = jnp.concatenate(
                [q_ref[rsl, (kv * G_A + g) * HEAD_DIM:(kv * G_A + g + 1) * HEAD_DIM] for g in range(G_A)],
                axis=0).astype(BF16)
            kk = jnp.concatenate([bk_ref[b, :, csl], kn_ref[rsl, csl]], axis=0).astype(BF16)
            vv = jnp.concatenate([bv_ref[b, :, csl], vn_ref[rsl, csl]], axis=0).astype(BF16)
            s = _dot_nt(q, kk) * SCALE + bias_ref[kv]
            s = jnp.where(valid, s, NEG)
            m = jnp.maximum(jnp.max(s, axis=-1, keepdims=True), sk)
            e = jnp.exp(s - m)
            p = e / (jnp.sum(e, axis=-1, keepdims=True) + jnp.exp(sk - m))
            o = _dot(p.astype(BF16), vv)
            for g in range(G_A):
                o_ref[rsl, (kv * G_A + g) * HEAD_DIM:(kv * G_A + g + 1) * HEAD_DIM] = o[g * ds:(g + 1) * ds]
    for b in range(_SWA_SB):
        rsl = slice(b * ds, (b + 1) * ds)
        wk_ref[b, :win - ds, :] = bk_ref[b, ds:, :]
        wk_ref[b, win - ds:, :] = kn_ref[rsl, :]
        wv_ref[b, :win - ds, :] = bv_ref[b, ds:, :]
        wv_ref[b, win - ds:, :] = vn_ref[rsl, :]


def swa_sample(proj, row0, cache_k, cache_v, rel_bias, sinks, db, ds):
    win = cache_k.shape[1]
    rb = _SWA_SB * ds
    r0 = row0 // rb
    bucket = _t5_bucket(win + jnp.arange(ds)[:, None] - jnp.arange(win + ds)[None, :]).astype(jnp.int32)
    smem = pl.BlockSpec(memory_space=pltpu.SMEM)
    return pl.pallas_call(
        functools.partial(_swa_sample_kernel, ds=ds, win=win),
        grid=(db // _SWA_SB,),
        in_specs=[
            pl.BlockSpec((rb, D_A), lambda i: (r0 + i, 0)),
            pl.BlockSpec((rb, KVD_A), lambda i: (r0 + i, E_KA // KVD_A)),
            pl.BlockSpec((rb, KVD_A), lambda i: (r0 + i, E_VA // KVD_A)),
            pl.BlockSpec((_SWA_SB, win, KVD_A), lambda i: (i, 0, 0)),
            pl.BlockSpec((_SWA_SB, win, KVD_A), lambda i: (i, 0, 0)),
            pl.BlockSpec((ds, win + ds), lambda i: (0, 0)),
            smem, smem,
        ],
        out_specs=[
            pl.BlockSpec((rb, D_A), lambda i: (i, 0)),
            pl.BlockSpec((_SWA_SB, win, KVD_A), lambda i: (i, 0, 0)),
            pl.BlockSpec((_SWA_SB, win, KVD_A), lambda i: (i, 0, 0)),
        ],
        out_shape=[jax.ShapeDtypeStruct((db * ds, D_A), F32),
                   jax.ShapeDtypeStruct(cache_k.shape, F32),
                   jax.ShapeDtypeStruct(cache_v.shape, F32)],
        scratch_shapes=[pltpu.VMEM((N_KV_A, G_A * ds, win + ds), F32)],
        compiler_params=_cparams(1),
        name="swa_sample",
    )(proj, proj, proj, cache_k, cache_v, bucket, rel_bias, sinks)


_HG_CHUNK = 128
_HG_LEVELS = int(math.log2(_HG_CHUNK))


def _hgrn_constants(c):
    levels = int(math.log2(c))
    t = np.arange(c)
    mats = [(t[None, :] <= t[:, None])]
    lvl = np.full((c, c), -1, np.int32)
    lvl[t, t] = levels
    for l in range(levels):
        m = c >> (l + 1)
        pivot = (t // (2 * m)) * (2 * m) + m
        mats.append(t[None, :] <= pivot[:, None])
        same = (t[:, None] // (2 * m)) == (t[None, :] // (2 * m))
        pair = same & ((t[:, None] % (2 * m)) >= m) & ((t[None, :] % (2 * m)) < m)
        lvl[pair] = l
    return np.concatenate(mats, axis=0).astype(np.float32), lvl


def _column_of(row):
    n = row.shape[1]
    r = lax.broadcasted_iota(jnp.int32, (n, n), 0)
    c = lax.broadcasted_iota(jnp.int32, (n, n), 1)
    return jnp.sum(jnp.where(r == c, jnp.broadcast_to(row, (n, n)), 0.0), axis=1, keepdims=True)


def _hgrn_prompt_kernel(q_ref, f_ref, v_ref, cm_ref, lvl_ref, o_ref, s_out_ref, s_ref):
    c = _HG_CHUNK
    ci = pl.program_id(2)

    @pl.when(ci == 0)
    def _():
        s_ref[...] = jnp.zeros_like(s_ref)

    lvl = lvl_ref[...]
    sums_all = _dot_exact01(cm_ref[...], jnp.log(f_ref[...]))
    for h in range(_HG_HP):
        hs = slice(h * HEAD_DIM, (h + 1) * HEAD_DIM)
        q = q_ref[:, hs]
        v = v_ref[:, hs].astype(BF16)
        k = 1.0 - f_ref[:, hs]
        sums = sums_all[:, hs]
        b = sums[:c]
        a = jnp.where(lvl == _HG_LEVELS, _dot_nt(q.astype(BF16), k.astype(BF16)), 0.0)
        for l in range(_HG_LEVELS):
            e = jnp.exp(-jnp.abs(b - sums[(l + 1) * c:(l + 2) * c]))
            a = a + jnp.where(lvl == l, _dot_nt((q * e).astype(BF16), (k * e).astype(BF16)), 0.0)
        s0 = s_ref[h]
        o_ref[:, hs] = _dot((q * jnp.exp(b)).astype(BF16), s0.astype(BF16)) + _dot(a.astype(BF16), v)
        b_last = b[c - 1:c, :]
        khat = (k * jnp.exp(b_last - b)).astype(BF16)
        s_new = _column_of(jnp.exp(b_last)) * s0 + _dot_tn(khat, v)
        s_ref[h] = s_new

        @pl.when(ci == pl.num_programs(2) - 1)
        def _():
            s_out_ref[0, h] = s_new


_HG_HP = 4


def hgrn_prompt(proj, batch, seq):
    c = _HG_CHUNK
    nc = seq // c
    hp = _HG_HP
    cmat, lvl = _hgrn_constants(c)
    w = hp * HEAD_DIM
    qc, fc, vc = E_QB // w, E_F // w, E_VB // w
    return pl.pallas_call(
        _hgrn_prompt_kernel,
        grid=(batch, N_HEADS_B // hp, nc),
        in_specs=[
            pl.BlockSpec((c, w), lambda b, h, i: (b * nc + i, qc + h)),
            pl.BlockSpec((c, w), lambda b, h, i: (b * nc + i, fc + h)),
            pl.BlockSpec((c, w), lambda b, h, i: (b * nc + i, vc + h)),
            pl.BlockSpec(cmat.shape, lambda b, h, i: (0, 0)),
            pl.BlockSpec(lvl.shape, lambda b, h, i: (0, 0)),
        ],
        out_specs=[
            pl.BlockSpec((c, w), lambda b, h, i: (b * nc + i, h)),
            pl.BlockSpec((1, hp, HEAD_DIM, HEAD_DIM), lambda b, h, i: (b, h, 0, 0)),
        ],
        out_shape=[jax.ShapeDtypeStruct((batch * seq, D_B), F32),
                   jax.ShapeDtypeStruct((batch, N_HEADS_B, HEAD_DIM, HEAD_DIM), F32)],
        scratch_shapes=[pltpu.VMEM((hp, HEAD_DIM, HEAD_DIM), F32)],
        compiler_params=_cparams(3),
        name="hgrn_prompt",
    )(proj, proj, proj, jnp.asarray(cmat), jnp.asarray(lvl))


_HG_SB = 8


def _hgrn_sample_kernel(q_ref, f_ref, v_ref, st_ref, o_ref, s_out_ref, *, ds):
    s_idx = lax.broadcasted_iota(jnp.int32, (ds, 1), 0)
    for sb in range(_HG_SB):
        rsl = slice(sb * ds, (sb + 1) * ds)
        q = q_ref[rsl, :]
        f = f_ref[rsl, :]
        v = v_ref[rsl, :]
        k = 1.0 - f
        lf = jnp.log(f)
        rows = [lf[0:1]]
        for t in range(1, ds):
            rows.append(rows[-1] + lf[t:t + 1])
        b = jnp.concatenate(rows, axis=0)
        s0 = st_ref[sb, 0]
        o_inter = _dot((q * jnp.exp(b)).astype(BF16), s0.astype(BF16))
        outs = []
        for t in range(ds):
            w = q[t:t + 1] * k * jnp.exp(jnp.minimum(b[t:t + 1] - b, 0.0))
            a_t = jnp.where(s_idx <= t, jnp.sum(w, axis=-1, keepdims=True), 0.0)
            outs.append(jnp.sum(a_t * v, axis=0, keepdims=True))
        o_ref[rsl, :] = o_inter + jnp.concatenate(outs, axis=0)
        b_last = b[ds - 1:ds, :]
        khat = (k * jnp.exp(b_last - b)).astype(BF16)
        s_out_ref[sb, 0] = _column_of(jnp.exp(b_last)) * s0 + _dot_tn(khat, v.astype(BF16))


def hgrn_sample(proj, row0, state, db, ds):
    rb = _HG_SB * ds
    r0 = row0 // rb
    qc, fc, vc = E_QB // HEAD_DIM, E_F // HEAD_DIM, E_VB // HEAD_DIM
    return pl.pallas_call(
        functools.partial(_hgrn_sample_kernel, ds=ds),
        grid=(db // _HG_SB, N_HEADS_B),
        in_specs=[
            pl.BlockSpec((rb, HEAD_DIM), lambda i, h: (r0 + i, qc + h)),
            pl.BlockSpec((rb, HEAD_DIM), lambda i, h: (r0 + i, fc + h)),
            pl.BlockSpec((rb, HEAD_DIM), lambda i, h: (r0 + i, vc + h)),
            pl.BlockSpec((_HG_SB, 1, HEAD_DIM, HEAD_DIM), lambda i, h: (i, h, 0, 0)),
        ],
        out_specs=[
            pl.BlockSpec((rb, HEAD_DIM), lambda i, h: (i, h)),
            pl.BlockSpec((_HG_SB, 1, HEAD_DIM, HEAD_DIM), lambda i, h: (i, h, 0, 0)),
        ],
        out_shape=[jax.ShapeDtypeStruct((db * ds, D_B), F32),
                   jax.ShapeDtypeStruct(state.shape, F32)],
        compiler_params=_cparams(2),
        name="hgrn_sample",
    )(proj, proj, proj, state)


_CS_BLK = 128


def _split3(x):
    x1 = x.astype(BF16).astype(F32)
    r = x - x1
    x2 = r.astype(BF16).astype(F32)
    return x1, x2, r - x2


_AUG_ONES = 3 * N_HEADS_C
_INV_SCALE = HEAD_DIM ** 0.5
_LOG2E = 1.4426950408889634


def _cumsum_kernel(lf_ref, tri_ref, place_ref, c_ref, aug_ref):
    nblk = lf_ref.shape[1] // _CS_BLK
    lane = lax.broadcasted_iota(jnp.int32, (_CS_BLK, HEAD_DIM), 1)
    ones = jnp.where((lane >= _AUG_ONES) & (lane < _AUG_ONES + 3), 1.0, 0.0)

    def body(i, carry):
        r = pl.ds(pl.multiple_of(i * _CS_BLK, _CS_BLK), _CS_BLK)
        c = _dot_exact01(tri_ref[...], lf_ref[0, r, :]) + carry
        c_ref[0, r, :] = c
        aug = ones
        for j, piece in enumerate(_split3(c * _INV_SCALE)):
            aug = aug + _dot(piece.astype(BF16), place_ref[j].astype(BF16))
        aug_ref[0, r, :] = aug.astype(BF16)
        return c[_CS_BLK - 1:_CS_BLK, :]

    lax.fori_loop(0, nblk, body, jnp.zeros((1, lf_ref.shape[2]), F32))


def cumsum_rows(lf):
    b, s, h = lf.shape
    t = np.arange(_CS_BLK)
    tri = jnp.asarray((t[None, :] <= t[:, None]).astype(np.float32))
    place = np.zeros((3, h, HEAD_DIM), np.float32)
    for j in range(3):
        place[j, np.arange(h), j * h + np.arange(h)] = 1.0
    return pl.pallas_call(
        _cumsum_kernel,
        grid=(b,),
        in_specs=[pl.BlockSpec((1, s, h), lambda i: (i, 0, 0)),
                  pl.BlockSpec((_CS_BLK, _CS_BLK), lambda i: (0, 0)),
                  pl.BlockSpec((3, h, HEAD_DIM), lambda i: (0, 0, 0))],
        out_specs=[pl.BlockSpec((1, s, h), lambda i: (i, 0, 0)),
                   pl.BlockSpec((1, s, HEAD_DIM), lambda i: (i, 0, 0))],
        out_shape=[jax.ShapeDtypeStruct(lf.shape, F32),
                   jax.ShapeDtypeStruct((b, s, HEAD_DIM), BF16)],
        compiler_params=_cparams(1),
        name="fox_cumsum",
    )(lf, tri, jnp.asarray(place))


_FOX_TQ = 512
_FOX_TK = 512


def _fox_prompt_kernel(q_ref, k_ref, vt_ref, cq_ref, ca_ref, o_ref, qa_ref, m_ref, l_ref, acc_ref):
    kv = pl.program_id(1)
    qi = pl.program_id(2)
    ki = pl.program_id(3)
    tq, tk = q_ref.shape[0], k_ref.shape[0]

    @pl.when(ki == 0)
    def _():
        m_ref[...] = jnp.full_like(m_ref, NEG)
        l_ref[...] = jnp.zeros_like(l_ref)
        acc_ref[...] = jnp.zeros_like(acc_ref)
        lane = lax.broadcasted_iota(jnp.int32, (tq, HEAD_DIM), 1)
        for g in range(G_C):
            h = kv * G_C + g
            own = (lane == h) | (lane == N_HEADS_C + h) | (lane == 2 * N_HEADS_C + h)
            extra = jnp.where(own, -1.0, 0.0)
            for j, piece in enumerate(_split3(cq_ref[0, 0, :, g:g + 1] * _INV_SCALE)):
                extra = jnp.where(lane == _AUG_ONES + j, piece, extra)
            qa_ref[g, :, :HEAD_DIM] = q_ref[:, g * HEAD_DIM:(g + 1) * HEAD_DIM].astype(BF16)
            qa_ref[g, :, HEAD_DIM:] = extra.astype(BF16)

    def update(masked):
        kaug = jnp.concatenate([k_ref[...].astype(BF16), ca_ref[0]], axis=1)
        vt = vt_ref[0, 0].astype(BF16)
        if masked:
            causal = (lax.broadcasted_iota(jnp.int32, (tk, tq), 0)
                      <= lax.broadcasted_iota(jnp.int32, (tk, tq), 1))
        for g in range(G_C):
            s = _dot_nt(kaug, qa_ref[g]) * (SCALE * _LOG2E)
            if masked:
                s = jnp.where(causal, s, NEG)
            m_prev = m_ref[g]
            m_new = jnp.maximum(m_prev, jnp.max(s, axis=0, keepdims=True))
            alpha = jnp.exp2(m_prev - m_new)
            p = jnp.exp2(s - m_new)
            l_ref[g] = alpha * l_ref[g] + jnp.sum(p, axis=0, keepdims=True)
            acc_ref[g] = alpha * acc_ref[g] + _dot(vt, p.astype(BF16))
            m_ref[g] = m_new

    @pl.when(ki < qi)
    def _():
        update(False)

    @pl.when(ki == qi)
    def _():
        update(True)

    @pl.when(ki == pl.num_programs(3) - 1)
    def _():
        for g in range(G_C):
            o_ref[:, g * HEAD_DIM:(g + 1) * HEAD_DIM] = jnp.transpose(acc_ref[g] / l_ref[g])


def fox_prompt(projo, c, c_aug, batch, seq):
    tq = tk = min(_FOX_TQ, seq)
    nq, nk = seq // tq, seq // tk
    c_q = jnp.transpose(c.reshape(batch, seq, N_KV_C, G_C), (0, 2, 1, 3))
    v_t = jnp.transpose(projo[:batch * seq, O_V:O_V + KVD_C].reshape(batch, seq, N_KV_C, HEAD_DIM),
                        (0, 2, 3, 1))
    kcol = O_K // HEAD_DIM
    return pl.pallas_call(
        _fox_prompt_kernel,
        grid=(batch, N_KV_C, nq, nk),
        in_specs=[
            pl.BlockSpec((tq, G_C * HEAD_DIM), lambda b, k, i, j: (b * nq + i, k)),
            pl.BlockSpec((tk, HEAD_DIM), lambda b, k, i, j: (b * nk + jnp.minimum(j, i), kcol + k)),
            pl.BlockSpec((1, 1, HEAD_DIM, tk), lambda b, k, i, j: (b, k, 0, jnp.minimum(j, i))),
            pl.BlockSpec((1, 1, tq, G_C), lambda b, k, i, j: (b, k, i, 0)),
            pl.BlockSpec((1, tk, HEAD_DIM), lambda b, k, i, j: (b, jnp.minimum(j, i), 0)),
        ],
        out_specs=pl.BlockSpec((tq, G_C * HEAD_DIM), lambda b, k, i, j: (b * nq + i, k)),
        out_shape=jax.ShapeDtypeStruct((batch * seq, D_C), F32),
        scratch_shapes=[pltpu.VMEM((G_C, tq, 2 * HEAD_DIM), BF16),
                        pltpu.VMEM((G_C, 1, tq), F32), pltpu.VMEM((G_C, 1, tq), F32),
                        pltpu.VMEM((G_C, HEAD_DIM, tq), F32)],
        compiler_params=_cparams(4),
        name="fox_prompt",
    )(projo, projo, v_t, c_q, c_aug)


_FOX_PP = 16


def _fox_sample_kernel(pt_ref, q_ref, kn_ref, vn_ref, lfn_ref, tri_ref, ut1_ref, *refs, ds, page):
    del pt_ref
    pp = _FOX_PP
    k_refs, v_refs, lf_refs = refs[:pp], refs[pp:2 * pp], refs[2 * pp:3 * pp]
    o_ref, qb_ref, cq_ref, m_ref, l_ref, acc_ref, carry_ref = refs[3 * pp:]
    nrow = N_HEADS_C * ds
    rows_kv = G_C * ds
    step = pl.program_id(1)

    @pl.when(step == 0)
    def _():
        qb = q_ref[0].astype(BF16)
        qb_ref[...] = qb
        cn = _dot_exact01_r(lfn_ref[0], tri_ref[...])
        t_row = lax.broadcasted_iota(jnp.int32, (nrow, ds), 0) % ds
        s_col = lax.broadcasted_iota(jnp.int32, (nrow, ds), 1)
        cq = jnp.sum(jnp.where(s_col == t_row, cn, 0.0), axis=-1, keepdims=True)
        cq_ref[...] = cq
        causal = (lax.broadcasted_iota(jnp.int32, (rows_kv, ds), 1)
                  <= lax.broadcasted_iota(jnp.int32, (rows_kv, ds), 0) % ds)
        for kv in range(N_KV_C):
            rsl = slice(kv * rows_kv, (kv + 1) * rows_kv)
            csl = slice(kv * HEAD_DIM, (kv + 1) * HEAD_DIM)
            s = _dot_nt(qb_ref[rsl, :], kn_ref[:, csl].astype(BF16)) * SCALE + cq_ref[rsl, :] - cn[rsl]
            s = jnp.where(causal, s, NEG)
            m = jnp.max(s, axis=-1, keepdims=True)
            e = jnp.exp(s - m)
            m_ref[rsl, :] = m
            l_ref[rsl, :] = jnp.sum(e, axis=-1, keepdims=True)
            acc_ref[rsl, :] = _dot(e.astype(BF16), vn_ref[:, csl].astype(BF16))
        carry_ref[...] = jnp.zeros_like(carry_ref)

    lf_stack = jnp.concatenate([lf_refs[i][0] for i in range(pp)], axis=0)
    sums = _dot_exact01_r(lf_stack, ut1_ref[...])
    carry = carry_ref[...]
    dex = []
    for i in range(pp):
        hs = slice(i * N_HEADS_C, (i + 1) * N_HEADS_C)
        dex.append(sums[hs, :page] + carry)
        carry = carry + sums[hs, page:]
    carry_ref[...] = carry

    def kv_rows(refs_, kv):
        strided = pl.ds(kv, page, stride=N_KV_C)
        return jnp.concatenate([refs_[i][0, strided, :] for i in range(pp)], axis=0).astype(BF16)

    qk = jnp.concatenate(
        [_dot_nt(qb_ref[kv * rows_kv:(kv + 1) * rows_kv, :], kv_rows(k_refs, kv)) for kv in range(N_KV_C)],
        axis=0)
    bias = jnp.concatenate(
        [jnp.broadcast_to(d[:, None, :], (N_HEADS_C, ds, page)).reshape(nrow, page) for d in dex], axis=1)
    s = qk * SCALE + cq_ref[...] + bias
    m_prev = m_ref[...]
    m_new = jnp.maximum(m_prev, jnp.max(s, axis=-1, keepdims=True))
    alpha = jnp.exp(m_prev - m_new)
    p = jnp.exp(s - m_new)
    l_ref[...] = alpha * l_ref[...] + jnp.sum(p, axis=-1, keepdims=True)
    m_ref[...] = m_new
    pb = p.astype(BF16)
    pv = jnp.concatenate(
        [_dot(pb[kv * rows_kv:(kv + 1) * rows_kv, :], kv_rows(v_refs, kv)) for kv in range(N_KV_C)], axis=0)
    acc_ref[...] = alpha * acc_ref[...] + pv

    @pl.when(step == pl.num_programs(1) - 1)
    def _():
        o_ref[0] = acc_ref[...] / l_ref[...]


def fox_sample(projo, row0, lf_s, cache_k, cache_v, cache_lf, page_table, db, ds):
    pool, page = cache_lf.shape[0], cache_lf.shape[1]
    n_pages = page_table.shape[1]
    pp = _FOX_PP
    nrow = N_HEADS_C * ds
    q = projo[row0:row0 + db * ds, O_Q:O_Q + D_C].reshape(db, ds, N_HEADS_C, HEAD_DIM)
    q = jnp.transpose(q, (0, 2, 1, 3)).reshape(db, nrow, HEAD_DIM)
    lfn = jnp.transpose(lf_s.reshape(db, ds, N_HEADS_C), (0, 2, 1))
    lfn = jnp.broadcast_to(lfn[:, :, None, :], (db, N_HEADS_C, ds, ds)).reshape(db, nrow, ds)
    cache_lft = jnp.transpose(cache_lf, (0, 2, 1))
    t = np.arange(ds)
    tri = jnp.asarray((t[:, None] <= t[None, :]).astype(np.float32))
    u = np.arange(page)
    ut1 = jnp.asarray(np.concatenate([(u[:, None] > u[None, :]), np.ones((page, page), bool)],
                                     axis=1).astype(np.float32))
    r0 = row0 // ds

    def page_map(i):
        return lambda b, p, pt: (pt[b, n_pages - 1 - (p * pp + i)], 0, 0)

    in_specs = [
        pl.BlockSpec((1, nrow, HEAD_DIM), lambda b, p, pt: (b, 0, 0)),
        pl.BlockSpec((ds, KVD_C), lambda b, p, pt: (r0 + b, O_K // KVD_C)),
        pl.BlockSpec((ds, KVD_C), lambda b, p, pt: (r0 + b, O_V // KVD_C)),
        pl.BlockSpec((1, nrow, ds), lambda b, p, pt: (b, 0, 0)),
        pl.BlockSpec((ds, ds), lambda b, p, pt: (0, 0)),
        pl.BlockSpec((page, 2 * page), lambda b, p, pt: (0, 0)),
    ]
    in_specs += [pl.BlockSpec((1, page * N_KV_C, HEAD_DIM), page_map(i)) for i in range(pp)]
    in_specs += [pl.BlockSpec((1, page * N_KV_C, HEAD_DIM), page_map(i)) for i in range(pp)]
    in_specs += [pl.BlockSpec((1, N_HEADS_C, page), page_map(i)) for i in range(pp)]
    out = pl.pallas_call(
        functools.partial(_fox_sample_kernel, ds=ds, page=page),
        grid_spec=pltpu.PrefetchScalarGridSpec(
            num_scalar_prefetch=1,
            grid=(db, n_pages // pp),
            in_specs=in_specs,
            out_specs=pl.BlockSpec((1, nrow, HEAD_DIM), lambda b, p, pt: (b, 0, 0)),
            scratch_shapes=[pltpu.VMEM((nrow, HEAD_DIM), BF16), pltpu.VMEM((nrow, 1), F32),
                            pltpu.VMEM((nrow, 1), F32), pltpu.VMEM((nrow, 1), F32),
                            pltpu.VMEM((nrow, HEAD_DIM), F32), pltpu.VMEM((N_HEADS_C, page), F32)],
        ),
        out_shape=jax.ShapeDtypeStruct((db, nrow, HEAD_DIM), F32),
        compiler_params=_cparams(2),
        name="fox_sample",
    )(page_table, q, projo, projo, lfn, tri, ut1, *([cache_k] * pp), *([cache_v] * pp),
      *([cache_lft] * pp))
    out = jnp.transpose(out.reshape(db, N_HEADS_C, ds, HEAD_DIM), (0, 2, 1, 3))
    return out.reshape(db * ds, D_C)


def _moe_route_kernel(x_ref, g_ref, r_ref, idx_ref, w_ref):
    hn = _rms(x_ref[...], g_ref[...])
    logits = jnp.dot(hn, r_ref[...], preferred_element_type=F32, precision=lax.Precision.HIGHEST)
    lane = lax.broadcasted_iota(jnp.int32, logits.shape, 1)
    m1 = jnp.max(logits, axis=-1, keepdims=True)
    i1 = jnp.min(jnp.where(logits == m1, lane, N_EXPERTS), axis=-1, keepdims=True)
    rest = jnp.where(lane == i1, -jnp.inf, logits)
    m2 = jnp.max(rest, axis=-1, keepdims=True)
    i2 = jnp.min(jnp.where(rest == m2, lane, N_EXPERTS), axis=-1, keepdims=True)
    e = jnp.exp(m2 - m1)
    slot = lax.broadcasted_iota(jnp.int32, idx_ref.shape, 1)
    idx_ref[...] = jnp.where(slot == 0, i1, i2)
    w_ref[...] = jnp.where(slot == 0, 1.0 / (1.0 + e), e / (1.0 + e))


def moe_route(x, g, router, tm):
    n, d = x.shape
    return pl.pallas_call(
        _moe_route_kernel,
        grid=(n // tm,),
        in_specs=[pl.BlockSpec((tm, d), lambda i: (i, 0)),
                  pl.BlockSpec((1, d), lambda i: (0, 0)),
                  pl.BlockSpec((d, N_EXPERTS), lambda i: (0, 0))],
        out_specs=[pl.BlockSpec((tm, TOP_K), lambda i: (i, 0)),
                   pl.BlockSpec((tm, TOP_K), lambda i: (i, 0))],
        out_shape=[jax.ShapeDtypeStruct((n, TOP_K), jnp.int32),
                   jax.ShapeDtypeStruct((n, TOP_K), F32)],
        compiler_params=_cparams(1),
        name="moe_route",
    )(x, g.reshape(1, d), router)


def _route_plan(idx, w, tm):
    n = idx.shape[0]
    flat_e = idx.reshape(-1)
    onehot = (flat_e[:, None] == jnp.arange(N_EXPERTS, dtype=jnp.int32)[None, :]).astype(jnp.int32)
    csum = jnp.cumsum(onehot, axis=0)
    rank = jnp.take_along_axis(csum, flat_e[:, None], axis=1)[:, 0] - 1
    count = csum[-1]
    tiles_e = (count + tm - 1) // tm
    tile_end = jnp.cumsum(tiles_e)
    tile_start = tile_end - tiles_e
    pos = tile_start[flat_e] * tm + rank
    n_tiles = (TOP_K * n) // tm + N_EXPERTS
    token = jnp.zeros((n_tiles * tm,), jnp.int32).at[pos].set(jnp.arange(TOP_K * n, dtype=jnp.int32) // TOP_K)
    gate = jnp.zeros((n_tiles * tm,), F32).at[pos].set(w.reshape(-1))
    tile = jnp.arange(n_tiles, dtype=jnp.int32)
    tile_expert = jnp.minimum(jnp.searchsorted(tile_end, tile, side="right"), N_EXPERTS - 1).astype(jnp.int32)
    tile_valid = jnp.clip(count[tile_expert] - (tile - tile_start[tile_expert]) * tm, 0, tm)
    tile_valid = jnp.where(tile < tile_end[-1], tile_valid, 0).astype(jnp.int32)
    return pos.reshape(n, TOP_K), token, gate.reshape(-1, 1), tile_expert, tile_valid


def moe_layer(x, g, router, w_gate, w_up, w_down, tm_route):
    tm = min(_FFN_TM, x.shape[0])
    idx, w = moe_route(x, g, router, tm_route)
    pos, token, gate, tile_expert, tile_valid = _route_plan(idx, w, tm)
    x_sorted = norm_cast(jnp.take(x, token, axis=0, mode="clip"), g, tm)
    act = ffn_up(x_sorted, tile_expert, tile_valid, w_gate, w_up)
    y_sorted = ffn_down(act, tile_expert, tile_valid, w_down, gate=gate)
    return (x + jnp.take(y_sorted, pos[:, 0], axis=0, mode="clip")
            + jnp.take(y_sorted, pos[:, 1], axis=0, mode="clip"))


def _row_tile(n, want):
    t = min(want, n)
    while n % t:
        t //= 2
    return t


def kernel(x_prompt, x_sample, cache_k_win, cache_v_win, state_hgrn, cache_k_fox, cache_v_fox, cache_logf_fox, page_table, rel_bias, norm_mix_e, w_in_e, q_norm_a, k_norm_a, sinks_a, lb_b, o_norm_b, w_out_e, norm_ffn_e, w_gate_e, w_up_e, w_down_e, norm_mix_o, w_in_o, fgate_bias_c, q_norm_c, k_norm_c, w_out_o, norm_ffn_o, router_o, w_gate_x, w_up_x, w_down_x):
    batch, seq, d = x_prompt.shape
    db, ds, _ = x_sample.shape
    n_p, n_s = batch * seq, db * ds
    n = n_p + n_s
    win = cache_k_win.shape[2]
    pool, page = cache_k_fox.shape[1], cache_k_fox.shape[2]
    tm_big = _row_tile(n, 1024)

    x = jnp.concatenate([x_prompt.reshape(n_p, d), x_sample.reshape(n_s, d)], axis=0)

    proj = even_project(x, norm_mix_e[0], w_in_e[0], q_norm_a[0], k_norm_a[0], lb_b, tm_big)
    oa_p = swa_prompt(proj, rel_bias, sinks_a[0], batch, seq)
    oa_s, wk_s, wv_s = swa_sample(proj, n_p, cache_k_win[0].reshape(db, win, KVD_A),
                                  cache_v_win[0].reshape(db, win, KVD_A), rel_bias, sinks_a[0], db, ds)
    ob_p, hg_p = hgrn_prompt(proj, batch, seq)
    ob_s, hg_s = hgrn_sample(proj, n_p, state_hgrn[0], db, ds)
    oa = jnp.concatenate([oa_p, oa_s], axis=0)
    ob = jnp.concatenate([ob_p, ob_s], axis=0)
    y = even_merge(oa, ob, proj, o_norm_b[0], w_out_e[0], x, tm_big)
    y = dense_ffn(y, norm_ffn_e[0], w_gate_e[0], w_up_e[0], w_down_e[0])

    projo, lf = odd_project(y, norm_mix_o[0], w_in_o[0], q_norm_c[0], k_norm_c[0], fgate_bias_c[0], tm_big)
    c, c_aug = cumsum_rows(lf[:n_p].reshape(batch, seq, N_HEADS_C))
    oc_p = fox_prompt(projo, c, c_aug, batch, seq)
    oc_s = fox_sample(projo, n_p, lf[n_p:], cache_k_fox[0].reshape(pool, page * N_KV_C, HEAD_DIM),
                      cache_v_fox[0].reshape(pool, page * N_KV_C, HEAD_DIM), cache_logf_fox[0],
                      page_table, db, ds)
    y = out_project(jnp.concatenate([oc_p, oc_s], axis=0), w_out_o[0], y, tm_big)
    y = moe_layer(y, norm_ffn_o[0], router_o[0], w_gate_x[0], w_up_x[0], w_down_x[0], tm_big)

    def prompt_tail(col, width, heads):
        t = proj[:n_p, col:col + width].reshape(batch, seq, heads, HEAD_DIM)
        return t[:, seq - WINDOW:][None]

    return (
        y[:n_p].reshape(batch, seq, d),
        y[n_p:].reshape(db, ds, d),
        prompt_tail(E_KA, KVD_A, N_KV_A),
        prompt_tail(E_VA, KVD_A, N_KV_A),
        wk_s.reshape(1, db, win, N_KV_A, HEAD_DIM),
        wv_s.reshape(1, db, win, N_KV_A, HEAD_DIM),
        hg_p[None],
        hg_s[None],
        projo[:n_p, O_K:O_K + KVD_C].reshape(1, batch, seq, N_KV_C, HEAD_DIM),
        projo[:n_p, O_V:O_V + KVD_C].reshape(1, batch, seq, N_KV_C, HEAD_DIM),
        lf[:n_p].reshape(1, batch, seq, N_HEADS_C),
        projo[n_p:, O_K:O_K + KVD_C].reshape(1, db, ds, N_KV_C, HEAD_DIM),
        projo[n_p:, O_V:O_V + KVD_C].reshape(1, db, ds, N_KV_C, HEAD_DIM),
        lf[n_p:].reshape(1, db, ds, N_HEADS_C),
    )
```

```python
import functools
import math

import numpy as np
import jax
import jax.numpy as jnp
from jax import lax
from jax.experimental import pallas as pl
from jax.experimental.pallas import tpu as pltpu

F32 = jnp.float32
BF16 = jnp.bfloat16

HEAD_DIM = 128
N_HEADS_A, N_KV_A = 8, 2
G_A = N_HEADS_A // N_KV_A
WINDOW = 128
REL_BUCKETS, REL_MAX_DIST = 32, 128
N_HEADS_B = 8
N_HEADS_C, N_KV_C = 16, 4
G_C = N_HEADS_C // N_KV_C
N_EXPERTS, TOP_K = 8, 2
EPS = 1e-6
SCALE = HEAD_DIM ** -0.5
NEG = -1e30

D_A = N_HEADS_A * HEAD_DIM
KVD_A = N_KV_A * HEAD_DIM
D_B = N_HEADS_B * HEAD_DIM
D_C = N_HEADS_C * HEAD_DIM
KVD_C = N_KV_C * HEAD_DIM

E_QA, E_QB, E_F, E_VB, E_GB = 0, D_A, D_A + D_B, D_A + 2 * D_B, D_A + 3 * D_B
E_KA = D_A + 4 * D_B
E_VA = E_KA + KVD_A
E_OUT = E_VA + KVD_A
O_Q, O_K, O_V = 0, D_C, D_C + KVD_C
O_OUT = D_C + 2 * KVD_C

VMEM_LIMIT = 56 * 1024 * 1024


def _cparams(n_axes):
    return pltpu.CompilerParams(dimension_semantics=("arbitrary",) * n_axes,
                                vmem_limit_bytes=VMEM_LIMIT)


def _rms(x, gain):
    return x * lax.rsqrt(jnp.mean(x * x, axis=-1, keepdims=True) + EPS) * gain


def _silu(x):
    return x * (1.0 / (1.0 + jnp.exp(-x)))


def _dot(a, b):
    return jnp.dot(a, b, preferred_element_type=F32)


def _dot_nt(a, b):
    return lax.dot_general(a, b, (((1,), (1,)), ((), ())), preferred_element_type=F32)


def _dot_tn(a, b):
    return lax.dot_general(a, b, (((0,), (0,)), ((), ())), preferred_element_type=F32)


def _dot_exact01(m01, x):
    x1 = x.astype(BF16)
    r1 = x - x1.astype(F32)
    x2 = r1.astype(BF16)
    x3 = (r1 - x2.astype(F32)).astype(BF16)
    m = m01.astype(BF16)
    return _dot(m, x1) + _dot(m, x2) + _dot(m, x3)


def _dot_exact01_r(x, m01):
    x1 = x.astype(BF16)
    r1 = x - x1.astype(F32)
    x2 = r1.astype(BF16)
    x3 = (r1 - x2.astype(F32)).astype(BF16)
    m = m01.astype(BF16)
    return _dot(x1, m) + _dot(x2, m) + _dot(x3, m)


_TN = 256


def _even_out_tile(j):
    return jnp.where(j < 4, j, jnp.where(j < 6, j + 16, j - 2))


def _even_proj_kernel(x_ref, g_ref, w_ref, qn_ref, kn_ref, lb_ref, o_ref, hn_ref):
    j = pl.program_id(1)

    @pl.when(j == 0)
    def _():
        hn_ref[...] = _rms(x_ref[...], g_ref[...]).astype(BF16)

    acc = _dot(hn_ref[...], w_ref[...].astype(BF16))

    def head_norm(gain):
        for h in range(_TN // HEAD_DIM):
            sl = slice(h * HEAD_DIM, (h + 1) * HEAD_DIM)
            o_ref[:, sl] = _rms(acc[:, sl], gain)

    @pl.when(j < 4)
    def _():
        head_norm(qn_ref[...])

    @pl.when(j == 4)
    def _():
        head_norm(kn_ref[...])

    @pl.when((j == 5) | (j >= 14))
    def _():
        o_ref[...] = acc

    @pl.when((j >= 6) & (j < 10))
    def _():
        o_ref[...] = _silu(acc)

    @pl.when((j >= 10) & (j < 14))
    def _():
        lb = lb_ref[...]
        e = jnp.exp(lb - jnp.max(lb, axis=0, keepdims=True))
        lb0 = e[0:1, :] / jnp.sum(e, axis=0, keepdims=True)
        o_ref[...] = lb0 + (1.0 - lb0) * (1.0 / (1.0 + jnp.exp(-acc)))


def even_project(x, g, w_in, q_norm, k_norm, lb_b, tm):
    n, d = x.shape
    n_tiles = w_in.shape[1] // _TN
    return pl.pallas_call(
        _even_proj_kernel,
        grid=(n // tm, n_tiles),
        in_specs=[
            pl.BlockSpec((tm, d), lambda i, j: (i, 0)),
            pl.BlockSpec((1, d), lambda i, j: (0, 0)),
            pl.BlockSpec((d, _TN), lambda i, j: (0, j)),
            pl.BlockSpec((1, HEAD_DIM), lambda i, j: (0, 0)),
            pl.BlockSpec((1, HEAD_DIM), lambda i, j: (0, 0)),
            pl.BlockSpec((lb_b.shape[0], _TN), lambda i, j: (0, jnp.clip(j - 10, 0, 3))),
        ],
        out_specs=pl.BlockSpec((tm, _TN), lambda i, j: (i, _even_out_tile(j))),
        out_shape=jax.ShapeDtypeStruct((n, E_OUT), F32),
        scratch_shapes=[pltpu.VMEM((tm, d), BF16)],
        compiler_params=_cparams(2),
        name="even_project",
    )(x, g.reshape(1, d), w_in, q_norm.reshape(1, -1), k_norm.reshape(1, -1), lb_b)


def _split_rows_specs(shape_cols, tm, tiles_p):
    return [pl.BlockSpec((tm, shape_cols), lambda i, j: (jnp.minimum(i, tiles_p - 1), 0)),
            pl.BlockSpec((tm, shape_cols), lambda i, j: (jnp.maximum(i - tiles_p, 0), 0))]


def _even_merge_kernel(oap_ref, oas_ref, obp_ref, obs_ref, gb_ref, on_ref, w_ref, x_ref, o_ref, lhs_ref,
                       *, tiles_p):
    i = pl.program_id(0)

    def fill(oa_ref, ob_ref):
        lhs_ref[:, :D_A] = oa_ref[...].astype(BF16)
        for h in range(N_HEADS_B):
            sl = slice(h * HEAD_DIM, (h + 1) * HEAD_DIM)
            obn = _rms(ob_ref[:, sl], on_ref[...]) * _silu(gb_ref[:, sl])
            lhs_ref[:, D_A + h * HEAD_DIM:D_A + (h + 1) * HEAD_DIM] = obn.astype(BF16)

    @pl.when((pl.program_id(1) == 0) & (i < tiles_p))
    def _():
        fill(oap_ref, obp_ref)

    @pl.when((pl.program_id(1) == 0) & (i >= tiles_p))
    def _():
        fill(oas_ref, obs_ref)

    o_ref[...] = x_ref[...] + _dot(lhs_ref[...], w_ref[...].astype(BF16))


def even_merge(oa_p, oa_s, ob_p, ob_s, proj, o_norm, w_out, x, tm, tn=512):
    n, d = x.shape
    tiles_p = oa_p.shape[0] // tm
    return pl.pallas_call(
        functools.partial(_even_merge_kernel, tiles_p=tiles_p),
        grid=(n // tm, d // tn),
        in_specs=_split_rows_specs(D_A, tm, tiles_p) + _split_rows_specs(D_B, tm, tiles_p) + [
            pl.BlockSpec((tm, D_B), lambda i, j: (i, E_GB // D_B)),
            pl.BlockSpec((1, HEAD_DIM), lambda i, j: (0, 0)),
            pl.BlockSpec((D_A + D_B, tn), lambda i, j: (0, j)),
            pl.BlockSpec((tm, tn), lambda i, j: (i, j)),
        ],
        out_specs=pl.BlockSpec((tm, tn), lambda i, j: (i, j)),
        out_shape=jax.ShapeDtypeStruct((n, d), F32),
        scratch_shapes=[pltpu.VMEM((tm, D_A + D_B), BF16)],
        compiler_params=_cparams(2),
        name="even_merge",
    )(oa_p, oa_s, ob_p, ob_s, proj, o_norm.reshape(1, -1), w_out, x)


_FFN_TM = 512
_FFN_TF = 512
_FFN_TN = 512


def _norm_cast_kernel(x_ref, g_ref, o_ref):
    o_ref[...] = _rms(x_ref[...], g_ref[...]).astype(BF16)


def norm_cast(x, g, tm):
    n, d = x.shape
    return pl.pallas_call(
        _norm_cast_kernel,
        grid=(n // tm,),
        in_specs=[pl.BlockSpec((tm, d), lambda i: (i, 0)), pl.BlockSpec((1, d), lambda i: (0, 0))],
        out_specs=pl.BlockSpec((tm, d), lambda i: (i, 0)),
        out_shape=jax.ShapeDtypeStruct((n, d), BF16),
        compiler_params=_cparams(1),
        name="norm_cast",
    )(x, g.reshape(1, d))


def _group_start(te_ref, i):
    return (i == 0) | (te_ref[i] != te_ref[jnp.maximum(i - 1, 0)])


def _by_valid_rows(nv, tm, compute, o_ref):
    half = tm // 2

    @pl.when(nv > half)
    def _():
        o_ref[...] = compute(slice(None))

    @pl.when((nv > 0) & (nv <= half))
    def _():
        o_ref[:half, :] = compute(slice(0, half))
        o_ref[half:, :] = jnp.zeros((tm - half, o_ref.shape[1]), o_ref.dtype)

    @pl.when(nv == 0)
    def _():
        o_ref[...] = jnp.zeros_like(o_ref)


def _ffn_up_kernel(te_ref, nv_ref, x_ref, wg_ref, wu_ref, *rest):
    o_ref, wg_s, wu_s = rest[-3:]
    i = pl.program_id(1)

    @pl.when(_group_start(te_ref, i))
    def _():
        wg_s[...] = wg_ref[0].astype(BF16)
        wu_s[...] = wu_ref[0].astype(BF16)

    def compute(rows):
        x = x_ref[rows, :]
        return (_silu(_dot(x, wg_s[...])) * _dot(x, wu_s[...])).astype(BF16)

    _by_valid_rows(nv_ref[i], x_ref.shape[0], compute, o_ref)


def ffn_up(x, tile_expert, tile_valid, w_gate, w_up, out=None, tile0=0, total_tiles=None):
    p, d = x.shape
    d_ff = w_gate.shape[2]
    tm, tf = min(_FFN_TM, p), _FFN_TF
    rows_out = (total_tiles if total_tiles is not None else p // tm) * tm
    in_specs = [
        pl.BlockSpec((tm, d), lambda f, i, te, nv: (i, 0)),
        pl.BlockSpec((1, d, tf), lambda f, i, te, nv: (te[i], 0, f)),
        pl.BlockSpec((1, d, tf), lambda f, i, te, nv: (te[i], 0, f)),
    ]
    args = [x, w_gate, w_up]
    aliases = {}
    if out is not None:
        in_specs.append(pl.BlockSpec(memory_space=pl.ANY))
        args.append(out)
        aliases = {2 + len(args) - 1: 0}
    return pl.pallas_call(
        _ffn_up_kernel,
        grid_spec=pltpu.PrefetchScalarGridSpec(
            num_scalar_prefetch=2,
            grid=(d_ff // tf, p // tm),
            in_specs=in_specs,
            out_specs=pl.BlockSpec((tm, tf), lambda f, i, te, nv: (i + tile0, f)),
            scratch_shapes=[pltpu.VMEM((d, tf), BF16), pltpu.VMEM((d, tf), BF16)],
        ),
        out_shape=jax.ShapeDtypeStruct((rows_out, d_ff), BF16),
        input_output_aliases=aliases,
        compiler_params=_cparams(2),
        name="ffn_up",
    )(tile_expert, tile_valid, *args)


def _ffn_down_kernel(te_ref, nv_ref, a_ref, wd_ref, *rest, has_gate, has_res):
    rest = list(rest)
    gate_ref = rest.pop(0) if has_gate else None
    res_ref = rest.pop(0) if has_res else None
    o_ref, wd_s = rest
    i = pl.program_id(1)

    @pl.when(_group_start(te_ref, i))
    def _():
        wd_s[...] = wd_ref[0].astype(BF16)

    def compute(rows):
        y = _dot(a_ref[rows, :], wd_s[...])
        if has_gate:
            y = y * gate_ref[rows, :]
        if has_res:
            y = res_ref[rows, :] + y
        return y

    _by_valid_rows(nv_ref[i], a_ref.shape[0], compute, o_ref)


def ffn_down(act, tile_expert, tile_valid, w_down, gate=None, res=None):
    p, d_ff = act.shape
    d = w_down.shape[2]
    tm, tn = min(_FFN_TM, p), _FFN_TN
    in_specs = [
        pl.BlockSpec((tm, d_ff), lambda n, i, te, nv: (i, 0)),
        pl.BlockSpec((1, d_ff, tn), lambda n, i, te, nv: (te[i], 0, n)),
    ]
    args = [act, w_down]
    if gate is not None:
        in_specs.append(pl.BlockSpec((tm, 1), lambda n, i, te, nv: (i, 0)))
        args.append(gate)
    if res is not None:
        in_specs.append(pl.BlockSpec((tm, tn), lambda n, i, te, nv: (i, n)))
        args.append(res)
    return pl.pallas_call(
        functools.partial(_ffn_down_kernel, has_gate=gate is not None, has_res=res is not None),
        grid_spec=pltpu.PrefetchScalarGridSpec(
            num_scalar_prefetch=2,
            grid=(d // tn, p // tm),
            in_specs=in_specs,
            out_specs=pl.BlockSpec((tm, tn), lambda n, i, te, nv: (i, n)),
            scratch_shapes=[pltpu.VMEM((d_ff, tn), BF16)],
        ),
        out_shape=jax.ShapeDtypeStruct((p, d), F32),
        compiler_params=_cparams(2),
        name="ffn_down",
    )(tile_expert, tile_valid, *args)


def dense_ffn(x, g, w_gate, w_up, w_down):
    n = x.shape[0]
    tm = min(_FFN_TM, n)
    tiles = n // tm
    te = jnp.zeros((tiles,), jnp.int32)
    nv = jnp.full((tiles,), tm, jnp.int32)
    act = ffn_up(norm_cast(x, g, tm), te, nv, w_gate[None], w_up[None])
    return ffn_down(act, te, nv, w_down[None], res=x)


def _odd_proj_kernel(x_ref, g_ref, w_ref, wf_ref, qn_ref, kn_ref, fb_ref, o_ref, lf_ref, hn_ref):
    j = pl.program_id(1)
    nq = D_C // _TN
    nk = KVD_C // _TN

    @pl.when(j == 0)
    def _():
        hn = _rms(x_ref[...], g_ref[...]).astype(BF16)
        hn_ref[...] = hn
        z = _dot(hn, wf_ref[...].astype(BF16)) + fb_ref[...]
        lf_ref[...] = jnp.minimum(z, 0.0) - jnp.log(1.0 + jnp.exp(-jnp.abs(z)))

    acc = _dot(hn_ref[...], w_ref[...].astype(BF16))

    def head_norm(gain):
        for h in range(_TN // HEAD_DIM):
            sl = slice(h * HEAD_DIM, (h + 1) * HEAD_DIM)
            o_ref[:, sl] = _rms(acc[:, sl], gain)

    @pl.when(j < nq)
    def _():
        head_norm(qn_ref[...])

    @pl.when((j >= nq) & (j < nq + nk))
    def _():
        head_norm(kn_ref[...])

    @pl.when(j >= nq + nk)
    def _():
        o_ref[...] = acc


def odd_project(x, g, w_in, q_norm, k_norm, f_bias, tm):
    n, d = x.shape
    w_f = w_in[:, O_OUT:]
    return pl.pallas_call(
        _odd_proj_kernel,
        grid=(n // tm, O_OUT // _TN),
        in_specs=[
            pl.BlockSpec((tm, d), lambda i, j: (i, 0)),
            pl.BlockSpec((1, d), lambda i, j: (0, 0)),
            pl.BlockSpec((d, _TN), lambda i, j: (0, j)),
            pl.BlockSpec((d, N_HEADS_C), lambda i, j: (0, 0)),
            pl.BlockSpec((1, HEAD_DIM), lambda i, j: (0, 0)),
            pl.BlockSpec((1, HEAD_DIM), lambda i, j: (0, 0)),
            pl.BlockSpec((1, N_HEADS_C), lambda i, j: (0, 0)),
        ],
        out_specs=[
            pl.BlockSpec((tm, _TN), lambda i, j: (i, j)),
            pl.BlockSpec((tm, N_HEADS_C), lambda i, j: (i, 0)),
        ],
        out_shape=[jax.ShapeDtypeStruct((n, O_OUT), F32),
                   jax.ShapeDtypeStruct((n, N_HEADS_C), F32)],
        scratch_shapes=[pltpu.VMEM((tm, d), BF16)],
        compiler_params=_cparams(2),
        name="odd_project",
    )(x, g.reshape(1, d), w_in, w_f, q_norm.reshape(1, -1), k_norm.reshape(1, -1),
      f_bias.reshape(1, -1))


def _out_proj_kernel(ap_ref, as_ref, w_ref, x_ref, o_ref, lhs_ref, *, tiles_p):
    i = pl.program_id(0)

    @pl.when((pl.program_id(1) == 0) & (i < tiles_p))
    def _():
        lhs_ref[...] = ap_ref[...].astype(BF16)

    @pl.when((pl.program_id(1) == 0) & (i >= tiles_p))
    def _():
        lhs_ref[...] = as_ref[...].astype(BF16)

    o_ref[...] = x_ref[...] + _dot(lhs_ref[...], w_ref[...].astype(BF16))


def out_project(a_p, a_s, w, x, tm, tn=512):
    n, d = x.shape
    k = a_p.shape[1]
    tiles_p = a_p.shape[0] // tm
    return pl.pallas_call(
        functools.partial(_out_proj_kernel, tiles_p=tiles_p),
        grid=(n // tm, d // tn),
        in_specs=_split_rows_specs(k, tm, tiles_p) + [
            pl.BlockSpec((k, tn), lambda i, j: (0, j)),
            pl.BlockSpec((tm, tn), lambda i, j: (i, j)),
        ],
        out_specs=pl.BlockSpec((tm, tn), lambda i, j: (i, j)),
        out_shape=jax.ShapeDtypeStruct((n, d), F32),
        scratch_shapes=[pltpu.VMEM((tm, k), BF16)],
        compiler_params=_cparams(2),
        name="out_project",
    )(a_p, a_s, w, x)


def _t5_bucket(dist):
    d = jnp.maximum(dist, 0)
    max_exact = REL_BUCKETS // 2
    ratio = jnp.log(jnp.maximum(d, 1).astype(F32) / max_exact) / math.log(REL_MAX_DIST / max_exact)
    large = jnp.minimum(max_exact + (ratio * (REL_BUCKETS - max_exact)).astype(jnp.int32),
                        REL_BUCKETS - 1)
    return jnp.where(d < max_exact, d, large)


def _bias_from_buckets(bucket, rel_ref, h):
    out = jnp.zeros(bucket.shape, F32)
    for b in range(REL_BUCKETS):
        out = jnp.where(bucket == b, rel_ref[b, h], out)
    return out


def _swa_prompt_kernel(q_ref, kp_ref, kc_ref, vp_ref, vc_ref, bkt_ref, rel_ref, sink_ref,
                       o_ref, bias_ref):
    n = pl.program_id(1)
    kv = pl.program_id(2)

    @pl.when((pl.program_id(0) == 0) & (n == 0) & (kv == 0))
    def _():
        for h in range(N_HEADS_A):
            bias_ref[h] = _bias_from_buckets(bkt_ref[...], rel_ref, h)

    kk = jnp.concatenate([kp_ref[...], kc_ref[...]], axis=0).astype(BF16)
    vv = jnp.concatenate([vp_ref[...], vc_ref[...]], axis=0).astype(BF16)
    row = lax.broadcasted_iota(jnp.int32, (WINDOW, 2 * WINDOW), 0)
    col = lax.broadcasted_iota(jnp.int32, (WINDOW, 2 * WINDOW), 1)
    dist = WINDOW + row - col
    valid = (dist >= 0) & (dist < WINDOW) & ((n > 0) | (col >= WINDOW))
    for g in range(G_A):
        h = kv * G_A + g
        sl = slice(g * HEAD_DIM, (g + 1) * HEAD_DIM)
        s = _dot_nt(q_ref[:, sl].astype(BF16), kk) * SCALE + bias_ref[h]
        s = jnp.where(valid, s, NEG)
        sk = sink_ref[h]
        m = jnp.maximum(jnp.max(s, axis=-1, keepdims=True), sk)
        e = jnp.exp(s - m)
        p = e / (jnp.sum(e, axis=-1, keepdims=True) + jnp.exp(sk - m))
        o_ref[:, sl] = _dot(p.astype(BF16), vv)


def swa_prompt(proj, rel_bias, sinks, batch, seq):
    nb = seq // WINDOW
    i = jnp.arange(WINDOW)[:, None]
    j = jnp.arange(2 * WINDOW)[None, :]
    bucket = _t5_bucket(WINDOW + i - j).astype(jnp.int32)
    kcol, vcol = E_KA // HEAD_DIM, E_VA // HEAD_DIM
    smem = pl.BlockSpec(memory_space=pltpu.SMEM)
    return pl.pallas_call(
        _swa_prompt_kernel,
        grid=(batch, nb, N_KV_A),
        in_specs=[
            pl.BlockSpec((WINDOW, G_A * HEAD_DIM), lambda b, n, k: (b * nb + n, k)),
            pl.BlockSpec((WINDOW, HEAD_DIM), lambda b, n, k: (b * nb + jnp.maximum(n - 1, 0), kcol + k)),
            pl.BlockSpec((WINDOW, HEAD_DIM), lambda b, n, k: (b * nb + n, kcol + k)),
            pl.BlockSpec((WINDOW, HEAD_DIM), lambda b, n, k: (b * nb + jnp.maximum(n - 1, 0), vcol + k)),
            pl.BlockSpec((WINDOW, HEAD_DIM), lambda b, n, k: (b * nb + n, vcol + k)),
            pl.BlockSpec((WINDOW, 2 * WINDOW), lambda b, n, k: (0, 0)),
            smem, smem,
        ],
        out_specs=pl.BlockSpec((WINDOW, G_A * HEAD_DIM), lambda b, n, k: (b * nb + n, k)),
        out_shape=jax.ShapeDtypeStruct((batch * seq, D_A), F32),
        scratch_shapes=[pltpu.VMEM((N_HEADS_A, WINDOW, 2 * WINDOW), F32)],
        compiler_params=_cparams(3),
        name="swa_prompt",
    )(proj, proj, proj, proj, proj, bucket, rel_bias, sinks)


_SWA_SB = 8


def _swa_sample_kernel(q_ref, kn_ref, vn_ref, bk_ref, bv_ref, bkt_ref, rel_ref, sink_ref,
                       o_ref, wk_ref, wv_ref, bias_ref, *, ds, win):
    rows = G_A * ds
    nkeys = win + ds

    @pl.when(pl.program_id(0) == 0)
    def _():
        for h in range(N_HEADS_A):
            kv, g = divmod(h, G_A)
            bias_ref[kv, g * ds:(g + 1) * ds, :] = _bias_from_buckets(bkt_ref[...], rel_ref, h)

    t = lax.broadcasted_iota(jnp.int32, (rows, nkeys), 0) % ds
    s_idx = lax.broadcasted_iota(jnp.int32, (rows, nkeys), 1)
    dist = win + t - s_idx
    valid = (dist >= 0) & (dist < WINDOW)
    g_of_row = lax.broadcasted_iota(jnp.int32, (rows, 1), 0) // ds
    for kv in range(N_KV_A):
        sk = jnp.zeros((rows, 1), F32)
        for g in range(G_A):
            sk = jnp.where(g_of_row == g, sink_ref[kv * G_A + g], sk)
        csl = slice(kv * HEAD_DIM, (kv + 1) * HEAD_DIM)
        for b in range(_SWA_SB):
            rsl = slice(b * ds, (b + 1) * ds)
            q = jnp.concatenate(
                [q_ref[rsl, (kv * G_A + g) * HEAD_DIM:(kv * G_A + g + 1) * HEAD_DIM] for g in range(G_A)],
                axis=0).astype(BF16)
            kk = jnp.concatenate([bk_ref[b, :, csl], kn_ref[rsl, csl]], axis=0).astype(BF16)
            vv = jnp.concatenate([bv_ref[b, :, csl], vn_ref[rsl, csl]], axis=0).astype(BF16)
            s = _dot_nt(q, kk) * SCALE + bias_ref[kv]
            s = jnp.where(valid, s, NEG)
            m = jnp.maximum(jnp.max(s, axis=-1, keepdims=True), sk)
            e = jnp.exp(s - m)
            p = e / (jnp.sum(e, axis=-1, keepdims=True) + jnp.exp(sk - m))
            o = _dot(p.astype(BF16), vv)
            for g in range(G_A):
                o_ref[rsl, (kv * G_A + g) * HEAD_DIM:(kv * G_A + g + 1) * HEAD_DIM] = o[g * ds:(g + 1) * ds]
    for b in range(_SWA_SB):
        rsl = slice(b * ds, (b + 1) * ds)
        wk_ref[b, :win - ds, :] = bk_ref[b, ds:, :]
        wk_ref[b, win - ds:, :] = kn_ref[rsl, :]
        wv_ref[b, :win - ds, :] = bv_ref[b, ds:, :]
        wv_ref[b, win - ds:, :] = vn_ref[rsl, :]


def swa_sample(proj, row0, cache_k, cache_v, rel_bias, sinks, db, ds):
    win = cache_k.shape[1]
    rb = _SWA_SB * ds
    r0 = row0 // rb
    bucket = _t5_bucket(win + jnp.arange(ds)[:, None] - jnp.arange(win + ds)[None, :]).astype(jnp.int32)
    smem = pl.BlockSpec(memory_space=pltpu.SMEM)
    return pl.pallas_call(
        functools.partial(_swa_sample_kernel, ds=ds, win=win),
        grid=(db // _SWA_SB,),
        in_specs=[
            pl.BlockSpec((rb, D_A), lambda i: (r0 + i, 0)),
            pl.BlockSpec((rb, KVD_A), lambda i: (r0 + i, E_KA // KVD_A)),
            pl.BlockSpec((rb, KVD_A), lambda i: (r0 + i, E_VA // KVD_A)),
            pl.BlockSpec((_SWA_SB, win, KVD_A), lambda i: (i, 0, 0)),
            pl.BlockSpec((_SWA_SB, win, KVD_A), lambda i: (i, 0, 0)),
            pl.BlockSpec((ds, win + ds), lambda i: (0, 0)),
            smem, smem,
        ],
        out_specs=[
            pl.BlockSpec((rb, D_A), lambda i: (i, 0)),
            pl.BlockSpec((_SWA_SB, win, KVD_A), lambda i: (i, 0, 0)),
            pl.BlockSpec((_SWA_SB, win, KVD_A), lambda i: (i, 0, 0)),
        ],
        out_shape=[jax.ShapeDtypeStruct((db * ds, D_A), F32),
                   jax.ShapeDtypeStruct(cache_k.shape, F32),
                   jax.ShapeDtypeStruct(cache_v.shape, F32)],
        scratch_shapes=[pltpu.VMEM((N_KV_A, G_A * ds, win + ds), F32)],
        compiler_params=_cparams(1),
        name="swa_sample",
    )(proj, proj, proj, cache_k, cache_v, bucket, rel_bias, sinks)


_HG_CHUNK = 128
_HG_LEVELS = int(math.log2(_HG_CHUNK))


def _hgrn_constants(c):
    levels = int(math.log2(c))
    t = np.arange(c)
    mats = [(t[None, :] <= t[:, None])]
    lvl = np.full((c, c), -1, np.int32)
    lvl[t, t] = levels
    for l in range(levels):
        m = c >> (l + 1)
        pivot = (t // (2 * m)) * (2 * m) + m
        mats.append(t[None, :] <= pivot[:, None])
        same = (t[:, None] // (2 * m)) == (t[None, :] // (2 * m))
        pair = same & ((t[:, None] % (2 * m)) >= m) & ((t[None, :] % (2 * m)) < m)
        lvl[pair] = l
    return np.concatenate(mats, axis=0).astype(np.float32), lvl


def _column_of(row):
    n = row.shape[1]
    r = lax.broadcasted_iota(jnp.int32, (n, n), 0)
    c = lax.broadcasted_iota(jnp.int32, (n, n), 1)
    return jnp.sum(jnp.where(r == c, jnp.broadcast_to(row, (n, n)), 0.0), axis=1, keepdims=True)


def _hgrn_prompt_kernel(q_ref, f_ref, v_ref, cm_ref, lvl_ref, o_ref, s_out_ref, s_ref):
    c = _HG_CHUNK
    ci = pl.program_id(2)

    @pl.when(ci == 0)
    def _():
        s_ref[...] = jnp.zeros_like(s_ref)

    lvl = lvl_ref[...]
    sums_all = _dot_exact01(cm_ref[...], jnp.log(f_ref[...]))
    for h in range(_HG_HP):
        hs = slice(h * HEAD_DIM, (h + 1) * HEAD_DIM)
        q = q_ref[:, hs]
        v = v_ref[:, hs].astype(BF16)
        k = 1.0 - f_ref[:, hs]
        sums = sums_all[:, hs]
        b = sums[:c]
        a = jnp.where(lvl == _HG_LEVELS, _dot_nt(q.astype(BF16), k.astype(BF16)), 0.0)
        for l in range(_HG_LEVELS):
            e = jnp.exp(-jnp.abs(b - sums[(l + 1) * c:(l + 2) * c]))
            a = a + jnp.where(lvl == l, _dot_nt((q * e).astype(BF16), (k * e).astype(BF16)), 0.0)
        s0 = s_ref[h]
        o_ref[:, hs] = _dot((q * jnp.exp(b)).astype(BF16), s0.astype(BF16)) + _dot(a.astype(BF16), v)
        b_last = b[c - 1:c, :]
        khat = (k * jnp.exp(b_last - b)).astype(BF16)
        s_new = _column_of(jnp.exp(b_last)) * s0 + _dot_tn(khat, v)
        s_ref[h] = s_new

        @pl.when(ci == pl.num_programs(2) - 1)
        def _():
            s_out_ref[0, h] = s_new


_HG_HP = 4


def hgrn_prompt(proj, batch, seq):
    c = _HG_CHUNK
    nc = seq // c
    hp = _HG_HP
    cmat, lvl = _hgrn_constants(c)
    w = hp * HEAD_DIM
    qc, fc, vc = E_QB // w, E_F // w, E_VB // w
    return pl.pallas_call(
        _hgrn_prompt_kernel,
        grid=(batch, N_HEADS_B // hp, nc),
        in_specs=[
            pl.BlockSpec((c, w), lambda b, h, i: (b * nc + i, qc + h)),
            pl.BlockSpec((c, w), lambda b, h, i: (b * nc + i, fc + h)),
            pl.BlockSpec((c, w), lambda b, h, i: (b * nc + i, vc + h)),
            pl.BlockSpec(cmat.shape, lambda b, h, i: (0, 0)),
            pl.BlockSpec(lvl.shape, lambda b, h, i: (0, 0)),
        ],
        out_specs=[
            pl.BlockSpec((c, w), lambda b, h, i: (b * nc + i, h)),
            pl.BlockSpec((1, hp, HEAD_DIM, HEAD_DIM), lambda b, h, i: (b, h, 0, 0)),
        ],
        out_shape=[jax.ShapeDtypeStruct((batch * seq, D_B), F32),
                   jax.ShapeDtypeStruct((batch, N_HEADS_B, HEAD_DIM, HEAD_DIM), F32)],
        scratch_shapes=[pltpu.VMEM((hp, HEAD_DIM, HEAD_DIM), F32)],
        compiler_params=_cparams(3),
        name="hgrn_prompt",
    )(proj, proj, proj, jnp.asarray(cmat), jnp.asarray(lvl))


_HG_SB = 8


def _hgrn_sample_kernel(q_ref, f_ref, v_ref, st_ref, o_ref, s_out_ref, *, ds):
    s_idx = lax.broadcasted_iota(jnp.int32, (ds, 1), 0)
    for sb in range(_HG_SB):
        rsl = slice(sb * ds, (sb + 1) * ds)
        q = q_ref[rsl, :]
        f = f_ref[rsl, :]
        v = v_ref[rsl, :]
        k = 1.0 - f
        lf = jnp.log(f)
        rows = [lf[0:1]]
        for t in range(1, ds):
            rows.append(rows[-1] + lf[t:t + 1])
        b = jnp.concatenate(rows, axis=0)
        s0 = st_ref[sb, 0]
        o_inter = _dot((q * jnp.exp(b)).astype(BF16), s0.astype(BF16))
        outs = []
        for t in range(ds):
            w = q[t:t + 1] * k * jnp.exp(jnp.minimum(b[t:t + 1] - b, 0.0))
            a_t = jnp.where(s_idx <= t, jnp.sum(w, axis=-1, keepdims=True), 0.0)
            outs.append(jnp.sum(a_t * v, axis=0, keepdims=True))
        o_ref[rsl, :] = o_inter + jnp.concatenate(outs, axis=0)
        b_last = b[ds - 1:ds, :]
        khat = (k * jnp.exp(b_last - b)).astype(BF16)
        s_out_ref[sb, 0] = _column_of(jnp.exp(b_last)) * s0 + _dot_tn(khat, v.astype(BF16))


def hgrn_sample(proj, row0, state, db, ds):
    rb = _HG_SB * ds
    r0 = row0 // rb
    qc, fc, vc = E_QB // HEAD_DIM, E_F // HEAD_DIM, E_VB // HEAD_DIM
    return pl.pallas_call(
        functools.partial(_hgrn_sample_kernel, ds=ds),
        grid=(db // _HG_SB, N_HEADS_B),
        in_specs=[
            pl.BlockSpec((rb, HEAD_DIM), lambda i, h: (r0 + i, qc + h)),
            pl.BlockSpec((rb, HEAD_DIM), lambda i, h: (r0 + i, fc + h)),
            pl.BlockSpec((rb, HEAD_DIM), lambda i, h: (r0 + i, vc + h)),
            pl.BlockSpec((_HG_SB, 1, HEAD_DIM, HEAD_DIM), lambda i, h: (i, h, 0, 0)),
        ],
        out_specs=[
            pl.BlockSpec((rb, HEAD_DIM), lambda i, h: (i, h)),
            pl.BlockSpec((_HG_SB, 1, HEAD_DIM, HEAD_DIM), lambda i, h: (i, h, 0, 0)),
        ],
        out_shape=[jax.ShapeDtypeStruct((db * ds, D_B), F32),
                   jax.ShapeDtypeStruct(state.shape, F32)],
        compiler_params=_cparams(2),
        name="hgrn_sample",
    )(proj, proj, proj, state)


_CS_BLK = 128


def _split3(x):
    x1 = x.astype(BF16).astype(F32)
    r = x - x1
    x2 = r.astype(BF16).astype(F32)
    return x1, x2, r - x2


_AUG_ONES = 3 * N_HEADS_C
_INV_SCALE = HEAD_DIM ** 0.5
_LOG2E = 1.4426950408889634


def _cumsum_kernel(lf_ref, tri_ref, place_ref, c_ref, aug_ref):
    nblk = lf_ref.shape[1] // _CS_BLK
    lane = lax.broadcasted_iota(jnp.int32, (_CS_BLK, HEAD_DIM), 1)
    ones = jnp.where((lane >= _AUG_ONES) & (lane < _AUG_ONES + 3), 1.0, 0.0)

    def body(i, carry):
        r = pl.ds(pl.multiple_of(i * _CS_BLK, _CS_BLK), _CS_BLK)
        c = _dot_exact01(tri_ref[...], lf_ref[0, r, :]) + carry
        c_ref[0, r, :] = c
        aug = ones
        for j, piece in enumerate(_split3(c * _INV_SCALE)):
            aug = aug + _dot(piece.astype(BF16), place_ref[j].astype(BF16))
        aug_ref[0, r, :] = aug.astype(BF16)
        return c[_CS_BLK - 1:_CS_BLK, :]

    lax.fori_loop(0, nblk, body, jnp.zeros((1, lf_ref.shape[2]), F32))


def cumsum_rows(lf):
    b, s, h = lf.shape
    t = np.arange(_CS_BLK)
    tri = jnp.asarray((t[None, :] <= t[:, None]).astype(np.float32))
    place = np.zeros((3, h, HEAD_DIM), np.float32)
    for j in range(3):
        place[j, np.arange(h), j * h + np.arange(h)] = 1.0
    return pl.pallas_call(
        _cumsum_kernel,
        grid=(b,),
        in_specs=[pl.BlockSpec((1, s, h), lambda i: (i, 0, 0)),
                  pl.BlockSpec((_CS_BLK, _CS_BLK), lambda i: (0, 0)),
                  pl.BlockSpec((3, h, HEAD_DIM), lambda i: (0, 0, 0))],
        out_specs=[pl.BlockSpec((1, s, h), lambda i: (i, 0, 0)),
                   pl.BlockSpec((1, s, HEAD_DIM), lambda i: (i, 0, 0))],
        out_shape=[jax.ShapeDtypeStruct(lf.shape, F32),
                   jax.ShapeDtypeStruct((b, s, HEAD_DIM), BF16)],
        compiler_params=_cparams(1),
        name="fox_cumsum",
    )(lf, tri, jnp.asarray(place))


_FOX_TQ = 512
_FOX_TK = 512


def _fox_prompt_kernel(q_ref, k_ref, vt_ref, cq_ref, ca_ref, o_ref, qa_ref, m_ref, l_ref, acc_ref):
    kv = pl.program_id(1)
    qi = pl.program_id(2)
    ki = pl.program_id(3)
    tq, tk = q_ref.shape[0], k_ref.shape[0]

    @pl.when(ki == 0)
    def _():
        m_ref[...] = jnp.full_like(m_ref, NEG)
        l_ref[...] = jnp.zeros_like(l_ref)
        acc_ref[...] = jnp.zeros_like(acc_ref)
        lane = lax.broadcasted_iota(jnp.int32, (tq, HEAD_DIM), 1)
        for g in range(G_C):
            h = kv * G_C + g
            own = (lane == h) | (lane == N_HEADS_C + h) | (lane == 2 * N_HEADS_C + h)
            extra = jnp.where(own, -1.0, 0.0)
            for j, piece in enumerate(_split3(cq_ref[0, 0, :, g:g + 1] * _INV_SCALE)):
                extra = jnp.where(lane == _AUG_ONES + j, piece, extra)
            qa_ref[g, :, :HEAD_DIM] = q_ref[:, g * HEAD_DIM:(g + 1) * HEAD_DIM].astype(BF16)
            qa_ref[g, :, HEAD_DIM:] = extra.astype(BF16)

    def update(masked):
        kaug = jnp.concatenate([k_ref[...].astype(BF16), ca_ref[0]], axis=1)
        vt = vt_ref[0, 0].astype(BF16)
        if masked:
            causal = (lax.broadcasted_iota(jnp.int32, (tk, tq), 0)
                      <= lax.broadcasted_iota(jnp.int32, (tk, tq), 1))
        for g in range(G_C):
            s = _dot_nt(kaug, qa_ref[g]) * (SCALE * _LOG2E)
            if masked:
                s = jnp.where(causal, s, NEG)
            m_prev = m_ref[g]
            m_new = jnp.maximum(m_prev, jnp.max(s, axis=0, keepdims=True))
            alpha = jnp.exp2(m_prev - m_new)
            p = jnp.exp2(s - m_new)
            l_ref[g] = alpha * l_ref[g] + jnp.sum(p, axis=0, keepdims=True)
            acc_ref[g] = alpha * acc_ref[g] + _dot(vt, p.astype(BF16))
            m_ref[g] = m_new

    @pl.when(ki < qi)
    def _():
        update(False)

    @pl.when(ki == qi)
    def _():
        update(True)

    @pl.when(ki == pl.num_programs(3) - 1)
    def _():
        for g in range(G_C):
            o_ref[:, g * HEAD_DIM:(g + 1) * HEAD_DIM] = jnp.transpose(acc_ref[g] / l_ref[g])


def fox_prompt(projo, c, c_aug, batch, seq):
    tq = tk = min(_FOX_TQ, seq)
    nq, nk = seq // tq, seq // tk
    c_q = jnp.transpose(c.reshape(batch, seq, N_KV_C, G_C), (0, 2, 1, 3))
    v_t = jnp.transpose(projo[:batch * seq, O_V:O_V + KVD_C].reshape(batch, seq, N_KV_C, HEAD_DIM),
                        (0, 2, 3, 1))
    kcol = O_K // HEAD_DIM
    return pl.pallas_call(
        _fox_prompt_kernel,
        grid=(batch, N_KV_C, nq, nk),
        in_specs=[
            pl.BlockSpec((tq, G_C * HEAD_DIM), lambda b, k, i, j: (b * nq + i, k)),
            pl.BlockSpec((tk, HEAD_DIM), lambda b, k, i, j: (b * nk + jnp.minimum(j, i), kcol + k)),
            pl.BlockSpec((1, 1, HEAD_DIM, tk), lambda b, k, i, j: (b, k, 0, jnp.minimum(j, i))),
            pl.BlockSpec((1, 1, tq, G_C), lambda b, k, i, j: (b, k, i, 0)),
            pl.BlockSpec((1, tk, HEAD_DIM), lambda b, k, i, j: (b, jnp.minimum(j, i), 0)),
        ],
        out_specs=pl.BlockSpec((tq, G_C * HEAD_DIM), lambda b, k, i, j: (b * nq + i, k)),
        out_shape=jax.ShapeDtypeStruct((batch * seq, D_C), F32),
        scratch_shapes=[pltpu.VMEM((G_C, tq, 2 * HEAD_DIM), BF16),
                        pltpu.VMEM((G_C, 1, tq), F32), pltpu.VMEM((G_C, 1, tq), F32),
                        pltpu.VMEM((G_C, HEAD_DIM, tq), F32)],
        compiler_params=_cparams(4),
        name="fox_prompt",
    )(projo, projo, v_t, c_q, c_aug)


_FOX_PP = 16
_FOX_NSEQ = 1


def _fox_sample_kernel(pt_ref, q_ref, kn_ref, vn_ref, lfn_ref, tri_ref, ut1_ref, *refs, ds, page):
    del pt_ref
    pp, nseq = _FOX_PP, _FOX_NSEQ
    n_pg = nseq * pp
    k_refs, v_refs, lf_refs = refs[:n_pg], refs[n_pg:2 * n_pg], refs[2 * n_pg:3 * n_pg]
    o_ref, qb_ref, cq_ref, m_ref, l_ref, acc_ref, carry_ref = refs[3 * n_pg:]
    nrow = N_HEADS_C * ds
    rows_kv = G_C * ds
    step = pl.program_id(1)

    @pl.when(step == 0)
    def _():
        causal = (lax.broadcasted_iota(jnp.int32, (rows_kv, ds), 1)
                  <= lax.broadcasted_iota(jnp.int32, (rows_kv, ds), 0) % ds)
        t_row = lax.broadcasted_iota(jnp.int32, (nrow, ds), 0) % ds
        s_col = lax.broadcasted_iota(jnp.int32, (nrow, ds), 1)
        for j in range(nseq):
            qb_ref[j] = q_ref[j].astype(BF16)
            cn = _dot_exact01_r(lfn_ref[j], tri_ref[...])
            cq_ref[j] = jnp.sum(jnp.where(s_col == t_row, cn, 0.0), axis=-1, keepdims=True)
            tok = slice(j * ds, (j + 1) * ds)
            for kv in range(N_KV_C):
                rsl = slice(kv * rows_kv, (kv + 1) * rows_kv)
                csl = slice(kv * HEAD_DIM, (kv + 1) * HEAD_DIM)
                s = (_dot_nt(qb_ref[j, rsl, :], kn_ref[tok, csl].astype(BF16)) * SCALE
                     + cq_ref[j, rsl, :] - cn[rsl])
                s = jnp.where(causal, s, NEG)
                m = jnp.max(s, axis=-1, keepdims=True)
                e = jnp.exp(s - m)
                m_ref[j, rsl, :] = m
                l_ref[j, rsl, :] = jnp.sum(e, axis=-1, keepdims=True)
                acc_ref[j, rsl, :] = _dot(e.astype(BF16), vn_ref[tok, csl].astype(BF16))
        carry_ref[...] = jnp.zeros_like(carry_ref)

    for j in range(nseq):
        kj, vj, lfj = (r[j * pp:(j + 1) * pp] for r in (k_refs, v_refs, lf_refs))
        lf_stack = jnp.concatenate([lfj[i][0] for i in range(pp)], axis=0)
        sums = _dot_exact01_r(lf_stack, ut1_ref[...])
        carry = carry_ref[j]
        dex = []
        for i in range(pp):
            hs = slice(i * N_HEADS_C, (i + 1) * N_HEADS_C)
            dex.append(sums[hs, :page] + carry)
            carry = carry + sums[hs, page:]
        carry_ref[j] = carry

        def kv_rows(refs_, kv):
            strided = pl.ds(kv, page, stride=N_KV_C)
            return jnp.concatenate([refs_[i][0, strided, :] for i in range(pp)], axis=0).astype(BF16)

        qk = jnp.concatenate(
            [_dot_nt(qb_ref[j, kv * rows_kv:(kv + 1) * rows_kv, :], kv_rows(kj, kv))
             for kv in range(N_KV_C)], axis=0)
        bias = jnp.concatenate(
            [jnp.broadcast_to(d[:, None, :], (N_HEADS_C, ds, page)).reshape(nrow, page) for d in dex], axis=1)
        s = qk * SCALE + cq_ref[j] + bias
        m_prev = m_ref[j]
        m_new = jnp.maximum(m_prev, jnp.max(s, axis=-1, keepdims=True))
        alpha = jnp.exp(m_prev - m_new)
        p = jnp.exp(s - m_new)
        l_ref[j] = alpha * l_ref[j] + jnp.sum(p, axis=-1, keepdims=True)
        m_ref[j] = m_new
        pb = p.astype(BF16)
        pv = jnp.concatenate(
            [_dot(pb[kv * rows_kv:(kv + 1) * rows_kv, :], kv_rows(vj, kv)) for kv in range(N_KV_C)], axis=0)
        acc_ref[j] = alpha * acc_ref[j] + pv

    @pl.when(step == pl.num_programs(1) - 1)
    def _():
        for j in range(nseq):
            o_ref[j] = acc_ref[j] / l_ref[j]


def fox_sample(projo, row0, lf_s, cache_k, cache_v, cache_lf, page_table, db, ds):
    pool, page = cache_lf.shape[0], cache_lf.shape[1]
    n_pages = page_table.shape[1]
    pp = _FOX_PP
    nrow = N_HEADS_C * ds
    q = projo[row0:row0 + db * ds, O_Q:O_Q + D_C].reshape(db, ds, N_HEADS_C, HEAD_DIM)
    q = jnp.transpose(q, (0, 2, 1, 3)).reshape(db, nrow, HEAD_DIM)
    lfn = jnp.transpose(lf_s.reshape(db, ds, N_HEADS_C), (0, 2, 1))
    lfn = jnp.broadcast_to(lfn[:, :, None, :], (db, N_HEADS_C, ds, ds)).reshape(db, nrow, ds)
    cache_lft = jnp.transpose(cache_lf, (0, 2, 1))
    t = np.arange(ds)
    tri = jnp.asarray((t[:, None] <= t[None, :]).astype(np.float32))
    u = np.arange(page)
    ut1 = jnp.asarray(np.concatenate([(u[:, None] > u[None, :]), np.ones((page, page), bool)],
                                     axis=1).astype(np.float32))
    nseq = _FOX_NSEQ
    r0 = row0 // (nseq * ds)

    def page_map(j, i):
        return lambda b, p, pt: (pt[b * nseq + j, n_pages - 1 - (p * pp + i)], 0, 0)

    pages = [(j, i) for j in range(nseq) for i in range(pp)]
    in_specs = [
        pl.BlockSpec((nseq, nrow, HEAD_DIM), lambda b, p, pt: (b, 0, 0)),
        pl.BlockSpec((nseq * ds, KVD_C), lambda b, p, pt: (r0 + b, O_K // KVD_C)),
        pl.BlockSpec((nseq * ds, KVD_C), lambda b, p, pt: (r0 + b, O_V // KVD_C)),
        pl.BlockSpec((nseq, nrow, ds), lambda b, p, pt: (b, 0, 0)),
        pl.BlockSpec((ds, ds), lambda b, p, pt: (0, 0)),
        pl.BlockSpec((page, 2 * page), lambda b, p, pt: (0, 0)),
    ]
    in_specs += [pl.BlockSpec((1, page * N_KV_C, HEAD_DIM), page_map(j, i)) for j, i in pages]
    in_specs += [pl.BlockSpec((1, page * N_KV_C, HEAD_DIM), page_map(j, i)) for j, i in pages]
    in_specs += [pl.BlockSpec((1, N_HEADS_C, page), page_map(j, i)) for j, i in pages]
    out = pl.pallas_call(
        functools.partial(_fox_sample_kernel, ds=ds, page=page),
        grid_spec=pltpu.PrefetchScalarGridSpec(
            num_scalar_prefetch=1,
            grid=(db // nseq, n_pages // pp),
            in_specs=in_specs,
            out_specs=pl.BlockSpec((nseq, nrow, HEAD_DIM), lambda b, p, pt: (b, 0, 0)),
            scratch_shapes=[pltpu.VMEM((nseq, nrow, HEAD_DIM), BF16), pltpu.VMEM((nseq, nrow, 1), F32),
                            pltpu.VMEM((nseq, nrow, 1), F32), pltpu.VMEM((nseq, nrow, 1), F32),
                            pltpu.VMEM((nseq, nrow, HEAD_DIM), F32),
                            pltpu.VMEM((nseq, N_HEADS_C, page), F32)],
        ),
        out_shape=jax.ShapeDtypeStruct((db, nrow, HEAD_DIM), F32),
        compiler_params=_cparams(2),
        name="fox_sample",
    )(page_table, q, projo, projo, lfn, tri, ut1, *([cache_k] * len(pages)), *([cache_v] * len(pages)),
      *([cache_lft] * len(pages)))
    out = jnp.transpose(out.reshape(db, N_HEADS_C, ds, HEAD_DIM), (0, 2, 1, 3))
    return out.reshape(db * ds, D_C)


def _moe_route_kernel(x_ref, g_ref, r_ref, idx_ref, w_ref):
    hn = _rms(x_ref[...], g_ref[...])
    logits = jnp.dot(hn, r_ref[...], preferred_element_type=F32, precision=lax.Precision.HIGHEST)
    lane = lax.broadcasted_iota(jnp.int32, logits.shape, 1)
    m1 = jnp.max(logits, axis=-1, keepdims=True)
    i1 = jnp.min(jnp.where(logits == m1, lane, N_EXPERTS), axis=-1, keepdims=True)
    rest = jnp.where(lane == i1, -jnp.inf, logits)
    m2 = jnp.max(rest, axis=-1, keepdims=True)
    i2 = jnp.min(jnp.where(rest == m2, lane, N_EXPERTS), axis=-1, keepdims=True)
    e = jnp.exp(m2 - m1)
    slot = lax.broadcasted_iota(jnp.int32, idx_ref.shape, 1)
    idx_ref[...] = jnp.where(slot == 0, i1, i2)
    w_ref[...] = jnp.where(slot == 0, 1.0 / (1.0 + e), e / (1.0 + e))


def moe_route(x, g, router, tm):
    n, d = x.shape
    return pl.pallas_call(
        _moe_route_kernel,
        grid=(n // tm,),
        in_specs=[pl.BlockSpec((tm, d), lambda i: (i, 0)),
                  pl.BlockSpec((1, d), lambda i: (0, 0)),
                  pl.BlockSpec((d, N_EXPERTS), lambda i: (0, 0))],
        out_specs=[pl.BlockSpec((tm, TOP_K), lambda i: (i, 0)),
                   pl.BlockSpec((tm, TOP_K), lambda i: (i, 0))],
        out_shape=[jax.ShapeDtypeStruct((n, TOP_K), jnp.int32),
                   jax.ShapeDtypeStruct((n, TOP_K), F32)],
        compiler_params=_cparams(1),
        name="moe_route",
    )(x, g.reshape(1, d), router)


def _route_plan(idx, w, tm):
    n = idx.shape[0]
    flat_e = idx.reshape(-1)
    onehot = (flat_e[:, None] == jnp.arange(N_EXPERTS, dtype=jnp.int32)[None, :]).astype(jnp.int32)
    csum = jnp.cumsum(onehot, axis=0)
    rank = jnp.take_along_axis(csum, flat_e[:, None], axis=1)[:, 0] - 1
    count = csum[-1]
    tiles_e = (count + tm - 1) // tm
    tile_end = jnp.cumsum(tiles_e)
    tile_start = tile_end - tiles_e
    pos = tile_start[flat_e] * tm + rank
    n_tiles = (TOP_K * n) // tm + N_EXPERTS
    token = jnp.zeros((n_tiles * tm,), jnp.int32).at[pos].set(jnp.arange(TOP_K * n, dtype=jnp.int32) // TOP_K)
    gate = jnp.zeros((n_tiles * tm,), F32).at[pos].set(w.reshape(-1))
    tile = jnp.arange(n_tiles, dtype=jnp.int32)
    tile_expert = jnp.minimum(jnp.searchsorted(tile_end, tile, side="right"), N_EXPERTS - 1).astype(jnp.int32)
    tile_valid = jnp.clip(count[tile_expert] - (tile - tile_start[tile_expert]) * tm, 0, tm)
    tile_valid = jnp.where(tile < tile_end[-1], tile_valid, 0).astype(jnp.int32)
    return pos.reshape(n, TOP_K), token, gate.reshape(-1, 1), tile_expert, tile_valid


_MOE_CHUNKS = 2


def moe_layer(x, g, router, w_gate, w_up, w_down, tm_route, n_split):
    tm = min(_FFN_TM, x.shape[0])
    idx, w = moe_route(x, g, router, tm_route)
    pos, token, gate, tile_expert, tile_valid = _route_plan(idx, w, tm)
    tiles = token.shape[0] // tm
    n_chunks = _MOE_CHUNKS if tiles % _MOE_CHUNKS == 0 else 1
    tiles_c = tiles // n_chunks
    act = None
    for c in range(n_chunks):
        ts = slice(c * tiles_c, (c + 1) * tiles_c)
        x_sorted = norm_cast(jnp.take(x, token[c * tiles_c * tm:(c + 1) * tiles_c * tm], axis=0, mode="clip"),
                             g, tm)
        act = ffn_up(x_sorted, tile_expert[ts], tile_valid[ts], w_gate, w_up,
                     out=act, tile0=c * tiles_c, total_tiles=tiles)
    y_sorted = ffn_down(act, tile_expert, tile_valid, w_down, gate=gate)
    y0 = jnp.take(y_sorted, pos[:, 0], axis=0, mode="clip")
    y1 = jnp.take(y_sorted, pos[:, 1], axis=0, mode="clip")
    return tuple(x[r] + y0[r] + y1[r] for r in (slice(0, n_split), slice(n_split, None)))


def _row_tile(n, want):
    t = min(want, n)
    while n % t:
        t //= 2
    return t


def kernel(x_prompt, x_sample, cache_k_win, cache_v_win, state_hgrn, cache_k_fox, cache_v_fox, cache_logf_fox, page_table, rel_bias, norm_mix_e, w_in_e, q_norm_a, k_norm_a, sinks_a, lb_b, o_norm_b, w_out_e, norm_ffn_e, w_gate_e, w_up_e, w_down_e, norm_mix_o, w_in_o, fgate_bias_c, q_norm_c, k_norm_c, w_out_o, norm_ffn_o, router_o, w_gate_x, w_up_x, w_down_x):
    batch, seq, d = x_prompt.shape
    db, ds, _ = x_sample.shape
    n_p, n_s = batch * seq, db * ds
    n = n_p + n_s
    win = cache_k_win.shape[2]
    pool, page = cache_k_fox.shape[1], cache_k_fox.shape[2]
    tm_big = _row_tile(math.gcd(n_p, n_s), 1024)

    x = jnp.concatenate([x_prompt.reshape(n_p, d), x_sample.reshape(n_s, d)], axis=0)

    proj = even_project(x, norm_mix_e[0], w_in_e[0], q_norm_a[0], k_norm_a[0], lb_b, tm_big)
    oa_p = swa_prompt(proj, rel_bias, sinks_a[0], batch, seq)
    oa_s, wk_s, wv_s = swa_sample(proj, n_p, cache_k_win[0].reshape(db, win, KVD_A),
                                  cache_v_win[0].reshape(db, win, KVD_A), rel_bias, sinks_a[0], db, ds)
    ob_p, hg_p = hgrn_prompt(proj, batch, seq)
    ob_s, hg_s = hgrn_sample(proj, n_p, state_hgrn[0], db, ds)
    y = even_merge(oa_p, oa_s, ob_p, ob_s, proj, o_norm_b[0], w_out_e[0], x, tm_big // 2)
    y = dense_ffn(y, norm_ffn_e[0], w_gate_e[0], w_up_e[0], w_down_e[0])

    projo, lf = odd_project(y, norm_mix_o[0], w_in_o[0], q_norm_c[0], k_norm_c[0], fgate_bias_c[0], tm_big)
    c, c_aug = cumsum_rows(lf[:n_p].reshape(batch, seq, N_HEADS_C))
    oc_p = fox_prompt(projo, c, c_aug, batch, seq)
    oc_s = fox_sample(projo, n_p, lf[n_p:], cache_k_fox[0].reshape(pool, page * N_KV_C, HEAD_DIM),
                      cache_v_fox[0].reshape(pool, page * N_KV_C, HEAD_DIM), cache_logf_fox[0],
                      page_table, db, ds)
    y = out_project(oc_p, oc_s, w_out_o[0], y, tm_big)
    y_p, y_s = moe_layer(y, norm_ffn_o[0], router_o[0], w_gate_x[0], w_up_x[0], w_down_x[0], tm_big, n_p)

    def prompt_tail(col, width, heads):
        t = proj[:n_p, col:col + width].reshape(batch, seq, heads, HEAD_DIM)
        return t[:, seq - WINDOW:][None]

    return (
        y_p.reshape(batch, seq, d),
        y_s.reshape(db, ds, d),
        prompt_tail(E_KA, KVD_A, N_KV_A),
        prompt_tail(E_VA, KVD_A, N_KV_A),
        wk_s.reshape(1, db, win, N_KV_A, HEAD_DIM),
        wv_s.reshape(1, db, win, N_KV_A, HEAD_DIM),
        hg_p[None],
        hg_s[None],
        projo[:n_p, O_K:O_K + KVD_C].reshape(1, batch, seq, N_KV_C, HEAD_DIM),
        projo[:n_p, O_V:O_V + KVD_C].reshape(1, batch, seq, N_KV_C, HEAD_DIM),
        lf[:n_p].reshape(1, batch, seq, N_HEADS_C),
        projo[n_p:, O_K:O_K + KVD_C].reshape(1, db, ds, N_KV_C, HEAD_DIM),
        projo[n_p:, O_V:O_V + KVD_C].reshape(1, db, ds, N_KV_C, HEAD_DIM),
        lf[n_p:].reshape(1, db, ds, N_HEADS_C),
    )
```

```python
import functools
import math

import numpy as np
import jax
import jax.numpy as jnp
from jax import lax
from jax.experimental import pallas as pl
from jax.experimental.pallas import tpu as pltpu

F32 = jnp.float32
BF16 = jnp.bfloat16

HEAD_DIM = 128
N_HEADS_A, N_KV_A = 8, 2
G_A = N_HEADS_A // N_KV_A
WINDOW = 128
REL_BUCKETS, REL_MAX_DIST = 32, 128
N_HEADS_B = 8
N_HEADS_C, N_KV_C = 16, 4
G_C = N_HEADS_C // N_KV_C
N_EXPERTS, TOP_K = 8, 2
EPS = 1e-6
SCALE = HEAD_DIM ** -0.5
NEG = -1e30

D_A = N_HEADS_A * HEAD_DIM
KVD_A = N_KV_A * HEAD_DIM
D_B = N_HEADS_B * HEAD_DIM
D_C = N_HEADS_C * HEAD_DIM
KVD_C = N_KV_C * HEAD_DIM

E_QA, E_QB, E_F, E_VB, E_GB = 0, D_A, D_A + D_B, D_A + 2 * D_B, D_A + 3 * D_B
E_KA = D_A + 4 * D_B
E_VA = E_KA + KVD_A
E_OUT = E_VA + KVD_A
O_Q, O_K, O_V = 0, D_C, D_C + KVD_C
O_OUT = D_C + 2 * KVD_C

VMEM_LIMIT = 56 * 1024 * 1024


def _cparams(n_axes):
    return pltpu.CompilerParams(dimension_semantics=("arbitrary",) * n_axes,
                                vmem_limit_bytes=VMEM_LIMIT)


def _rms(x, gain):
    return x * lax.rsqrt(jnp.mean(x * x, axis=-1, keepdims=True) + EPS) * gain


def _silu(x):
    return x * (1.0 / (1.0 + jnp.exp(-x)))


def _dot(a, b):
    return jnp.dot(a, b, preferred_element_type=F32)


def _dot_nt(a, b):
    return lax.dot_general(a, b, (((1,), (1,)), ((), ())), preferred_element_type=F32)


def _dot_tn(a, b):
    return lax.dot_general(a, b, (((0,), (0,)), ((), ())), preferred_element_type=F32)


def _dot_exact01(m01, x):
    x1 = x.astype(BF16)
    r1 = x - x1.astype(F32)
    x2 = r1.astype(BF16)
    x3 = (r1 - x2.astype(F32)).astype(BF16)
    m = m01.astype(BF16)
    return _dot(m, x1) + _dot(m, x2) + _dot(m, x3)


def _dot_exact01_r(x, m01):
    x1 = x.astype(BF16)
    r1 = x - x1.astype(F32)
    x2 = r1.astype(BF16)
    x3 = (r1 - x2.astype(F32)).astype(BF16)
    m = m01.astype(BF16)
    return _dot(x1, m) + _dot(x2, m) + _dot(x3, m)


_TN = 256


def _even_out_tile(j):
    return jnp.where(j < 4, j, jnp.where(j < 6, j + 16, j - 2))


def _even_proj_kernel(x_ref, g_ref, w_ref, qn_ref, kn_ref, lb_ref, o_ref, hn_ref):
    j = pl.program_id(1)

    @pl.when(j == 0)
    def _():
        hn_ref[...] = _rms(x_ref[...], g_ref[...]).astype(BF16)

    acc = _dot(hn_ref[...], w_ref[...].astype(BF16))

    def head_norm(gain):
        for h in range(_TN // HEAD_DIM):
            sl = slice(h * HEAD_DIM, (h + 1) * HEAD_DIM)
            o_ref[:, sl] = _rms(acc[:, sl], gain)

    @pl.when(j < 4)
    def _():
        head_norm(qn_ref[...])

    @pl.when(j == 4)
    def _():
        head_norm(kn_ref[...])

    @pl.when((j == 5) | (j >= 14))
    def _():
        o_ref[...] = acc

    @pl.when((j >= 6) & (j < 10))
    def _():
        o_ref[...] = _silu(acc)

    @pl.when((j >= 10) & (j < 14))
    def _():
        lb = lb_ref[...]
        e = jnp.exp(lb - jnp.max(lb, axis=0, keepdims=True))
        lb0 = e[0:1, :] / jnp.sum(e, axis=0, keepdims=True)
        o_ref[...] = lb0 + (1.0 - lb0) * (1.0 / (1.0 + jnp.exp(-acc)))


def even_project(x, g, w_in, q_norm, k_norm, lb_b, tm):
    n, d = x.shape
    n_tiles = w_in.shape[1] // _TN
    return pl.pallas_call(
        _even_proj_kernel,
        grid=(n // tm, n_tiles),
        in_specs=[
            pl.BlockSpec((tm, d), lambda i, j: (i, 0)),
            pl.BlockSpec((1, d), lambda i, j: (0, 0)),
            pl.BlockSpec((d, _TN), lambda i, j: (0, j)),
            pl.BlockSpec((1, HEAD_DIM), lambda i, j: (0, 0)),
            pl.BlockSpec((1, HEAD_DIM), lambda i, j: (0, 0)),
            pl.BlockSpec((lb_b.shape[0], _TN), lambda i, j: (0, jnp.clip(j - 10, 0, 3))),
        ],
        out_specs=pl.BlockSpec((tm, _TN), lambda i, j: (i, _even_out_tile(j))),
        out_shape=jax.ShapeDtypeStruct((n, E_OUT), F32),
        scratch_shapes=[pltpu.VMEM((tm, d), BF16)],
        compiler_params=_cparams(2),
        name="even_project",
    )(x, g.reshape(1, d), w_in, q_norm.reshape(1, -1), k_norm.reshape(1, -1), lb_b)


def _split_rows_specs(shape_cols, tm, tiles_p):
    return [pl.BlockSpec((tm, shape_cols), lambda i, j: (jnp.minimum(i, tiles_p - 1), 0)),
            pl.BlockSpec((tm, shape_cols), lambda i, j: (jnp.maximum(i - tiles_p, 0), 0))]


def _even_merge_kernel(oap_ref, oas_ref, obp_ref, obs_ref, gb_ref, on_ref, w_ref, x_ref, o_ref, lhs_ref,
                       *, tiles_p):
    i = pl.program_id(0)

    def fill(oa_ref, ob_ref):
        lhs_ref[:, :D_A] = oa_ref[...].astype(BF16)
        for h in range(N_HEADS_B):
            sl = slice(h * HEAD_DIM, (h + 1) * HEAD_DIM)
            obn = _rms(ob_ref[:, sl], on_ref[...]) * _silu(gb_ref[:, sl])
            lhs_ref[:, D_A + h * HEAD_DIM:D_A + (h + 1) * HEAD_DIM] = obn.astype(BF16)

    @pl.when((pl.program_id(1) == 0) & (i < tiles_p))
    def _():
        fill(oap_ref, obp_ref)

    @pl.when((pl.program_id(1) == 0) & (i >= tiles_p))
    def _():
        fill(oas_ref, obs_ref)

    o_ref[...] = x_ref[...] + _dot(lhs_ref[...], w_ref[...].astype(BF16))


def even_merge(oa_p, oa_s, ob_p, ob_s, proj, o_norm, w_out, x, tm, tn=512):
    n, d = x.shape
    tiles_p = oa_p.shape[0] // tm
    return pl.pallas_call(
        functools.partial(_even_merge_kernel, tiles_p=tiles_p),
        grid=(n // tm, d // tn),
        in_specs=_split_rows_specs(D_A, tm, tiles_p) + _split_rows_specs(D_B, tm, tiles_p) + [
            pl.BlockSpec((tm, D_B), lambda i, j: (i, E_GB // D_B)),
            pl.BlockSpec((1, HEAD_DIM), lambda i, j: (0, 0)),
            pl.BlockSpec((D_A + D_B, tn), lambda i, j: (0, j)),
            pl.BlockSpec((tm, tn), lambda i, j: (i, j)),
        ],
        out_specs=pl.BlockSpec((tm, tn), lambda i, j: (i, j)),
        out_shape=jax.ShapeDtypeStruct((n, d), F32),
        scratch_shapes=[pltpu.VMEM((tm, D_A + D_B), BF16)],
        compiler_params=_cparams(2),
        name="even_merge",
    )(oa_p, oa_s, ob_p, ob_s, proj, o_norm.reshape(1, -1), w_out, x)


_FFN_TM = 512
_FFN_TF = 512
_FFN_TN = 512


def _norm_cast_kernel(x_ref, g_ref, o_ref):
    o_ref[...] = _rms(x_ref[...], g_ref[...]).astype(BF16)


def norm_cast(x, g, tm):
    n, d = x.shape
    return pl.pallas_call(
        _norm_cast_kernel,
        grid=(n // tm,),
        in_specs=[pl.BlockSpec((tm, d), lambda i: (i, 0)), pl.BlockSpec((1, d), lambda i: (0, 0))],
        out_specs=pl.BlockSpec((tm, d), lambda i: (i, 0)),
        out_shape=jax.ShapeDtypeStruct((n, d), BF16),
        compiler_params=_cparams(1),
        name="norm_cast",
    )(x, g.reshape(1, d))


def _group_start(te_ref, i):
    return (i == 0) | (te_ref[i] != te_ref[jnp.maximum(i - 1, 0)])


def _by_valid_rows(nv, tm, compute, o_ref):
    half = tm // 2

    @pl.when(nv > half)
    def _():
        o_ref[...] = compute(slice(None))

    @pl.when((nv > 0) & (nv <= half))
    def _():
        o_ref[:half, :] = compute(slice(0, half))
        o_ref[half:, :] = jnp.zeros((tm - half, o_ref.shape[1]), o_ref.dtype)

    @pl.when(nv == 0)
    def _():
        o_ref[...] = jnp.zeros_like(o_ref)


def _ffn_up_kernel(te_ref, nv_ref, x_ref, wg_ref, wu_ref, *rest, packed):
    o_ref, wg_s, wu_s = rest[-3:]
    i = pl.program_id(1)

    @pl.when(_group_start(te_ref, i))
    def _():
        wg_s[...] = wg_ref[0].astype(BF16)
        wu_s[...] = wu_ref[0].astype(BF16)

    def compute(rows):
        x = _unpack_bf16_pairs(x_ref[rows, :]) if packed else x_ref[rows, :]
        return (_silu(_dot(x, wg_s[...])) * _dot(x, wu_s[...])).astype(BF16)

    _by_valid_rows(nv_ref[i], x_ref.shape[0], compute, o_ref)


def ffn_up(x, tile_expert, tile_valid, w_gate, w_up, out=None, tile0=0, total_tiles=None):
    p = x.shape[0]
    d, d_ff = w_gate.shape[1], w_gate.shape[2]
    tm, tf = min(_FFN_TM, p), _FFN_TF
    rows_out = (total_tiles if total_tiles is not None else p // tm) * tm
    in_specs = [
        pl.BlockSpec((tm, x.shape[1]), lambda f, i, te, nv: (i, 0)),
        pl.BlockSpec((1, d, tf), lambda f, i, te, nv: (te[i], 0, f)),
        pl.BlockSpec((1, d, tf), lambda f, i, te, nv: (te[i], 0, f)),
    ]
    args = [x, w_gate, w_up]
    aliases = {}
    if out is not None:
        in_specs.append(pl.BlockSpec(memory_space=pl.ANY))
        args.append(out)
        aliases = {2 + len(args) - 1: 0}
    return pl.pallas_call(
        functools.partial(_ffn_up_kernel, packed=x.dtype == jnp.uint32),
        grid_spec=pltpu.PrefetchScalarGridSpec(
            num_scalar_prefetch=2,
            grid=(d_ff // tf, p // tm),
            in_specs=in_specs,
            out_specs=pl.BlockSpec((tm, tf), lambda f, i, te, nv: (i + tile0, f)),
            scratch_shapes=[pltpu.VMEM((d, tf), BF16), pltpu.VMEM((d, tf), BF16)],
        ),
        out_shape=jax.ShapeDtypeStruct((rows_out, d_ff), BF16),
        input_output_aliases=aliases,
        compiler_params=_cparams(2),
        name="ffn_up",
    )(tile_expert, tile_valid, *args)


def _ffn_down_kernel(te_ref, nv_ref, a_ref, wd_ref, *rest, has_gate, has_res):
    rest = list(rest)
    gate_ref = rest.pop(0) if has_gate else None
    res_ref = rest.pop(0) if has_res else None
    o_ref, wd_s = rest
    i = pl.program_id(1)

    @pl.when(_group_start(te_ref, i))
    def _():
        wd_s[...] = wd_ref[0].astype(BF16)

    def compute(rows):
        y = _dot(a_ref[rows, :], wd_s[...])
        if has_gate:
            y = y * gate_ref[rows, :]
        if has_res:
            y = res_ref[rows, :] + y
        return y

    _by_valid_rows(nv_ref[i], a_ref.shape[0], compute, o_ref)


def ffn_down(act, tile_expert, tile_valid, w_down, gate=None, res=None):
    p, d_ff = act.shape
    d = w_down.shape[2]
    tm, tn = min(_FFN_TM, p), _FFN_TN
    in_specs = [
        pl.BlockSpec((tm, d_ff), lambda n, i, te, nv: (i, 0)),
        pl.BlockSpec((1, d_ff, tn), lambda n, i, te, nv: (te[i], 0, n)),
    ]
    args = [act, w_down]
    if gate is not None:
        in_specs.append(pl.BlockSpec((tm, 1), lambda n, i, te, nv: (i, 0)))
        args.append(gate)
    if res is not None:
        in_specs.append(pl.BlockSpec((tm, tn), lambda n, i, te, nv: (i, n)))
        args.append(res)
    return pl.pallas_call(
        functools.partial(_ffn_down_kernel, has_gate=gate is not None, has_res=res is not None),
        grid_spec=pltpu.PrefetchScalarGridSpec(
            num_scalar_prefetch=2,
            grid=(d // tn, p // tm),
            in_specs=in_specs,
            out_specs=pl.BlockSpec((tm, tn), lambda n, i, te, nv: (i, n)),
            scratch_shapes=[pltpu.VMEM((d_ff, tn), BF16)],
        ),
        out_shape=jax.ShapeDtypeStruct((p, d), F32),
        compiler_params=_cparams(2),
        name="ffn_down",
    )(tile_expert, tile_valid, *args)


def dense_ffn(x, g, w_gate, w_up, w_down):
    n = x.shape[0]
    tm = min(_FFN_TM, n)
    tiles = n // tm
    te = jnp.zeros((tiles,), jnp.int32)
    nv = jnp.full((tiles,), tm, jnp.int32)
    act = ffn_up(norm_cast(x, g, tm), te, nv, w_gate[None], w_up[None])
    return ffn_down(act, te, nv, w_down[None], res=x)


def _odd_proj_kernel(x_ref, g_ref, w_ref, wf_ref, qn_ref, kn_ref, fb_ref, o_ref, lf_ref, hn_ref):
    j = pl.program_id(1)
    nq = D_C // _TN
    nk = KVD_C // _TN

    @pl.when(j == 0)
    def _():
        hn = _rms(x_ref[...], g_ref[...]).astype(BF16)
        hn_ref[...] = hn
        z = _dot(hn, wf_ref[...].astype(BF16)) + fb_ref[...]
        lf_ref[...] = jnp.minimum(z, 0.0) - jnp.log(1.0 + jnp.exp(-jnp.abs(z)))

    acc = _dot(hn_ref[...], w_ref[...].astype(BF16))

    def head_norm(gain):
        for h in range(_TN // HEAD_DIM):
            sl = slice(h * HEAD_DIM, (h + 1) * HEAD_DIM)
            o_ref[:, sl] = _rms(acc[:, sl], gain)

    @pl.when(j < nq)
    def _():
        head_norm(qn_ref[...])

    @pl.when((j >= nq) & (j < nq + nk))
    def _():
        head_norm(kn_ref[...])

    @pl.when(j >= nq + nk)
    def _():
        o_ref[...] = acc


def odd_project(x, g, w_in, q_norm, k_norm, f_bias, tm):
    n, d = x.shape
    w_f = w_in[:, O_OUT:]
    return pl.pallas_call(
        _odd_proj_kernel,
        grid=(n // tm, O_OUT // _TN),
        in_specs=[
            pl.BlockSpec((tm, d), lambda i, j: (i, 0)),
            pl.BlockSpec((1, d), lambda i, j: (0, 0)),
            pl.BlockSpec((d, _TN), lambda i, j: (0, j)),
            pl.BlockSpec((d, N_HEADS_C), lambda i, j: (0, 0)),
            pl.BlockSpec((1, HEAD_DIM), lambda i, j: (0, 0)),
            pl.BlockSpec((1, HEAD_DIM), lambda i, j: (0, 0)),
            pl.BlockSpec((1, N_HEADS_C), lambda i, j: (0, 0)),
        ],
        out_specs=[
            pl.BlockSpec((tm, _TN), lambda i, j: (i, j)),
            pl.BlockSpec((tm, N_HEADS_C), lambda i, j: (i, 0)),
        ],
        out_shape=[jax.ShapeDtypeStruct((n, O_OUT), F32),
                   jax.ShapeDtypeStruct((n, N_HEADS_C), F32)],
        scratch_shapes=[pltpu.VMEM((tm, d), BF16)],
        compiler_params=_cparams(2),
        name="odd_project",
    )(x, g.reshape(1, d), w_in, w_f, q_norm.reshape(1, -1), k_norm.reshape(1, -1),
      f_bias.reshape(1, -1))


def _out_proj_kernel(ap_ref, as_ref, w_ref, x_ref, o_ref, lhs_ref, *, tiles_p):
    i = pl.program_id(0)

    @pl.when((pl.program_id(1) == 0) & (i < tiles_p))
    def _():
        lhs_ref[...] = ap_ref[...].astype(BF16)

    @pl.when((pl.program_id(1) == 0) & (i >= tiles_p))
    def _():
        lhs_ref[...] = as_ref[...].astype(BF16)

    o_ref[...] = x_ref[...] + _dot(lhs_ref[...], w_ref[...].astype(BF16))


def out_project(a_p, a_s, w, x, tm, tn=512):
    n, d = x.shape
    k = a_p.shape[1]
    tiles_p = a_p.shape[0] // tm
    return pl.pallas_call(
        functools.partial(_out_proj_kernel, tiles_p=tiles_p),
        grid=(n // tm, d // tn),
        in_specs=_split_rows_specs(k, tm, tiles_p) + [
            pl.BlockSpec((k, tn), lambda i, j: (0, j)),
            pl.BlockSpec((tm, tn), lambda i, j: (i, j)),
        ],
        out_specs=pl.BlockSpec((tm, tn), lambda i, j: (i, j)),
        out_shape=jax.ShapeDtypeStruct((n, d), F32),
        scratch_shapes=[pltpu.VMEM((tm, k), BF16)],
        compiler_params=_cparams(2),
        name="out_project",
    )(a_p, a_s, w, x)


def _t5_bucket(dist):
    d = jnp.maximum(dist, 0)
    max_exact = REL_BUCKETS // 2
    ratio = jnp.log(jnp.maximum(d, 1).astype(F32) / max_exact) / math.log(REL_MAX_DIST / max_exact)
    large = jnp.minimum(max_exact + (ratio * (REL_BUCKETS - max_exact)).astype(jnp.int32),
                        REL_BUCKETS - 1)
    return jnp.where(d < max_exact, d, large)


def _bias_from_buckets(bucket, rel_ref, h):
    out = jnp.zeros(bucket.shape, F32)
    for b in range(REL_BUCKETS):
        out = jnp.where(bucket == b, rel_ref[b, h], out)
    return out


def _swa_prompt_kernel(q_ref, kp_ref, kc_ref, vp_ref, vc_ref, bkt_ref, rel_ref, sink_ref,
                       o_ref, bias_ref):
    n = pl.program_id(1)
    kv = pl.program_id(2)

    @pl.when((pl.program_id(0) == 0) & (n == 0) & (kv == 0))
    def _():
        for h in range(N_HEADS_A):
            bias_ref[h] = _bias_from_buckets(bkt_ref[...], rel_ref, h)

    kk = jnp.concatenate([kp_ref[...], kc_ref[...]], axis=0).astype(BF16)
    vv = jnp.concatenate([vp_ref[...], vc_ref[...]], axis=0).astype(BF16)
    row = lax.broadcasted_iota(jnp.int32, (WINDOW, 2 * WINDOW), 0)
    col = lax.broadcasted_iota(jnp.int32, (WINDOW, 2 * WINDOW), 1)
    dist = WINDOW + row - col
    valid = (dist >= 0) & (dist < WINDOW) & ((n > 0) | (col >= WINDOW))
    for g in range(G_A):
        h = kv * G_A + g
        sl = slice(g * HEAD_DIM, (g + 1) * HEAD_DIM)
        s = _dot_nt(q_ref[:, sl].astype(BF16), kk) * SCALE + bias_ref[h]
        s = jnp.where(valid, s, NEG)
        sk = sink_ref[h]
        m = jnp.maximum(jnp.max(s, axis=-1, keepdims=True), sk)
        e = jnp.exp(s - m)
        p = e / (jnp.sum(e, axis=-1, keepdims=True) + jnp.exp(sk - m))
        o_ref[:, sl] = _dot(p.astype(BF16), vv)


def swa_prompt(proj, rel_bias, sinks, batch, seq):
    nb = seq // WINDOW
    i = jnp.arange(WINDOW)[:, None]
    j = jnp.arange(2 * WINDOW)[None, :]
    bucket = _t5_bucket(WINDOW + i - j).astype(jnp.int32)
    kcol, vcol = E_KA // HEAD_DIM, E_VA // HEAD_DIM
    smem = pl.BlockSpec(memory_space=pltpu.SMEM)
    return pl.pallas_call(
        _swa_prompt_kernel,
        grid=(batch, nb, N_KV_A),
        in_specs=[
            pl.BlockSpec((WINDOW, G_A * HEAD_DIM), lambda b, n, k: (b * nb + n, k)),
            pl.BlockSpec((WINDOW, HEAD_DIM), lambda b, n, k: (b * nb + jnp.maximum(n - 1, 0), kcol + k)),
            pl.BlockSpec((WINDOW, HEAD_DIM), lambda b, n, k: (b * nb + n, kcol + k)),
            pl.BlockSpec((WINDOW, HEAD_DIM), lambda b, n, k: (b * nb + jnp.maximum(n - 1, 0), vcol + k)),
            pl.BlockSpec((WINDOW, HEAD_DIM), lambda b, n, k: (b * nb + n, vcol + k)),
            pl.BlockSpec((WINDOW, 2 * WINDOW), lambda b, n, k: (0, 0)),
            smem, smem,
        ],
        out_specs=pl.BlockSpec((WINDOW, G_A * HEAD_DIM), lambda b, n, k: (b * nb + n, k)),
        out_shape=jax.ShapeDtypeStruct((batch * seq, D_A), F32),
        scratch_shapes=[pltpu.VMEM((N_HEADS_A, WINDOW, 2 * WINDOW), F32)],
        compiler_params=_cparams(3),
        name="swa_prompt",
    )(proj, proj, proj, proj, proj, bucket, rel_bias, sinks)


_SWA_SB = 8


def _swa_sample_kernel(q_ref, kn_ref, vn_ref, bk_ref, bv_ref, bkt_ref, rel_ref, sink_ref,
                       o_ref, wk_ref, wv_ref, bias_ref, *, ds, win):
    rows = G_A * ds
    nkeys = win + ds

    @pl.when(pl.program_id(0) == 0)
    def _():
        for h in range(N_HEADS_A):
            kv, g = divmod(h, G_A)
            bias_ref[kv, g * ds:(g + 1) * ds, :] = _bias_from_buckets(bkt_ref[...], rel_ref, h)

    t = lax.broadcasted_iota(jnp.int32, (rows, nkeys), 0) % ds
    s_idx = lax.broadcasted_iota(jnp.int32, (rows, nkeys), 1)
    dist = win + t - s_idx
    valid = (dist >= 0) & (dist < WINDOW)
    g_of_row = lax.broadcasted_iota(jnp.int32, (rows, 1), 0) // ds
    for kv in range(N_KV_A):
        sk = jnp.zeros((rows, 1), F32)
        for g in range(G_A):
            sk = jnp.where(g_of_row == g, sink_ref[kv * G_A + g], sk)
        csl = slice(kv * HEAD_DIM, (kv + 1) * HEAD_DIM)
        for b in range(_SWA_SB):
            rsl = slice(b * ds, (b + 1) * ds)
            q = jnp.concatenate(
                [q_ref[rsl, (kv * G_A + g) * HEAD_DIM:(kv * G_A + g + 1) * HEAD_DIM] for g in range(G_A)],
                axis=0).astype(BF16)
            kk = jnp.concatenate([bk_ref[b, :, csl], kn_ref[rsl, csl]], axis=0).astype(BF16)
            vv = jnp.concatenate([bv_ref[b, :, csl], vn_ref[rsl, csl]], axis=0).astype(BF16)
            s = _dot_nt(q, kk) * SCALE + bias_ref[kv]
            s = jnp.where(valid, s, NEG)
            m = jnp.maximum(jnp.max(s, axis=-1, keepdims=True), sk)
            e = jnp.exp(s - m)
            p = e / (jnp.sum(e, axis=-1, keepdims=True) + jnp.exp(sk - m))
            o = _dot(p.astype(BF16), vv)
            for g in range(G_A):
                o_ref[rsl, (kv * G_A + g) * HEAD_DIM:(kv * G_A + g + 1) * HEAD_DIM] = o[g * ds:(g + 1) * ds]
    for b in range(_SWA_SB):
        rsl = slice(b * ds, (b + 1) * ds)
        wk_ref[b, :win - ds, :] = bk_ref[b, ds:, :]
        wk_ref[b, win - ds:, :] = kn_ref[rsl, :]
        wv_ref[b, :win - ds, :] = bv_ref[b, ds:, :]
        wv_ref[b, win - ds:, :] = vn_ref[rsl, :]


def swa_sample(proj, row0, cache_k, cache_v, rel_bias, sinks, db, ds):
    win = cache_k.shape[1]
    rb = _SWA_SB * ds
    r0 = row0 // rb
    bucket = _t5_bucket(win + jnp.arange(ds)[:, None] - jnp.arange(win + ds)[None, :]).astype(jnp.int32)
    smem = pl.BlockSpec(memory_space=pltpu.SMEM)
    return pl.pallas_call(
        functools.partial(_swa_sample_kernel, ds=ds, win=win),
        grid=(db // _SWA_SB,),
        in_specs=[
            pl.BlockSpec((rb, D_A), lambda i: (r0 + i, 0)),
            pl.BlockSpec((rb, KVD_A), lambda i: (r0 + i, E_KA // KVD_A)),
            pl.BlockSpec((rb, KVD_A), lambda i: (r0 + i, E_VA // KVD_A)),
            pl.BlockSpec((_SWA_SB, win, KVD_A), lambda i: (i, 0, 0)),
            pl.BlockSpec((_SWA_SB, win, KVD_A), lambda i: (i, 0, 0)),
            pl.BlockSpec((ds, win + ds), lambda i: (0, 0)),
            smem, smem,
        ],
        out_specs=[
            pl.BlockSpec((rb, D_A), lambda i: (i, 0)),
            pl.BlockSpec((_SWA_SB, win, KVD_A), lambda i: (i, 0, 0)),
            pl.BlockSpec((_SWA_SB, win, KVD_A), lambda i: (i, 0, 0)),
        ],
        out_shape=[jax.ShapeDtypeStruct((db * ds, D_A), F32),
                   jax.ShapeDtypeStruct(cache_k.shape, F32),
                   jax.ShapeDtypeStruct(cache_v.shape, F32)],
        scratch_shapes=[pltpu.VMEM((N_KV_A, G_A * ds, win + ds), F32)],
        compiler_params=_cparams(1),
        name="swa_sample",
    )(proj, proj, proj, cache_k, cache_v, bucket, rel_bias, sinks)


_HG_CHUNK = 128
_HG_LEVELS = int(math.log2(_HG_CHUNK))


def _hgrn_constants(c):
    levels = int(math.log2(c))
    t = np.arange(c)
    mats = [(t[None, :] <= t[:, None])]
    lvl = np.full((c, c), -1, np.int32)
    lvl[t, t] = levels
    for l in range(levels):
        m = c >> (l + 1)
        pivot = (t // (2 * m)) * (2 * m) + m
        mats.append(t[None, :] <= pivot[:, None])
        same = (t[:, None] // (2 * m)) == (t[None, :] // (2 * m))
        pair = same & ((t[:, None] % (2 * m)) >= m) & ((t[None, :] % (2 * m)) < m)
        lvl[pair] = l
    return np.concatenate(mats, axis=0).astype(np.float32), lvl


def _column_of(row):
    n = row.shape[1]
    r = lax.broadcasted_iota(jnp.int32, (n, n), 0)
    c = lax.broadcasted_iota(jnp.int32, (n, n), 1)
    return jnp.sum(jnp.where(r == c, jnp.broadcast_to(row, (n, n)), 0.0), axis=1, keepdims=True)


def _hgrn_prompt_kernel(q_ref, f_ref, v_ref, cm_ref, lvl_ref, o_ref, s_out_ref, s_ref):
    c = _HG_CHUNK
    ci = pl.program_id(2)

    @pl.when(ci == 0)
    def _():
        s_ref[...] = jnp.zeros_like(s_ref)

    lvl = lvl_ref[...]
    sums_all = _dot_exact01(cm_ref[...], jnp.log(f_ref[...]))
    for h in range(_HG_HP):
        hs = slice(h * HEAD_DIM, (h + 1) * HEAD_DIM)
        q = q_ref[:, hs]
        v = v_ref[:, hs].astype(BF16)
        k = 1.0 - f_ref[:, hs]
        sums = sums_all[:, hs]
        b = sums[:c]
        a = jnp.where(lvl == _HG_LEVELS, _dot_nt(q.astype(BF16), k.astype(BF16)), 0.0)
        for l in range(_HG_LEVELS):
            e = jnp.exp(-jnp.abs(b - sums[(l + 1) * c:(l + 2) * c]))
            a = a + jnp.where(lvl == l, _dot_nt((q * e).astype(BF16), (k * e).astype(BF16)), 0.0)
        s0 = s_ref[h]
        o_ref[:, hs] = _dot((q * jnp.exp(b)).astype(BF16), s0.astype(BF16)) + _dot(a.astype(BF16), v)
        b_last = b[c - 1:c, :]
        khat = (k * jnp.exp(b_last - b)).astype(BF16)
        s_new = _column_of(jnp.exp(b_last)) * s0 + _dot_tn(khat, v)
        s_ref[h] = s_new

        @pl.when(ci == pl.num_programs(2) - 1)
        def _():
            s_out_ref[0, h] = s_new


_HG_HP = 4


def hgrn_prompt(proj, batch, seq):
    c = _HG_CHUNK
    nc = seq // c
    hp = _HG_HP
    cmat, lvl = _hgrn_constants(c)
    w = hp * HEAD_DIM
    qc, fc, vc = E_QB // w, E_F // w, E_VB // w
    return pl.pallas_call(
        _hgrn_prompt_kernel,
        grid=(batch, N_HEADS_B // hp, nc),
        in_specs=[
            pl.BlockSpec((c, w), lambda b, h, i: (b * nc + i, qc + h)),
            pl.BlockSpec((c, w), lambda b, h, i: (b * nc + i, fc + h)),
            pl.BlockSpec((c, w), lambda b, h, i: (b * nc + i, vc + h)),
            pl.BlockSpec(cmat.shape, lambda b, h, i: (0, 0)),
            pl.BlockSpec(lvl.shape, lambda b, h, i: (0, 0)),
        ],
        out_specs=[
            pl.BlockSpec((c, w), lambda b, h, i: (b * nc + i, h)),
            pl.BlockSpec((1, hp, HEAD_DIM, HEAD_DIM), lambda b, h, i: (b, h, 0, 0)),
        ],
        out_shape=[jax.ShapeDtypeStruct((batch * seq, D_B), F32),
                   jax.ShapeDtypeStruct((batch, N_HEADS_B, HEAD_DIM, HEAD_DIM), F32)],
        scratch_shapes=[pltpu.VMEM((hp, HEAD_DIM, HEAD_DIM), F32)],
        compiler_params=_cparams(3),
        name="hgrn_prompt",
    )(proj, proj, proj, jnp.asarray(cmat), jnp.asarray(lvl))


_HG_SB = 8


def _hgrn_sample_kernel(q_ref, f_ref, v_ref, st_ref, o_ref, s_out_ref, *, ds):
    s_idx = lax.broadcasted_iota(jnp.int32, (ds, 1), 0)
    for sb in range(_HG_SB):
        rsl = slice(sb * ds, (sb + 1) * ds)
        q = q_ref[rsl, :]
        f = f_ref[rsl, :]
        v = v_ref[rsl, :]
        k = 1.0 - f
        lf = jnp.log(f)
        rows = [lf[0:1]]
        for t in range(1, ds):
            rows.append(rows[-1] + lf[t:t + 1])
        b = jnp.concatenate(rows, axis=0)
        s0 = st_ref[sb, 0]
        o_inter = _dot((q * jnp.exp(b)).astype(BF16), s0.astype(BF16))
        outs = []
        for t in range(ds):
            w = q[t:t + 1] * k * jnp.exp(jnp.minimum(b[t:t + 1] - b, 0.0))
            a_t = jnp.where(s_idx <= t, jnp.sum(w, axis=-1, keepdims=True), 0.0)
            outs.append(jnp.sum(a_t * v, axis=0, keepdims=True))
        o_ref[rsl, :] = o_inter + jnp.concatenate(outs, axis=0)
        b_last = b[ds - 1:ds, :]
        khat = (k * jnp.exp(b_last - b)).astype(BF16)
        s_out_ref[sb, 0] = _column_of(jnp.exp(b_last)) * s0 + _dot_tn(khat, v.astype(BF16))


def hgrn_sample(proj, row0, state, db, ds):
    rb = _HG_SB * ds
    r0 = row0 // rb
    qc, fc, vc = E_QB // HEAD_DIM, E_F // HEAD_DIM, E_VB // HEAD_DIM
    return pl.pallas_call(
        functools.partial(_hgrn_sample_kernel, ds=ds),
        grid=(db // _HG_SB, N_HEADS_B),
        in_specs=[
            pl.BlockSpec((rb, HEAD_DIM), lambda i, h: (r0 + i, qc + h)),
            pl.BlockSpec((rb, HEAD_DIM), lambda i, h: (r0 + i, fc + h)),
            pl.BlockSpec((rb, HEAD_DIM), lambda i, h: (r0 + i, vc + h)),
            pl.BlockSpec((_HG_SB, 1, HEAD_DIM, HEAD_DIM), lambda i, h: (i, h, 0, 0)),
        ],
        out_specs=[
            pl.BlockSpec((rb, HEAD_DIM), lambda i, h: (i, h)),
            pl.BlockSpec((_HG_SB, 1, HEAD_DIM, HEAD_DIM), lambda i, h: (i, h, 0, 0)),
        ],
        out_shape=[jax.ShapeDtypeStruct((db * ds, D_B), F32),
                   jax.ShapeDtypeStruct(state.shape, F32)],
        compiler_params=_cparams(2),
        name="hgrn_sample",
    )(proj, proj, proj, state)


_CS_BLK = 128


def _split3(x):
    x1 = x.astype(BF16).astype(F32)
    r = x - x1
    x2 = r.astype(BF16).astype(F32)
    return x1, x2, r - x2


_AUG_ONES = 3 * N_HEADS_C
_INV_SCALE = HEAD_DIM ** 0.5
_LOG2E = 1.4426950408889634


def _cumsum_kernel(lf_ref, tri_ref, place_ref, c_ref, aug_ref):
    nblk = lf_ref.shape[1] // _CS_BLK
    lane = lax.broadcasted_iota(jnp.int32, (_CS_BLK, HEAD_DIM), 1)
    ones = jnp.where((lane >= _AUG_ONES) & (lane < _AUG_ONES + 3), 1.0, 0.0)

    def body(i, carry):
        r = pl.ds(pl.multiple_of(i * _CS_BLK, _CS_BLK), _CS_BLK)
        c = _dot_exact01(tri_ref[...], lf_ref[0, r, :]) + carry
        c_ref[0, r, :] = c
        aug = ones
        for j, piece in enumerate(_split3(c * _INV_SCALE)):
            aug = aug + _dot(piece.astype(BF16), place_ref[j].astype(BF16))
        aug_ref[0, r, :] = aug.astype(BF16)
        return c[_CS_BLK - 1:_CS_BLK, :]

    lax.fori_loop(0, nblk, body, jnp.zeros((1, lf_ref.shape[2]), F32))


def cumsum_rows(lf):
    b, s, h = lf.shape
    t = np.arange(_CS_BLK)
    tri = jnp.asarray((t[None, :] <= t[:, None]).astype(np.float32))
    place = np.zeros((3, h, HEAD_DIM), np.float32)
    for j in range(3):
        place[j, np.arange(h), j * h + np.arange(h)] = 1.0
    return pl.pallas_call(
        _cumsum_kernel,
        grid=(b,),
        in_specs=[pl.BlockSpec((1, s, h), lambda i: (i, 0, 0)),
                  pl.BlockSpec((_CS_BLK, _CS_BLK), lambda i: (0, 0)),
                  pl.BlockSpec((3, h, HEAD_DIM), lambda i: (0, 0, 0))],
        out_specs=[pl.BlockSpec((1, s, h), lambda i: (i, 0, 0)),
                   pl.BlockSpec((1, s, HEAD_DIM), lambda i: (i, 0, 0))],
        out_shape=[jax.ShapeDtypeStruct(lf.shape, F32),
                   jax.ShapeDtypeStruct((b, s, HEAD_DIM), BF16)],
        compiler_params=_cparams(1),
        name="fox_cumsum",
    )(lf, tri, jnp.asarray(place))


_FOX_TQ = 512
_FOX_TK = 512


_ONES_ROWS = 8


def _fox_prompt_kernel(qi_ref, ki_ref, q_ref, k_ref, vt_ref, cq_ref, ca_ref, o_ref, qa_ref, m_ref, acc_ref):
    kv = pl.program_id(1)
    qi = qi_ref[pl.program_id(2)]
    ki = ki_ref[pl.program_id(2)]
    tq, tk = q_ref.shape[0], k_ref.shape[0]

    @pl.when(ki == 0)
    def _():
        m_ref[...] = jnp.full_like(m_ref, NEG)
        acc_ref[...] = jnp.zeros_like(acc_ref)
        lane = lax.broadcasted_iota(jnp.int32, (tq, HEAD_DIM), 1)
        for g in range(G_C):
            h = kv * G_C + g
            own = (lane == h) | (lane == N_HEADS_C + h) | (lane == 2 * N_HEADS_C + h)
            extra = jnp.where(own, -1.0, 0.0)
            for j, piece in enumerate(_split3(cq_ref[0, 0, :, g:g + 1] * _INV_SCALE)):
                extra = jnp.where(lane == _AUG_ONES + j, piece, extra)
            qa_ref[g, :, :HEAD_DIM] = q_ref[:, g * HEAD_DIM:(g + 1) * HEAD_DIM].astype(BF16)
            qa_ref[g, :, HEAD_DIM:] = extra.astype(BF16)

    def update(masked):
        kaug = jnp.concatenate([k_ref[...].astype(BF16), ca_ref[0]], axis=1)
        vt = jnp.concatenate([vt_ref[0, 0].astype(BF16), jnp.ones((_ONES_ROWS, tk), BF16)],
                             axis=0)
        if masked:
            causal = (lax.broadcasted_iota(jnp.int32, (tk, tq), 0)
                      <= lax.broadcasted_iota(jnp.int32, (tk, tq), 1))
        for g in range(G_C):
            s = _dot_nt(kaug, qa_ref[g]) * (SCALE * _LOG2E)
            if masked:
                s = jnp.where(causal, s, NEG)
            m_prev = m_ref[g]
            m_new = jnp.maximum(m_prev, jnp.max(s, axis=0, keepdims=True))
            alpha = jnp.exp2(m_prev - m_new)
            p = jnp.exp2(s - m_new)
            acc_ref[g] = alpha * acc_ref[g] + _dot(vt, p.astype(BF16))
            m_ref[g] = m_new

    @pl.when(ki < qi)
    def _():
        update(False)

    @pl.when(ki == qi)
    def _():
        update(True)

    @pl.when(ki == qi)
    def _():
        for g in range(G_C):
            acc = acc_ref[g]
            o_ref[:, g * HEAD_DIM:(g + 1) * HEAD_DIM] = jnp.transpose(
                acc[:HEAD_DIM] / acc[HEAD_DIM:HEAD_DIM + 1])


def fox_prompt(projo, c, c_aug, batch, seq):
    tq = tk = min(_FOX_TQ, seq)
    nq, nk = seq // tq, seq // tk
    c_q = jnp.transpose(c.reshape(batch, seq, N_KV_C, G_C), (0, 2, 1, 3))
    v_t = jnp.transpose(projo[:batch * seq, O_V:O_V + KVD_C].reshape(batch, seq, N_KV_C, HEAD_DIM),
                        (0, 2, 3, 1))
    kcol = O_K // HEAD_DIM
    pairs = [(i, j) for i in range(nq) for j in range(i + 1)]
    qi_tab = jnp.asarray([p[0] for p in pairs], jnp.int32)
    ki_tab = jnp.asarray([p[1] for p in pairs], jnp.int32)
    return pl.pallas_call(
        _fox_prompt_kernel,
        grid_spec=pltpu.PrefetchScalarGridSpec(
            num_scalar_prefetch=2,
            grid=(batch, N_KV_C, len(pairs)),
            in_specs=[
                pl.BlockSpec((tq, G_C * HEAD_DIM), lambda b, k, p, qi, ki: (b * nq + qi[p], k)),
                pl.BlockSpec((tk, HEAD_DIM), lambda b, k, p, qi, ki: (b * nk + ki[p], kcol + k)),
                pl.BlockSpec((1, 1, HEAD_DIM, tk), lambda b, k, p, qi, ki: (b, k, 0, ki[p])),
                pl.BlockSpec((1, 1, tq, G_C), lambda b, k, p, qi, ki: (b, k, qi[p], 0)),
                pl.BlockSpec((1, tk, HEAD_DIM), lambda b, k, p, qi, ki: (b, ki[p], 0)),
            ],
            out_specs=pl.BlockSpec((tq, G_C * HEAD_DIM), lambda b, k, p, qi, ki: (b * nq + qi[p], k)),
            scratch_shapes=[pltpu.VMEM((G_C, tq, 2 * HEAD_DIM), BF16),
                            pltpu.VMEM((G_C, 1, tq), F32),
                            pltpu.VMEM((G_C, HEAD_DIM + _ONES_ROWS, tq), F32)],
        ),
        out_shape=jax.ShapeDtypeStruct((batch * seq, D_C), F32),
        compiler_params=_cparams(3),
        name="fox_prompt",
    )(qi_tab, ki_tab, projo, projo, v_t, c_q, c_aug)


_FOX_PP = 16
_FOX_NSEQ = 1


def _fox_sample_kernel(pt_ref, q_ref, kn_ref, vn_ref, lfn_ref, tri_ref, ut1_ref, *refs, ds, page):
    del pt_ref
    pp, nseq = _FOX_PP, _FOX_NSEQ
    n_pg = nseq * pp
    k_refs, v_refs, lf_refs = refs[:n_pg], refs[n_pg:2 * n_pg], refs[2 * n_pg:3 * n_pg]
    o_ref, qb_ref, cq_ref, m_ref, l_ref, acc_ref, carry_ref = refs[3 * n_pg:]
    nrow = N_HEADS_C * ds
    rows_kv = G_C * ds
    step = pl.program_id(1)

    @pl.when(step == 0)
    def _():
        causal = (lax.broadcasted_iota(jnp.int32, (rows_kv, ds), 1)
                  <= lax.broadcasted_iota(jnp.int32, (rows_kv, ds), 0) % ds)
        t_row = lax.broadcasted_iota(jnp.int32, (nrow, ds), 0) % ds
        s_col = lax.broadcasted_iota(jnp.int32, (nrow, ds), 1)
        for j in range(nseq):
            qb_ref[j] = q_ref[j].astype(BF16)
            cn = _dot_exact01_r(lfn_ref[j], tri_ref[...])
            cq_ref[j] = jnp.sum(jnp.where(s_col == t_row, cn, 0.0), axis=-1, keepdims=True)
            tok = slice(j * ds, (j + 1) * ds)
            for kv in range(N_KV_C):
                rsl = slice(kv * rows_kv, (kv + 1) * rows_kv)
                csl = slice(kv * HEAD_DIM, (kv + 1) * HEAD_DIM)
                s = (_dot_nt(qb_ref[j, rsl, :], kn_ref[tok, csl].astype(BF16)) * SCALE
                     + cq_ref[j, rsl, :] - cn[rsl])
                s = jnp.where(causal, s, NEG)
                m = jnp.max(s, axis=-1, keepdims=True)
                e = jnp.exp(s - m)
                m_ref[j, rsl, :] = m
                l_ref[j, rsl, :] = jnp.sum(e, axis=-1, keepdims=True)
                acc_ref[j, rsl, :] = _dot(e.astype(BF16), vn_ref[tok, csl].astype(BF16))
        carry_ref[...] = jnp.zeros_like(carry_ref)

    for j in range(nseq):
        kj, vj, lfj = (r[j * pp:(j + 1) * pp] for r in (k_refs, v_refs, lf_refs))
        lf_stack = jnp.concatenate([lfj[i][0] for i in range(pp)], axis=0)
        sums = _dot_exact01_r(lf_stack, ut1_ref[...])
        carry = carry_ref[j]
        dex = []
        for i in range(pp):
            hs = slice(i * N_HEADS_C, (i + 1) * N_HEADS_C)
            dex.append(sums[hs, :page] + carry)
            carry = carry + sums[hs, page:]
        carry_ref[j] = carry

        def kv_rows(refs_, kv):
            strided = pl.ds(kv, page, stride=N_KV_C)
            return jnp.concatenate([refs_[i][0, strided, :] for i in range(pp)], axis=0).astype(BF16)

        qk = jnp.concatenate(
            [_dot_nt(qb_ref[j, kv * rows_kv:(kv + 1) * rows_kv, :], kv_rows(kj, kv))
             for kv in range(N_KV_C)], axis=0)
        bias = jnp.concatenate(
            [jnp.broadcast_to(d[:, None, :], (N_HEADS_C, ds, page)).reshape(nrow, page) for d in dex], axis=1)
        s = qk * SCALE + cq_ref[j] + bias
        m_prev = m_ref[j]
        m_new = jnp.maximum(m_prev, jnp.max(s, axis=-1, keepdims=True))
        alpha = jnp.exp(m_prev - m_new)
        p = jnp.exp(s - m_new)
        l_ref[j] = alpha * l_ref[j] + jnp.sum(p, axis=-1, keepdims=True)
        m_ref[j] = m_new
        pb = p.astype(BF16)
        pv = jnp.concatenate(
            [_dot(pb[kv * rows_kv:(kv + 1) * rows_kv, :], kv_rows(vj, kv)) for kv in range(N_KV_C)], axis=0)
        acc_ref[j] = alpha * acc_ref[j] + pv

    @pl.when(step == pl.num_programs(1) - 1)
    def _():
        for j in range(nseq):
            o_ref[j] = acc_ref[j] / l_ref[j]


def fox_sample(projo, row0, lf_s, cache_k, cache_v, cache_lf, page_table, db, ds):
    pool, page = cache_lf.shape[0], cache_lf.shape[1]
    n_pages = page_table.shape[1]
    pp = _FOX_PP
    nrow = N_HEADS_C * ds
    q = projo[row0:row0 + db * ds, O_Q:O_Q + D_C].reshape(db, ds, N_HEADS_C, HEAD_DIM)
    q = jnp.transpose(q, (0, 2, 1, 3)).reshape(db, nrow, HEAD_DIM)
    lfn = jnp.transpose(lf_s.reshape(db, ds, N_HEADS_C), (0, 2, 1))
    lfn = jnp.broadcast_to(lfn[:, :, None, :], (db, N_HEADS_C, ds, ds)).reshape(db, nrow, ds)
    cache_lft = jnp.transpose(cache_lf, (0, 2, 1))
    t = np.arange(ds)
    tri = jnp.asarray((t[:, None] <= t[None, :]).astype(np.float32))
    u = np.arange(page)
    ut1 = jnp.asarray(np.concatenate([(u[:, None] > u[None, :]), np.ones((page, page), bool)],
                                     axis=1).astype(np.float32))
    nseq = _FOX_NSEQ
    r0 = row0 // (nseq * ds)

    def page_map(j, i):
        return lambda b, p, pt: (pt[b * nseq + j, n_pages - 1 - (p * pp + i)], 0, 0)

    pages = [(j, i) for j in range(nseq) for i in range(pp)]
    in_specs = [
        pl.BlockSpec((nseq, nrow, HEAD_DIM), lambda b, p, pt: (b, 0, 0)),
        pl.BlockSpec((nseq * ds, KVD_C), lambda b, p, pt: (r0 + b, O_K // KVD_C)),
        pl.BlockSpec((nseq * ds, KVD_C), lambda b, p, pt: (r0 + b, O_V // KVD_C)),
        pl.BlockSpec((nseq, nrow, ds), lambda b, p, pt: (b, 0, 0)),
        pl.BlockSpec((ds, ds), lambda b, p, pt: (0, 0)),
        pl.BlockSpec((page, 2 * page), lambda b, p, pt: (0, 0)),
    ]
    in_specs += [pl.BlockSpec((1, page * N_KV_C, HEAD_DIM), page_map(j, i)) for j, i in pages]
    in_specs += [pl.BlockSpec((1, page * N_KV_C, HEAD_DIM), page_map(j, i)) for j, i in pages]
    in_specs += [pl.BlockSpec((1, N_HEADS_C, page), page_map(j, i)) for j, i in pages]
    out = pl.pallas_call(
        functools.partial(_fox_sample_kernel, ds=ds, page=page),
        grid_spec=pltpu.PrefetchScalarGridSpec(
            num_scalar_prefetch=1,
            grid=(db // nseq, n_pages // pp),
            in_specs=in_specs,
            out_specs=pl.BlockSpec((nseq, nrow, HEAD_DIM), lambda b, p, pt: (b, 0, 0)),
            scratch_shapes=[pltpu.VMEM((nseq, nrow, HEAD_DIM), BF16), pltpu.VMEM((nseq, nrow, 1), F32),
                            pltpu.VMEM((nseq, nrow, 1), F32), pltpu.VMEM((nseq, nrow, 1), F32),
                            pltpu.VMEM((nseq, nrow, HEAD_DIM), F32),
                            pltpu.VMEM((nseq, N_HEADS_C, page), F32)],
        ),
        out_shape=jax.ShapeDtypeStruct((db, nrow, HEAD_DIM), F32),
        compiler_params=_cparams(2),
        name="fox_sample",
    )(page_table, q, projo, projo, lfn, tri, ut1, *([cache_k] * len(pages)), *([cache_v] * len(pages)),
      *([cache_lft] * len(pages)))
    out = jnp.transpose(out.reshape(db, N_HEADS_C, ds, HEAD_DIM), (0, 2, 1, 3))
    return out.reshape(db * ds, D_C)


def _pack_bf16_pairs(h):
    c = h.shape[1] // 2
    hb = h.astype(BF16).astype(F32)
    bits = lax.bitcast_convert_type(hb, jnp.uint32)
    return (bits[:, :c] & jnp.uint32(0xFFFF0000)) | (bits[:, c:] >> 16)


def _unpack_bf16_pairs(u):
    hi = lax.bitcast_convert_type(u & jnp.uint32(0xFFFF0000), F32).astype(BF16)
    lo = lax.bitcast_convert_type(u << 16, F32).astype(BF16)
    return jnp.concatenate([hi, lo], axis=1)


def _moe_route_kernel(x_ref, g_ref, r_ref, h_ref, idx_ref, w_ref):
    hn = _rms(x_ref[...], g_ref[...])
    h_ref[...] = _pack_bf16_pairs(hn)
    logits = jnp.dot(hn, r_ref[...], preferred_element_type=F32, precision=lax.Precision.HIGHEST)
    lane = lax.broadcasted_iota(jnp.int32, logits.shape, 1)
    m1 = jnp.max(logits, axis=-1, keepdims=True)
    i1 = jnp.min(jnp.where(logits == m1, lane, N_EXPERTS), axis=-1, keepdims=True)
    rest = jnp.where(lane == i1, -jnp.inf, logits)
    m2 = jnp.max(rest, axis=-1, keepdims=True)
    i2 = jnp.min(jnp.where(rest == m2, lane, N_EXPERTS), axis=-1, keepdims=True)
    e = jnp.exp(m2 - m1)
    slot = lax.broadcasted_iota(jnp.int32, idx_ref.shape, 1)
    idx_ref[...] = jnp.where(slot == 0, i1, i2)
    w_ref[...] = jnp.where(slot == 0, 1.0 / (1.0 + e), e / (1.0 + e))


def moe_route(x, g, router, tm):
    n, d = x.shape
    return pl.pallas_call(
        _moe_route_kernel,
        grid=(n // tm,),
        in_specs=[pl.BlockSpec((tm, d), lambda i: (i, 0)),
                  pl.BlockSpec((1, d), lambda i: (0, 0)),
                  pl.BlockSpec((d, N_EXPERTS), lambda i: (0, 0))],
        out_specs=[pl.BlockSpec((tm, d // 2), lambda i: (i, 0)),
                   pl.BlockSpec((tm, TOP_K), lambda i: (i, 0)),
                   pl.BlockSpec((tm, TOP_K), lambda i: (i, 0))],
        out_shape=[jax.ShapeDtypeStruct((n, d // 2), jnp.uint32),
                   jax.ShapeDtypeStruct((n, TOP_K), jnp.int32),
                   jax.ShapeDtypeStruct((n, TOP_K), F32)],
        compiler_params=_cparams(1),
        name="moe_route",
    )(x, g.reshape(1, d), router)


def _route_plan(idx, w, tm):
    n = idx.shape[0]
    flat_e = idx.reshape(-1)
    onehot = (flat_e[:, None] == jnp.arange(N_EXPERTS, dtype=jnp.int32)[None, :]).astype(jnp.int32)
    csum = jnp.cumsum(onehot, axis=0)
    rank = jnp.take_along_axis(csum, flat_e[:, None], axis=1)[:, 0] - 1
    count = csum[-1]
    tiles_e = (count + tm - 1) // tm
    tile_end = jnp.cumsum(tiles_e)
    tile_start = tile_end - tiles_e
    pos = tile_start[flat_e] * tm + rank
    n_tiles = (TOP_K * n) // tm + N_EXPERTS
    tile = jnp.arange(n_tiles, dtype=jnp.int32)
    tile_expert = jnp.minimum(jnp.searchsorted(tile_end, tile, side="right"), N_EXPERTS - 1).astype(jnp.int32)
    tile_valid = jnp.clip(count[tile_expert] - (tile - tile_start[tile_expert]) * tm, 0, tm)
    tile_valid = jnp.where(tile < tile_end[-1], tile_valid, 0).astype(jnp.int32)
    pair = jnp.zeros((n_tiles * tm,), jnp.int32).at[pos].set(jnp.arange(TOP_K * n, dtype=jnp.int32))
    row_in_tile = jnp.arange(n_tiles * tm, dtype=jnp.int32) % tm
    gate = jnp.where(row_in_tile < jnp.repeat(tile_valid, tm), jnp.take(w.reshape(-1), pair, mode="clip"), 0.0)
    return pos.reshape(n, TOP_K), pair // TOP_K, gate.reshape(-1, 1), tile_expert, tile_valid


_MOE_CHUNKS = 4


def moe_layer(x, g, router, w_gate, w_up, w_down, tm_route, n_split):
    tm = min(_FFN_TM, x.shape[0])
    h, idx, w = moe_route(x, g, router, tm_route)
    pos, token, gate, tile_expert, tile_valid = _route_plan(idx, w, tm)
    tiles = token.shape[0] // tm
    n_chunks = next(c for c in (_MOE_CHUNKS, 2, 1) if tiles % c == 0)
    tiles_c = tiles // n_chunks
    act = None
    for c in range(n_chunks):
        ts = slice(c * tiles_c, (c + 1) * tiles_c)
        h_sorted = jnp.take(h, token[c * tiles_c * tm:(c + 1) * tiles_c * tm], axis=0, mode="clip")
        act = ffn_up(h_sorted, tile_expert[ts], tile_valid[ts], w_gate, w_up,
                     out=act, tile0=c * tiles_c, total_tiles=tiles)
    y_sorted = ffn_down(act, tile_expert, tile_valid, w_down, gate=gate)
    y0 = jnp.take(y_sorted, pos[:, 0], axis=0, mode="clip")
    y1 = jnp.take(y_sorted, pos[:, 1], axis=0, mode="clip")
    return tuple(x[r] + y0[r] + y1[r] for r in (slice(0, n_split), slice(n_split, None)))


def _row_tile(n, want):
    t = min(want, n)
    while n % t:
        t //= 2
    return t


def kernel(x_prompt, x_sample, cache_k_win, cache_v_win, state_hgrn, cache_k_fox, cache_v_fox, cache_logf_fox, page_table, rel_bias, norm_mix_e, w_in_e, q_norm_a, k_norm_a, sinks_a, lb_b, o_norm_b, w_out_e, norm_ffn_e, w_gate_e, w_up_e, w_down_e, norm_mix_o, w_in_o, fgate_bias_c, q_norm_c, k_norm_c, w_out_o, norm_ffn_o, router_o, w_gate_x, w_up_x, w_down_x):
    batch, seq, d = x_prompt.shape
    db, ds, _ = x_sample.shape
    n_p, n_s = batch * seq, db * ds
    n = n_p + n_s
    win = cache_k_win.shape[2]
    pool, page = cache_k_fox.shape[1], cache_k_fox.shape[2]
    tm_big = _row_tile(math.gcd(n_p, n_s), 1024)

    x = jnp.concatenate([x_prompt.reshape(n_p, d), x_sample.reshape(n_s, d)], axis=0)

    proj = even_project(x, norm_mix_e[0], w_in_e[0], q_norm_a[0], k_norm_a[0], lb_b, tm_big)
    oa_p = swa_prompt(proj, rel_bias, sinks_a[0], batch, seq)
    oa_s, wk_s, wv_s = swa_sample(proj, n_p, cache_k_win[0].reshape(db, win, KVD_A),
                                  cache_v_win[0].reshape(db, win, KVD_A), rel_bias, sinks_a[0], db, ds)
    ob_p, hg_p = hgrn_prompt(proj, batch, seq)
    ob_s, hg_s = hgrn_sample(proj, n_p, state_hgrn[0], db, ds)
    y = even_merge(oa_p, oa_s, ob_p, ob_s, proj, o_norm_b[0], w_out_e[0], x, tm_big // 2)
    y = dense_ffn(y, norm_ffn_e[0], w_gate_e[0], w_up_e[0], w_down_e[0])

    projo, lf = odd_project(y, norm_mix_o[0], w_in_o[0], q_norm_c[0], k_norm_c[0], fgate_bias_c[0], tm_big)
    c, c_aug = cumsum_rows(lf[:n_p].reshape(batch, seq, N_HEADS_C))
    oc_p = fox_prompt(projo, c, c_aug, batch, seq)
    oc_s = fox_sample(projo, n_p, lf[n_p:], cache_k_fox[0].reshape(pool, page * N_KV_C, HEAD_DIM),
                      cache_v_fox[0].reshape(pool, page * N_KV_C, HEAD_DIM), cache_logf_fox[0],
                      page_table, db, ds)
    y = out_project(oc_p, oc_s, w_out_o[0], y, tm_big)
    y_p, y_s = moe_layer(y, norm_ffn_o[0], router_o[0], w_gate_x[0], w_up_x[0], w_down_x[0], tm_big, n_p)

    def prompt_tail(col, width, heads):
        t = proj[:n_p, col:col + width].reshape(batch, seq, heads, HEAD_DIM)
        return t[:, seq - WINDOW:][None]

    return (
        y_p.reshape(batch, seq, d),
        y_s.reshape(db, ds, d),
        prompt_tail(E_KA, KVD_A, N_KV_A),
        prompt_tail(E_VA, KVD_A, N_KV_A),
        wk_s.reshape(1, db, win, N_KV_A, HEAD_DIM),
        wv_s.reshape(1, db, win, N_KV_A, HEAD_DIM),
        hg_p[None],
        hg_s[None],
        projo[:n_p, O_K:O_K + KVD_C].reshape(1, batch, seq, N_KV_C, HEAD_DIM),
        projo[:n_p, O_V:O_V + KVD_C].reshape(1, batch, seq, N_KV_C, HEAD_DIM),
        lf[:n_p].reshape(1, batch, seq, N_HEADS_C),
        projo[n_p:, O_K:O_K + KVD_C].reshape(1, db, ds, N_KV_C, HEAD_DIM),
        projo[n_p:, O_V:O_V + KVD_C].reshape(1, db, ds, N_KV_C, HEAD_DIM),
        lf[n_p:].reshape(1, db, ds, N_HEADS_C),
    )
```

```python
import functools
import math

import numpy as np
import jax
import jax.numpy as jnp
from jax import lax
from jax.experimental import pallas as pl
from jax.experimental.pallas import tpu as pltpu

F32 = jnp.float32
BF16 = jnp.bfloat16

HEAD_DIM = 128
N_HEADS_A, N_KV_A = 8, 2
G_A = N_HEADS_A // N_KV_A
WINDOW = 128
REL_BUCKETS, REL_MAX_DIST = 32, 128
N_HEADS_B = 8
N_HEADS_C, N_KV_C = 16, 4
G_C = N_HEADS_C // N_KV_C
N_EXPERTS, TOP_K = 8, 2
EPS = 1e-6
SCALE = HEAD_DIM ** -0.5
NEG = -1e30

D_A = N_HEADS_A * HEAD_DIM
KVD_A = N_KV_A * HEAD_DIM
D_B = N_HEADS_B * HEAD_DIM
D_C = N_HEADS_C * HEAD_DIM
KVD_C = N_KV_C * HEAD_DIM

E_QA, E_QB, E_F, E_VB, E_GB = 0, D_A, D_A + D_B, D_A + 2 * D_B, D_A + 3 * D_B
E_KA = D_A + 4 * D_B
E_VA = E_KA + KVD_A
E_OUT = E_VA + KVD_A
O_Q, O_K, O_V = 0, D_C, D_C + KVD_C
O_OUT = D_C + 2 * KVD_C

VMEM_LIMIT = 56 * 1024 * 1024


def _cparams(n_axes):
    return pltpu.CompilerParams(dimension_semantics=("arbitrary",) * n_axes,
                                vmem_limit_bytes=VMEM_LIMIT)


def _rms(x, gain):
    return x * lax.rsqrt(jnp.mean(x * x, axis=-1, keepdims=True) + EPS) * gain


def _silu(x):
    return x * (1.0 / (1.0 + jnp.exp(-x)))


def _dot(a, b):
    return jnp.dot(a, b, preferred_element_type=F32)


def _dot_nt(a, b):
    return lax.dot_general(a, b, (((1,), (1,)), ((), ())), preferred_element_type=F32)


def _dot_tn(a, b):
    return lax.dot_general(a, b, (((0,), (0,)), ((), ())), preferred_element_type=F32)


def _dot_exact01(m01, x):
    x1 = x.astype(BF16)
    r1 = x - x1.astype(F32)
    x2 = r1.astype(BF16)
    x3 = (r1 - x2.astype(F32)).astype(BF16)
    m = m01.astype(BF16)
    return _dot(m, x1) + _dot(m, x2) + _dot(m, x3)


def _dot_exact01_r(x, m01):
    x1 = x.astype(BF16)
    r1 = x - x1.astype(F32)
    x2 = r1.astype(BF16)
    x3 = (r1 - x2.astype(F32)).astype(BF16)
    m = m01.astype(BF16)
    return _dot(x1, m) + _dot(x2, m) + _dot(x3, m)


_TN = 256


def _even_out_tile(j):
    return jnp.where(j < 4, j, jnp.where(j < 6, j + 16, j - 2))


def _even_proj_kernel(x_ref, g_ref, w_ref, qn_ref, kn_ref, lb_ref, o_ref, hn_ref):
    j = pl.program_id(1)

    @pl.when(j == 0)
    def _():
        hn_ref[...] = _rms(x_ref[...], g_ref[...]).astype(BF16)

    acc = _dot(hn_ref[...], w_ref[...].astype(BF16))

    def head_norm(gain):
        for h in range(_TN // HEAD_DIM):
            sl = slice(h * HEAD_DIM, (h + 1) * HEAD_DIM)
            o_ref[:, sl] = _rms(acc[:, sl], gain)

    @pl.when(j < 4)
    def _():
        head_norm(qn_ref[...])

    @pl.when(j == 4)
    def _():
        head_norm(kn_ref[...])

    @pl.when((j == 5) | (j >= 14))
    def _():
        o_ref[...] = acc

    @pl.when((j >= 6) & (j < 10))
    def _():
        o_ref[...] = _silu(acc)

    @pl.when((j >= 10) & (j < 14))
    def _():
        lb = lb_ref[...]
        e = jnp.exp(lb - jnp.max(lb, axis=0, keepdims=True))
        lb0 = e[0:1, :] / jnp.sum(e, axis=0, keepdims=True)
        o_ref[...] = lb0 + (1.0 - lb0) * (1.0 / (1.0 + jnp.exp(-acc)))


def even_project(x, g, w_in, q_norm, k_norm, lb_b, tm):
    n, d = x.shape
    n_tiles = w_in.shape[1] // _TN
    return pl.pallas_call(
        _even_proj_kernel,
        grid=(n // tm, n_tiles),
        in_specs=[
            pl.BlockSpec((tm, d), lambda i, j: (i, 0)),
            pl.BlockSpec((1, d), lambda i, j: (0, 0)),
            pl.BlockSpec((d, _TN), lambda i, j: (0, j)),
            pl.BlockSpec((1, HEAD_DIM), lambda i, j: (0, 0)),
            pl.BlockSpec((1, HEAD_DIM), lambda i, j: (0, 0)),
            pl.BlockSpec((lb_b.shape[0], _TN), lambda i, j: (0, jnp.clip(j - 10, 0, 3))),
        ],
        out_specs=pl.BlockSpec((tm, _TN), lambda i, j: (i, _even_out_tile(j))),
        out_shape=jax.ShapeDtypeStruct((n, E_OUT), F32),
        scratch_shapes=[pltpu.VMEM((tm, d), BF16)],
        compiler_params=_cparams(2),
        name="even_project",
    )(x, g.reshape(1, d), w_in, q_norm.reshape(1, -1), k_norm.reshape(1, -1), lb_b)


def _split_rows_specs(shape_cols, tm, tiles_p):
    return [pl.BlockSpec((tm, shape_cols), lambda i, j: (jnp.minimum(i, tiles_p - 1), 0)),
            pl.BlockSpec((tm, shape_cols), lambda i, j: (jnp.maximum(i - tiles_p, 0), 0))]


def _even_merge_kernel(oap_ref, oas_ref, obp_ref, obs_ref, gb_ref, on_ref, w_ref, x_ref, o_ref, lhs_ref,
                       *, tiles_p):
    i = pl.program_id(0)

    def fill(oa_ref, ob_ref):
        lhs_ref[:, :D_A] = oa_ref[...].astype(BF16)
        for h in range(N_HEADS_B):
            sl = slice(h * HEAD_DIM, (h + 1) * HEAD_DIM)
            obn = _rms(ob_ref[:, sl], on_ref[...]) * _silu(gb_ref[:, sl])
            lhs_ref[:, D_A + h * HEAD_DIM:D_A + (h + 1) * HEAD_DIM] = obn.astype(BF16)

    @pl.when((pl.program_id(1) == 0) & (i < tiles_p))
    def _():
        fill(oap_ref, obp_ref)

    @pl.when((pl.program_id(1) == 0) & (i >= tiles_p))
    def _():
        fill(oas_ref, obs_ref)

    o_ref[...] = x_ref[...] + _dot(lhs_ref[...], w_ref[...].astype(BF16))


def even_merge(oa_p, oa_s, ob_p, ob_s, proj, o_norm, w_out, x, tm, tn=512):
    n, d = x.shape
    tiles_p = oa_p.shape[0] // tm
    return pl.pallas_call(
        functools.partial(_even_merge_kernel, tiles_p=tiles_p),
        grid=(n // tm, d // tn),
        in_specs=_split_rows_specs(D_A, tm, tiles_p) + _split_rows_specs(D_B, tm, tiles_p) + [
            pl.BlockSpec((tm, D_B), lambda i, j: (i, E_GB // D_B)),
            pl.BlockSpec((1, HEAD_DIM), lambda i, j: (0, 0)),
            pl.BlockSpec((D_A + D_B, tn), lambda i, j: (0, j)),
            pl.BlockSpec((tm, tn), lambda i, j: (i, j)),
        ],
        out_specs=pl.BlockSpec((tm, tn), lambda i, j: (i, j)),
        out_shape=jax.ShapeDtypeStruct((n, d), F32),
        scratch_shapes=[pltpu.VMEM((tm, D_A + D_B), BF16)],
        compiler_params=_cparams(2),
        name="even_merge",
    )(oa_p, oa_s, ob_p, ob_s, proj, o_norm.reshape(1, -1), w_out, x)


_FFN_TM = 512
_FFN_TF = 512
_FFN_TN = 512


def _norm_cast_kernel(x_ref, g_ref, o_ref):
    o_ref[...] = _rms(x_ref[...], g_ref[...]).astype(BF16)


def norm_cast(x, g, tm):
    n, d = x.shape
    return pl.pallas_call(
        _norm_cast_kernel,
        grid=(n // tm,),
        in_specs=[pl.BlockSpec((tm, d), lambda i: (i, 0)), pl.BlockSpec((1, d), lambda i: (0, 0))],
        out_specs=pl.BlockSpec((tm, d), lambda i: (i, 0)),
        out_shape=jax.ShapeDtypeStruct((n, d), BF16),
        compiler_params=_cparams(1),
        name="norm_cast",
    )(x, g.reshape(1, d))


def _group_start(te_ref, i):
    return (i == 0) | (te_ref[i] != te_ref[jnp.maximum(i - 1, 0)])


def _by_valid_rows(nv, tm, compute, o_ref):
    half = tm // 2

    @pl.when(nv > half)
    def _():
        o_ref[...] = compute(slice(None))

    @pl.when((nv > 0) & (nv <= half))
    def _():
        o_ref[:half, :] = compute(slice(0, half))
        o_ref[half:, :] = jnp.zeros((tm - half, o_ref.shape[1]), o_ref.dtype)

    @pl.when(nv == 0)
    def _():
        o_ref[...] = jnp.zeros_like(o_ref)


def _ffn_up_kernel(te_ref, nv_ref, x_ref, wg_ref, wu_ref, *rest, packed):
    o_ref, wg_s, wu_s = rest[-3:]
    i = pl.program_id(1)

    @pl.when(_group_start(te_ref, i))
    def _():
        wg_s[...] = wg_ref[0].astype(BF16)
        wu_s[...] = wu_ref[0].astype(BF16)

    def compute(rows):
        x = _unpack_bf16_pairs(x_ref[rows, :]) if packed else x_ref[rows, :]
        return (_silu(_dot(x, wg_s[...])) * _dot(x, wu_s[...])).astype(BF16)

    _by_valid_rows(nv_ref[i], x_ref.shape[0], compute, o_ref)


def ffn_up(x, tile_expert, tile_valid, w_gate, w_up, out=None, tile0=0, total_tiles=None):
    p = x.shape[0]
    d, d_ff = w_gate.shape[1], w_gate.shape[2]
    tm, tf = min(_FFN_TM, p), _FFN_TF
    rows_out = (total_tiles if total_tiles is not None else p // tm) * tm
    in_specs = [
        pl.BlockSpec((tm, x.shape[1]), lambda f, i, te, nv: (i, 0)),
        pl.BlockSpec((1, d, tf), lambda f, i, te, nv: (te[i], 0, f)),
        pl.BlockSpec((1, d, tf), lambda f, i, te, nv: (te[i], 0, f)),
    ]
    args = [x, w_gate, w_up]
    aliases = {}
    if out is not None:
        in_specs.append(pl.BlockSpec(memory_space=pl.ANY))
        args.append(out)
        aliases = {2 + len(args) - 1: 0}
    return pl.pallas_call(
        functools.partial(_ffn_up_kernel, packed=x.dtype == jnp.uint32),
        grid_spec=pltpu.PrefetchScalarGridSpec(
            num_scalar_prefetch=2,
            grid=(d_ff // tf, p // tm),
            in_specs=in_specs,
            out_specs=pl.BlockSpec((tm, tf), lambda f, i, te, nv: (i + tile0, f)),
            scratch_shapes=[pltpu.VMEM((d, tf), BF16), pltpu.VMEM((d, tf), BF16)],
        ),
        out_shape=jax.ShapeDtypeStruct((rows_out, d_ff), BF16),
        input_output_aliases=aliases,
        compiler_params=_cparams(2),
        name="ffn_up",
    )(tile_expert, tile_valid, *args)


def _ffn_down_kernel(te_ref, nv_ref, a_ref, wd_ref, *rest, has_gate, has_res):
    rest = list(rest)
    gate_ref = rest.pop(0) if has_gate else None
    res_ref = rest.pop(0) if has_res else None
    o_ref, wd_s = rest
    i = pl.program_id(1)

    @pl.when(_group_start(te_ref, i))
    def _():
        wd_s[...] = wd_ref[0].astype(BF16)

    def compute(rows):
        y = _dot(a_ref[rows, :], wd_s[...])
        if has_gate:
            y = y * gate_ref[rows, :]
        if has_res:
            y = res_ref[rows, :] + y
        return y

    _by_valid_rows(nv_ref[i], a_ref.shape[0], compute, o_ref)


def ffn_down(act, tile_expert, tile_valid, w_down, gate=None, res=None):
    p, d_ff = act.shape
    d = w_down.shape[2]
    tm, tn = min(_FFN_TM, p), _FFN_TN
    in_specs = [
        pl.BlockSpec((tm, d_ff), lambda n, i, te, nv: (i, 0)),
        pl.BlockSpec((1, d_ff, tn), lambda n, i, te, nv: (te[i], 0, n)),
    ]
    args = [act, w_down]
    if gate is not None:
        in_specs.append(pl.BlockSpec((tm, 1), lambda n, i, te, nv: (i, 0)))
        args.append(gate)
    if res is not None:
        in_specs.append(pl.BlockSpec((tm, tn), lambda n, i, te, nv: (i, n)))
        args.append(res)
    return pl.pallas_call(
        functools.partial(_ffn_down_kernel, has_gate=gate is not None, has_res=res is not None),
        grid_spec=pltpu.PrefetchScalarGridSpec(
            num_scalar_prefetch=2,
            grid=(d // tn, p // tm),
            in_specs=in_specs,
            out_specs=pl.BlockSpec((tm, tn), lambda n, i, te, nv: (i, n)),
            scratch_shapes=[pltpu.VMEM((d_ff, tn), BF16)],
        ),
        out_shape=jax.ShapeDtypeStruct((p, d), F32),
        compiler_params=_cparams(2),
        name="ffn_down",
    )(tile_expert, tile_valid, *args)


def dense_ffn(x, g, w_gate, w_up, w_down):
    n = x.shape[0]
    tm = min(_FFN_TM, n)
    tiles = n // tm
    te = jnp.zeros((tiles,), jnp.int32)
    nv = jnp.full((tiles,), tm, jnp.int32)
    act = ffn_up(norm_cast(x, g, tm), te, nv, w_gate[None], w_up[None])
    return ffn_down(act, te, nv, w_down[None], res=x)


def _odd_proj_kernel(x_ref, g_ref, w_ref, wf_ref, qn_ref, kn_ref, fb_ref, o_ref, lf_ref, hn_ref):
    j = pl.program_id(1)
    nq = D_C // _TN
    nk = KVD_C // _TN

    @pl.when(j == 0)
    def _():
        hn = _rms(x_ref[...], g_ref[...]).astype(BF16)
        hn_ref[...] = hn
        z = _dot(hn, wf_ref[...].astype(BF16)) + fb_ref[...]
        lf_ref[...] = jnp.minimum(z, 0.0) - jnp.log(1.0 + jnp.exp(-jnp.abs(z)))

    acc = _dot(hn_ref[...], w_ref[...].astype(BF16))

    def head_norm(gain):
        for h in range(_TN // HEAD_DIM):
            sl = slice(h * HEAD_DIM, (h + 1) * HEAD_DIM)
            o_ref[:, sl] = _rms(acc[:, sl], gain)

    @pl.when(j < nq)
    def _():
        head_norm(qn_ref[...])

    @pl.when((j >= nq) & (j < nq + nk))
    def _():
        head_norm(kn_ref[...])

    @pl.when(j >= nq + nk)
    def _():
        o_ref[...] = acc


def odd_project(x, g, w_in, q_norm, k_norm, f_bias, tm):
    n, d = x.shape
    w_f = w_in[:, O_OUT:]
    return pl.pallas_call(
        _odd_proj_kernel,
        grid=(n // tm, O_OUT // _TN),
        in_specs=[
            pl.BlockSpec((tm, d), lambda i, j: (i, 0)),
            pl.BlockSpec((1, d), lambda i, j: (0, 0)),
            pl.BlockSpec((d, _TN), lambda i, j: (0, j)),
            pl.BlockSpec((d, N_HEADS_C), lambda i, j: (0, 0)),
            pl.BlockSpec((1, HEAD_DIM), lambda i, j: (0, 0)),
            pl.BlockSpec((1, HEAD_DIM), lambda i, j: (0, 0)),
            pl.BlockSpec((1, N_HEADS_C), lambda i, j: (0, 0)),
        ],
        out_specs=[
            pl.BlockSpec((tm, _TN), lambda i, j: (i, j)),
            pl.BlockSpec((tm, N_HEADS_C), lambda i, j: (i, 0)),
        ],
        out_shape=[jax.ShapeDtypeStruct((n, O_OUT), F32),
                   jax.ShapeDtypeStruct((n, N_HEADS_C), F32)],
        scratch_shapes=[pltpu.VMEM((tm, d), BF16)],
        compiler_params=_cparams(2),
        name="odd_project",
    )(x, g.reshape(1, d), w_in, w_f, q_norm.reshape(1, -1), k_norm.reshape(1, -1),
      f_bias.reshape(1, -1))


def _out_proj_kernel(ap_ref, as_ref, w_ref, x_ref, o_ref, lhs_ref, *, tiles_p):
    i = pl.program_id(0)

    @pl.when((pl.program_id(1) == 0) & (i < tiles_p))
    def _():
        lhs_ref[...] = ap_ref[...].astype(BF16)

    @pl.when((pl.program_id(1) == 0) & (i >= tiles_p))
    def _():
        lhs_ref[...] = as_ref[...].astype(BF16)

    o_ref[...] = x_ref[...] + _dot(lhs_ref[...], w_ref[...].astype(BF16))


def out_project(a_p, a_s, w, x, tm, tn=512):
    n, d = x.shape
    k = a_p.shape[1]
    tiles_p = a_p.shape[0] // tm
    return pl.pallas_call(
        functools.partial(_out_proj_kernel, tiles_p=tiles_p),
        grid=(n // tm, d // tn),
        in_specs=_split_rows_specs(k, tm, tiles_p) + [
            pl.BlockSpec((k, tn), lambda i, j: (0, j)),
            pl.BlockSpec((tm, tn), lambda i, j: (i, j)),
        ],
        out_specs=pl.BlockSpec((tm, tn), lambda i, j: (i, j)),
        out_shape=jax.ShapeDtypeStruct((n, d), F32),
        scratch_shapes=[pltpu.VMEM((tm, k), BF16)],
        compiler_params=_cparams(2),
        name="out_project",
    )(a_p, a_s, w, x)


def _t5_bucket(dist):
    d = jnp.maximum(dist, 0)
    max_exact = REL_BUCKETS // 2
    ratio = jnp.log(jnp.maximum(d, 1).astype(F32) / max_exact) / math.log(REL_MAX_DIST / max_exact)
    large = jnp.minimum(max_exact + (ratio * (REL_BUCKETS - max_exact)).astype(jnp.int32),
                        REL_BUCKETS - 1)
    return jnp.where(d < max_exact, d, large)


def _bias_from_buckets(bucket, rel_ref, h):
    out = jnp.zeros(bucket.shape, F32)
    for b in range(REL_BUCKETS):
        out = jnp.where(bucket == b, rel_ref[b, h], out)
    return out


def _swa_prompt_kernel(q_ref, kp_ref, kc_ref, vp_ref, vc_ref, bkt_ref, rel_ref, sink_ref,
                       o_ref, bias_ref):
    n = pl.program_id(1)
    kv = pl.program_id(2)

    @pl.when((pl.program_id(0) == 0) & (n == 0) & (kv == 0))
    def _():
        for h in range(N_HEADS_A):
            bias_ref[h] = _bias_from_buckets(bkt_ref[...], rel_ref, h)

    kk = jnp.concatenate([kp_ref[...], kc_ref[...]], axis=0).astype(BF16)
    vv = jnp.concatenate([vp_ref[...], vc_ref[...]], axis=0).astype(BF16)
    row = lax.broadcasted_iota(jnp.int32, (WINDOW, 2 * WINDOW), 0)
    col = lax.broadcasted_iota(jnp.int32, (WINDOW, 2 * WINDOW), 1)
    dist = WINDOW + row - col
    valid = (dist >= 0) & (dist < WINDOW) & ((n > 0) | (col >= WINDOW))
    for g in range(G_A):
        h = kv * G_A + g
        sl = slice(g * HEAD_DIM, (g + 1) * HEAD_DIM)
        s = _dot_nt(q_ref[:, sl].astype(BF16), kk) * SCALE + bias_ref[h]
        s = jnp.where(valid, s, NEG)
        sk = sink_ref[h]
        m = jnp.maximum(jnp.max(s, axis=-1, keepdims=True), sk)
        e = jnp.exp(s - m)
        p = e / (jnp.sum(e, axis=-1, keepdims=True) + jnp.exp(sk - m))
        o_ref[:, sl] = _dot(p.astype(BF16), vv)


def swa_prompt(proj, rel_bias, sinks, batch, seq):
    nb = seq // WINDOW
    i = jnp.arange(WINDOW)[:, None]
    j = jnp.arange(2 * WINDOW)[None, :]
    bucket = _t5_bucket(WINDOW + i - j).astype(jnp.int32)
    kcol, vcol = E_KA // HEAD_DIM, E_VA // HEAD_DIM
    smem = pl.BlockSpec(memory_space=pltpu.SMEM)
    return pl.pallas_call(
        _swa_prompt_kernel,
        grid=(batch, nb, N_KV_A),
        in_specs=[
            pl.BlockSpec((WINDOW, G_A * HEAD_DIM), lambda b, n, k: (b * nb + n, k)),
            pl.BlockSpec((WINDOW, HEAD_DIM), lambda b, n, k: (b * nb + jnp.maximum(n - 1, 0), kcol + k)),
            pl.BlockSpec((WINDOW, HEAD_DIM), lambda b, n, k: (b * nb + n, kcol + k)),
            pl.BlockSpec((WINDOW, HEAD_DIM), lambda b, n, k: (b * nb + jnp.maximum(n - 1, 0), vcol + k)),
            pl.BlockSpec((WINDOW, HEAD_DIM), lambda b, n, k: (b * nb + n, vcol + k)),
            pl.BlockSpec((WINDOW, 2 * WINDOW), lambda b, n, k: (0, 0)),
            smem, smem,
        ],
        out_specs=pl.BlockSpec((WINDOW, G_A * HEAD_DIM), lambda b, n, k: (b * nb + n, k)),
        out_shape=jax.ShapeDtypeStruct((batch * seq, D_A), F32),
        scratch_shapes=[pltpu.VMEM((N_HEADS_A, WINDOW, 2 * WINDOW), F32)],
        compiler_params=_cparams(3),
        name="swa_prompt",
    )(proj, proj, proj, proj, proj, bucket, rel_bias, sinks)


_SWA_SB = 8


def _swa_sample_kernel(q_ref, kn_ref, vn_ref, bk_ref, bv_ref, bkt_ref, rel_ref, sink_ref,
                       o_ref, wk_ref, wv_ref, bias_ref, *, ds, win):
    rows = G_A * ds
    nkeys = win + ds

    @pl.when(pl.program_id(0) == 0)
    def _():
        for h in range(N_HEADS_A):
            kv, g = divmod(h, G_A)
            bias_ref[kv, g * ds:(g + 1) * ds, :] = _bias_from_buckets(bkt_ref[...], rel_ref, h)

    t = lax.broadcasted_iota(jnp.int32, (rows, nkeys), 0) % ds
    s_idx = lax.broadcasted_iota(jnp.int32, (rows, nkeys), 1)
    dist = win + t - s_idx
    valid = (dist >= 0) & (dist < WINDOW)
    g_of_row = lax.broadcasted_iota(jnp.int32, (rows, 1), 0) // ds
    for kv in range(N_KV_A):
        sk = jnp.zeros((rows, 1), F32)
        for g in range(G_A):
            sk = jnp.where(g_of_row == g, sink_ref[kv * G_A + g], sk)
        csl = slice(kv * HEAD_DIM, (kv + 1) * HEAD_DIM)
        for b in range(_SWA_SB):
            rsl = slice(b * ds, (b + 1) * ds)
            q = jnp.concatenate(
                [q_ref[rsl, (kv * G_A + g) * HEAD_DIM:(kv * G_A + g + 1) * HEAD_DIM] for g in range(G_A)],
                axis=0).astype(BF16)
            kk = jnp.concatenate([bk_ref[b, :, csl], kn_ref[rsl, csl]], axis=0).astype(BF16)
            vv = jnp.concatenate([bv_ref[b, :, csl], vn_ref[rsl, csl]], axis=0).astype(BF16)
            s = _dot_nt(q, kk) * SCALE + bias_ref[kv]
            s = jnp.where(valid, s, NEG)
            m = jnp.maximum(jnp.max(s, axis=-1, keepdims=True), sk)
            e = jnp.exp(s - m)
            p = e / (jnp.sum(e, axis=-1, keepdims=True) + jnp.exp(sk - m))
            o = _dot(p.astype(BF16), vv)
            for g in range(G_A):
                o_ref[rsl, (kv * G_A + g) * HEAD_DIM:(kv * G_A + g + 1) * HEAD_DIM] = o[g * ds:(g + 1) * ds]
    for b in range(_SWA_SB):
        rsl = slice(b * ds, (b + 1) * ds)
        wk_ref[b, :win - ds, :] = bk_ref[b, ds:, :]
        wk_ref[b, win - ds:, :] = kn_ref[rsl, :]
        wv_ref[b, :win - ds, :] = bv_ref[b, ds:, :]
        wv_ref[b, win - ds:, :] = vn_ref[rsl, :]


def swa_sample(proj, row0, cache_k, cache_v, rel_bias, sinks, db, ds):
    win = cache_k.shape[1]
    rb = _SWA_SB * ds
    r0 = row0 // rb
    bucket = _t5_bucket(win + jnp.arange(ds)[:, None] - jnp.arange(win + ds)[None, :]).astype(jnp.int32)
    smem = pl.BlockSpec(memory_space=pltpu.SMEM)
    return pl.pallas_call(
        functools.partial(_swa_sample_kernel, ds=ds, win=win),
        grid=(db // _SWA_SB,),
        in_specs=[
            pl.BlockSpec((rb, D_A), lambda i: (r0 + i, 0)),
            pl.BlockSpec((rb, KVD_A), lambda i: (r0 + i, E_KA // KVD_A)),
            pl.BlockSpec((rb, KVD_A), lambda i: (r0 + i, E_VA // KVD_A)),
            pl.BlockSpec((_SWA_SB, win, KVD_A), lambda i: (i, 0, 0)),
            pl.BlockSpec((_SWA_SB, win, KVD_A), lambda i: (i, 0, 0)),
            pl.BlockSpec((ds, win + ds), lambda i: (0, 0)),
            smem, smem,
        ],
        out_specs=[
            pl.BlockSpec((rb, D_A), lambda i: (i, 0)),
            pl.BlockSpec((_SWA_SB, win, KVD_A), lambda i: (i, 0, 0)),
            pl.BlockSpec((_SWA_SB, win, KVD_A), lambda i: (i, 0, 0)),
        ],
        out_shape=[jax.ShapeDtypeStruct((db * ds, D_A), F32),
                   jax.ShapeDtypeStruct(cache_k.shape, F32),
                   jax.ShapeDtypeStruct(cache_v.shape, F32)],
        scratch_shapes=[pltpu.VMEM((N_KV_A, G_A * ds, win + ds), F32)],
        compiler_params=_cparams(1),
        name="swa_sample",
    )(proj, proj, proj, cache_k, cache_v, bucket, rel_bias, sinks)


_HG_CHUNK = 128
_HG_LEVELS = int(math.log2(_HG_CHUNK))


def _hgrn_constants(c):
    levels = int(math.log2(c))
    t = np.arange(c)
    mats = [(t[None, :] <= t[:, None])]
    lvl = np.full((c, c), -1, np.int32)
    lvl[t, t] = levels
    for l in range(levels):
        m = c >> (l + 1)
        pivot = (t // (2 * m)) * (2 * m) + m
        mats.append(t[None, :] <= pivot[:, None])
        same = (t[:, None] // (2 * m)) == (t[None, :] // (2 * m))
        pair = same & ((t[:, None] % (2 * m)) >= m) & ((t[None, :] % (2 * m)) < m)
        lvl[pair] = l
    return np.concatenate(mats, axis=0).astype(np.float32), lvl


def _column_of(row):
    n = row.shape[1]
    r = lax.broadcasted_iota(jnp.int32, (n, n), 0)
    c = lax.broadcasted_iota(jnp.int32, (n, n), 1)
    return jnp.sum(jnp.where(r == c, jnp.broadcast_to(row, (n, n)), 0.0), axis=1, keepdims=True)


def _hgrn_prompt_kernel(q_ref, f_ref, v_ref, cm_ref, lvl_ref, o_ref, s_out_ref, s_ref):
    c = _HG_CHUNK
    ci = pl.program_id(2)

    @pl.when(ci == 0)
    def _():
        s_ref[...] = jnp.zeros_like(s_ref)

    lvl = lvl_ref[...]
    sums_all = _dot_exact01(cm_ref[...], jnp.log(f_ref[...]))
    for h in range(_HG_HP):
        hs = slice(h * HEAD_DIM, (h + 1) * HEAD_DIM)
        q = q_ref[:, hs]
        v = v_ref[:, hs].astype(BF16)
        k = 1.0 - f_ref[:, hs]
        sums = sums_all[:, hs]
        b = sums[:c]
        a = jnp.where(lvl == _HG_LEVELS, _dot_nt(q.astype(BF16), k.astype(BF16)), 0.0)
        for l in range(_HG_LEVELS):
            e = jnp.exp(-jnp.abs(b - sums[(l + 1) * c:(l + 2) * c]))
            a = a + jnp.where(lvl == l, _dot_nt((q * e).astype(BF16), (k * e).astype(BF16)), 0.0)
        s0 = s_ref[h]
        o_ref[:, hs] = _dot((q * jnp.exp(b)).astype(BF16), s0.astype(BF16)) + _dot(a.astype(BF16), v)
        b_last = b[c - 1:c, :]
        khat = (k * jnp.exp(b_last - b)).astype(BF16)
        s_new = _column_of(jnp.exp(b_last)) * s0 + _dot_tn(khat, v)
        s_ref[h] = s_new

        @pl.when(ci == pl.num_programs(2) - 1)
        def _():
            s_out_ref[0, h] = s_new


_HG_HP = 4


def hgrn_prompt(proj, batch, seq):
    c = _HG_CHUNK
    nc = seq // c
    hp = _HG_HP
    cmat, lvl = _hgrn_constants(c)
    w = hp * HEAD_DIM
    qc, fc, vc = E_QB // w, E_F // w, E_VB // w
    return pl.pallas_call(
        _hgrn_prompt_kernel,
        grid=(batch, N_HEADS_B // hp, nc),
        in_specs=[
            pl.BlockSpec((c, w), lambda b, h, i: (b * nc + i, qc + h)),
            pl.BlockSpec((c, w), lambda b, h, i: (b * nc + i, fc + h)),
            pl.BlockSpec((c, w), lambda b, h, i: (b * nc + i, vc + h)),
            pl.BlockSpec(cmat.shape, lambda b, h, i: (0, 0)),
            pl.BlockSpec(lvl.shape, lambda b, h, i: (0, 0)),
        ],
        out_specs=[
            pl.BlockSpec((c, w), lambda b, h, i: (b * nc + i, h)),
            pl.BlockSpec((1, hp, HEAD_DIM, HEAD_DIM), lambda b, h, i: (b, h, 0, 0)),
        ],
        out_shape=[jax.ShapeDtypeStruct((batch * seq, D_B), F32),
                   jax.ShapeDtypeStruct((batch, N_HEADS_B, HEAD_DIM, HEAD_DIM), F32)],
        scratch_shapes=[pltpu.VMEM((hp, HEAD_DIM, HEAD_DIM), F32)],
        compiler_params=_cparams(3),
        name="hgrn_prompt",
    )(proj, proj, proj, jnp.asarray(cmat), jnp.asarray(lvl))


_HG_SB = 8


def _hgrn_sample_kernel(q_ref, f_ref, v_ref, st_ref, o_ref, s_out_ref, *, ds):
    s_idx = lax.broadcasted_iota(jnp.int32, (ds, 1), 0)
    for sb in range(_HG_SB):
        rsl = slice(sb * ds, (sb + 1) * ds)
        q = q_ref[rsl, :]
        f = f_ref[rsl, :]
        v = v_ref[rsl, :]
        k = 1.0 - f
        lf = jnp.log(f)
        rows = [lf[0:1]]
        for t in range(1, ds):
            rows.append(rows[-1] + lf[t:t + 1])
        b = jnp.concatenate(rows, axis=0)
        s0 = st_ref[sb, 0]
        o_inter = _dot((q * jnp.exp(b)).astype(BF16), s0.astype(BF16))
        outs = []
        for t in range(ds):
            w = q[t:t + 1] * k * jnp.exp(jnp.minimum(b[t:t + 1] - b, 0.0))
            a_t = jnp.where(s_idx <= t, jnp.sum(w, axis=-1, keepdims=True), 0.0)
            outs.append(jnp.sum(a_t * v, axis=0, keepdims=True))
        o_ref[rsl, :] = o_inter + jnp.concatenate(outs, axis=0)
        b_last = b[ds - 1:ds, :]
        khat = (k * jnp.exp(b_last - b)).astype(BF16)
        s_out_ref[sb, 0] = _column_of(jnp.exp(b_last)) * s0 + _dot_tn(khat, v.astype(BF16))


def hgrn_sample(proj, row0, state, db, ds):
    rb = _HG_SB * ds
    r0 = row0 // rb
    qc, fc, vc = E_QB // HEAD_DIM, E_F // HEAD_DIM, E_VB // HEAD_DIM
    return pl.pallas_call(
        functools.partial(_hgrn_sample_kernel, ds=ds),
        grid=(db // _HG_SB, N_HEADS_B),
        in_specs=[
            pl.BlockSpec((rb, HEAD_DIM), lambda i, h: (r0 + i, qc + h)),
            pl.BlockSpec((rb, HEAD_DIM), lambda i, h: (r0 + i, fc + h)),
            pl.BlockSpec((rb, HEAD_DIM), lambda i, h: (r0 + i, vc + h)),
            pl.BlockSpec((_HG_SB, 1, HEAD_DIM, HEAD_DIM), lambda i, h: (i, h, 0, 0)),
        ],
        out_specs=[
            pl.BlockSpec((rb, HEAD_DIM), lambda i, h: (i, h)),
            pl.BlockSpec((_HG_SB, 1, HEAD_DIM, HEAD_DIM), lambda i, h: (i, h, 0, 0)),
        ],
        out_shape=[jax.ShapeDtypeStruct((db * ds, D_B), F32),
                   jax.ShapeDtypeStruct(state.shape, F32)],
        compiler_params=_cparams(2),
        name="hgrn_sample",
    )(proj, proj, proj, state)


_CS_BLK = 128


def _split3(x):
    x1 = x.astype(BF16).astype(F32)
    r = x - x1
    x2 = r.astype(BF16).astype(F32)
    return x1, x2, r - x2


_AUG_ONES = 3 * N_HEADS_C
_INV_SCALE = HEAD_DIM ** 0.5
_LOG2E = 1.4426950408889634


def _cumsum_kernel(lf_ref, tri_ref, place_ref, c_ref, aug_ref):
    nblk = lf_ref.shape[1] // _CS_BLK
    lane = lax.broadcasted_iota(jnp.int32, (_CS_BLK, HEAD_DIM), 1)
    ones = jnp.where((lane >= _AUG_ONES) & (lane < _AUG_ONES + 3), 1.0, 0.0)

    def body(i, carry):
        r = pl.ds(pl.multiple_of(i * _CS_BLK, _CS_BLK), _CS_BLK)
        c = _dot_exact01(tri_ref[...], lf_ref[0, r, :]) + carry
        c_ref[0, r, :] = c
        aug = ones
        for j, piece in enumerate(_split3(c * _INV_SCALE)):
            aug = aug + _dot(piece.astype(BF16), place_ref[j].astype(BF16))
        aug_ref[0, r, :] = aug.astype(BF16)
        return c[_CS_BLK - 1:_CS_BLK, :]

    lax.fori_loop(0, nblk, body, jnp.zeros((1, lf_ref.shape[2]), F32))


def cumsum_rows(lf):
    b, s, h = lf.shape
    t = np.arange(_CS_BLK)
    tri = jnp.asarray((t[None, :] <= t[:, None]).astype(np.float32))
    place = np.zeros((3, h, HEAD_DIM), np.float32)
    for j in range(3):
        place[j, np.arange(h), j * h + np.arange(h)] = 1.0
    return pl.pallas_call(
        _cumsum_kernel,
        grid=(b,),
        in_specs=[pl.BlockSpec((1, s, h), lambda i: (i, 0, 0)),
                  pl.BlockSpec((_CS_BLK, _CS_BLK), lambda i: (0, 0)),
                  pl.BlockSpec((3, h, HEAD_DIM), lambda i: (0, 0, 0))],
        out_specs=[pl.BlockSpec((1, s, h), lambda i: (i, 0, 0)),
                   pl.BlockSpec((1, s, HEAD_DIM), lambda i: (i, 0, 0))],
        out_shape=[jax.ShapeDtypeStruct(lf.shape, F32),
                   jax.ShapeDtypeStruct((b, s, HEAD_DIM), BF16)],
        compiler_params=_cparams(1),
        name="fox_cumsum",
    )(lf, tri, jnp.asarray(place))


_FOX_TQ = 512
_FOX_TK = 512


_ONES_ROWS = 8


def _fox_prompt_kernel(qi_ref, ki_ref, q_ref, k_ref, vt_ref, cq_ref, ca_ref, o_ref, qa_ref, m_ref, acc_ref):
    kv = pl.program_id(1)
    qi = qi_ref[pl.program_id(2)]
    ki = ki_ref[pl.program_id(2)]
    tq, tk = q_ref.shape[0], k_ref.shape[0]

    @pl.when(ki == 0)
    def _():
        m_ref[...] = jnp.full_like(m_ref, NEG)
        acc_ref[...] = jnp.zeros_like(acc_ref)
        lane = lax.broadcasted_iota(jnp.int32, (tq, HEAD_DIM), 1)
        for g in range(G_C):
            h = kv * G_C + g
            own = (lane == h) | (lane == N_HEADS_C + h) | (lane == 2 * N_HEADS_C + h)
            extra = jnp.where(own, -1.0, 0.0)
            for j, piece in enumerate(_split3(cq_ref[0, 0, :, g:g + 1] * _INV_SCALE)):
                extra = jnp.where(lane == _AUG_ONES + j, piece, extra)
            qa_ref[g, :, :HEAD_DIM] = q_ref[:, g * HEAD_DIM:(g + 1) * HEAD_DIM].astype(BF16)
            qa_ref[g, :, HEAD_DIM:] = extra.astype(BF16)

    def update(masked):
        kaug = jnp.concatenate([k_ref[...].astype(BF16), ca_ref[0]], axis=1)
        vt = jnp.concatenate([vt_ref[0, 0].astype(BF16), jnp.ones((_ONES_ROWS, tk), BF16)],
                             axis=0)
        if masked:
            causal = (lax.broadcasted_iota(jnp.int32, (tk, tq), 0)
                      <= lax.broadcasted_iota(jnp.int32, (tk, tq), 1))
        for g in range(G_C):
            s = _dot_nt(kaug, qa_ref[g]) * (SCALE * _LOG2E)
            if masked:
                s = jnp.where(causal, s, NEG)
            m_prev = m_ref[g]
            m_new = jnp.maximum(m_prev, jnp.max(s, axis=0, keepdims=True))
            alpha = jnp.exp2(m_prev - m_new)
            p = jnp.exp2(s - m_new)
            acc_ref[g] = alpha * acc_ref[g] + _dot(vt, p.astype(BF16))
            m_ref[g] = m_new

    @pl.when(ki < qi)
    def _():
        update(False)

    @pl.when(ki == qi)
    def _():
        update(True)

    @pl.when(ki == qi)
    def _():
        for g in range(G_C):
            acc = acc_ref[g]
            o_ref[:, g * HEAD_DIM:(g + 1) * HEAD_DIM] = jnp.transpose(
                acc[:HEAD_DIM] / acc[HEAD_DIM:HEAD_DIM + 1])


def fox_prompt(projo, c, c_aug, batch, seq):
    tq = tk = min(_FOX_TQ, seq)
    nq, nk = seq // tq, seq // tk
    c_q = jnp.transpose(c.reshape(batch, seq, N_KV_C, G_C), (0, 2, 1, 3))
    v_t = jnp.transpose(projo[:batch * seq, O_V:O_V + KVD_C].reshape(batch, seq, N_KV_C, HEAD_DIM),
                        (0, 2, 3, 1))
    kcol = O_K // HEAD_DIM
    pairs = [(i, j) for i in range(nq) for j in range(i + 1)]
    qi_tab = jnp.asarray([p[0] for p in pairs], jnp.int32)
    ki_tab = jnp.asarray([p[1] for p in pairs], jnp.int32)
    return pl.pallas_call(
        _fox_prompt_kernel,
        grid_spec=pltpu.PrefetchScalarGridSpec(
            num_scalar_prefetch=2,
            grid=(batch, N_KV_C, len(pairs)),
            in_specs=[
                pl.BlockSpec((tq, G_C * HEAD_DIM), lambda b, k, p, qi, ki: (b * nq + qi[p], k)),
                pl.BlockSpec((tk, HEAD_DIM), lambda b, k, p, qi, ki: (b * nk + ki[p], kcol + k)),
                pl.BlockSpec((1, 1, HEAD_DIM, tk), lambda b, k, p, qi, ki: (b, k, 0, ki[p])),
                pl.BlockSpec((1, 1, tq, G_C), lambda b, k, p, qi, ki: (b, k, qi[p], 0)),
                pl.BlockSpec((1, tk, HEAD_DIM), lambda b, k, p, qi, ki: (b, ki[p], 0)),
            ],
            out_specs=pl.BlockSpec((tq, G_C * HEAD_DIM), lambda b, k, p, qi, ki: (b * nq + qi[p], k)),
            scratch_shapes=[pltpu.VMEM((G_C, tq, 2 * HEAD_DIM), BF16),
                            pltpu.VMEM((G_C, 1, tq), F32),
                            pltpu.VMEM((G_C, HEAD_DIM + _ONES_ROWS, tq), F32)],
        ),
        out_shape=jax.ShapeDtypeStruct((batch * seq, D_C), F32),
        compiler_params=_cparams(3),
        name="fox_prompt",
    )(qi_tab, ki_tab, projo, projo, v_t, c_q, c_aug)


_FOX_PP = 16
_FOX_NSEQ = 1


def _fox_sample_kernel(pt_ref, q_ref, kn_ref, vn_ref, lfn_ref, tri_ref, ut1_ref, *refs, ds, page):
    del pt_ref
    pp, nseq = _FOX_PP, _FOX_NSEQ
    n_pg = nseq * pp
    k_refs, v_refs, lf_refs = refs[:n_pg], refs[n_pg:2 * n_pg], refs[2 * n_pg:3 * n_pg]
    o_ref, qb_ref, cq_ref, m_ref, l_ref, acc_ref, carry_ref = refs[3 * n_pg:]
    nrow = N_HEADS_C * ds
    rows_kv = G_C * ds
    step = pl.program_id(1)

    @pl.when(step == 0)
    def _():
        t_row = lax.broadcasted_iota(jnp.int32, (nrow, ds), 0) % ds
        s_col = lax.broadcasted_iota(jnp.int32, (nrow, ds), 1)
        for j in range(nseq):
            qb = q_ref[j].astype(BF16)
            qb_ref[j] = qb
            cn = _dot_exact01_r(lfn_ref[j], tri_ref[...])
            cq = jnp.sum(jnp.where(s_col == t_row, cn, 0.0), axis=-1, keepdims=True)
            cq_ref[j] = cq
            tok = slice(j * ds, (j + 1) * ds)
            kv_cols = [slice(kv * HEAD_DIM, (kv + 1) * HEAD_DIM) for kv in range(N_KV_C)]
            qk = jnp.concatenate(
                [_dot_nt(qb[kv * rows_kv:(kv + 1) * rows_kv], kn_ref[tok, kv_cols[kv]].astype(BF16))
                 for kv in range(N_KV_C)], axis=0)
            s = jnp.where(s_col <= t_row, qk * SCALE + cq - cn, NEG)
            m = jnp.max(s, axis=-1, keepdims=True)
            e = jnp.exp(s - m)
            m_ref[j] = m
            l_ref[j] = jnp.sum(e, axis=-1, keepdims=True)
            eb = e.astype(BF16)
            acc_ref[j] = jnp.concatenate(
                [_dot(eb[kv * rows_kv:(kv + 1) * rows_kv], vn_ref[tok, kv_cols[kv]].astype(BF16))
                 for kv in range(N_KV_C)], axis=0)
        carry_ref[...] = jnp.zeros_like(carry_ref)

    for j in range(nseq):
        kj, vj, lfj = (r[j * pp:(j + 1) * pp] for r in (k_refs, v_refs, lf_refs))
        lf_stack = jnp.concatenate([lfj[i][0] for i in range(pp)], axis=0)
        sums = _dot_exact01_r(lf_stack, ut1_ref[...])
        carry = carry_ref[j]
        dex = []
        for i in range(pp):
            hs = slice(i * N_HEADS_C, (i + 1) * N_HEADS_C)
            dex.append(sums[hs, :page] + carry)
            carry = carry + sums[hs, page:]
        carry_ref[j] = carry

        def kv_rows(refs_, kv):
            strided = pl.ds(kv, page, stride=N_KV_C)
            return jnp.concatenate([refs_[i][0, strided, :] for i in range(pp)], axis=0).astype(BF16)

        qk = jnp.concatenate(
            [_dot_nt(qb_ref[j, kv * rows_kv:(kv + 1) * rows_kv, :], kv_rows(kj, kv))
             for kv in range(N_KV_C)], axis=0)
        bias = jnp.concatenate(
            [jnp.broadcast_to(d[:, None, :], (N_HEADS_C, ds, page)).reshape(nrow, page) for d in dex], axis=1)
        s = qk * SCALE + cq_ref[j] + bias
        m_prev = m_ref[j]
        m_new = jnp.maximum(m_prev, jnp.max(s, axis=-1, keepdims=True))
        alpha = jnp.exp(m_prev - m_new)
        p = jnp.exp(s - m_new)
        l_ref[j] = alpha * l_ref[j] + jnp.sum(p, axis=-1, keepdims=True)
        m_ref[j] = m_new
        pb = p.astype(BF16)
        pv = jnp.concatenate(
            [_dot(pb[kv * rows_kv:(kv + 1) * rows_kv, :], kv_rows(vj, kv)) for kv in range(N_KV_C)], axis=0)
        acc_ref[j] = alpha * acc_ref[j] + pv

    @pl.when(step == pl.num_programs(1) - 1)
    def _():
        for j in range(nseq):
            o_ref[j] = acc_ref[j] / l_ref[j]


def fox_sample(projo, row0, lf_s, cache_k, cache_v, cache_lf, page_table, db, ds):
    pool, page = cache_lf.shape[0], cache_lf.shape[1]
    n_pages = page_table.shape[1]
    pp = _FOX_PP
    nrow = N_HEADS_C * ds
    q = projo[row0:row0 + db * ds, O_Q:O_Q + D_C].reshape(db, ds, N_HEADS_C, HEAD_DIM)
    q = jnp.transpose(q, (0, 2, 1, 3)).reshape(db, nrow, HEAD_DIM)
    lfn = jnp.transpose(lf_s.reshape(db, ds, N_HEADS_C), (0, 2, 1))
    lfn = jnp.broadcast_to(lfn[:, :, None, :], (db, N_HEADS_C, ds, ds)).reshape(db, nrow, ds)
    cache_lft = jnp.transpose(cache_lf, (0, 2, 1))
    t = np.arange(ds)
    tri = jnp.asarray((t[:, None] <= t[None, :]).astype(np.float32))
    u = np.arange(page)
    ut1 = jnp.asarray(np.concatenate([(u[:, None] > u[None, :]), np.ones((page, page), bool)],
                                     axis=1).astype(np.float32))
    nseq = _FOX_NSEQ
    r0 = row0 // (nseq * ds)

    def page_map(j, i):
        return lambda b, p, pt: (pt[b * nseq + j, n_pages - 1 - (p * pp + i)], 0, 0)

    pages = [(j, i) for j in range(nseq) for i in range(pp)]
    in_specs = [
        pl.BlockSpec((nseq, nrow, HEAD_DIM), lambda b, p, pt: (b, 0, 0)),
        pl.BlockSpec((nseq * ds, KVD_C), lambda b, p, pt: (r0 + b, O_K // KVD_C)),
        pl.BlockSpec((nseq * ds, KVD_C), lambda b, p, pt: (r0 + b, O_V // KVD_C)),
        pl.BlockSpec((nseq, nrow, ds), lambda b, p, pt: (b, 0, 0)),
        pl.BlockSpec((ds, ds), lambda b, p, pt: (0, 0)),
        pl.BlockSpec((page, 2 * page), lambda b, p, pt: (0, 0)),
    ]
    in_specs += [pl.BlockSpec((1, page * N_KV_C, HEAD_DIM), page_map(j, i)) for j, i in pages]
    in_specs += [pl.BlockSpec((1, page * N_KV_C, HEAD_DIM), page_map(j, i)) for j, i in pages]
    in_specs += [pl.BlockSpec((1, N_HEADS_C, page), page_map(j, i)) for j, i in pages]
    out = pl.pallas_call(
        functools.partial(_fox_sample_kernel, ds=ds, page=page),
        grid_spec=pltpu.PrefetchScalarGridSpec(
            num_scalar_prefetch=1,
            grid=(db // nseq, n_pages // pp),
            in_specs=in_specs,
            out_specs=pl.BlockSpec((nseq, nrow, HEAD_DIM), lambda b, p, pt: (b, 0, 0)),
            scratch_shapes=[pltpu.VMEM((nseq, nrow, HEAD_DIM), BF16), pltpu.VMEM((nseq, nrow, 1), F32),
                            pltpu.VMEM((nseq, nrow, 1), F32), pltpu.VMEM((nseq, nrow, 1), F32),
                            pltpu.VMEM((nseq, nrow, HEAD_DIM), F32),
                            pltpu.VMEM((nseq, N_HEADS_C, page), F32)],
        ),
        out_shape=jax.ShapeDtypeStruct((db, nrow, HEAD_DIM), F32),
        compiler_params=_cparams(2),
        name="fox_sample",
    )(page_table, q, projo, projo, lfn, tri, ut1, *([cache_k] * len(pages)), *([cache_v] * len(pages)),
      *([cache_lft] * len(pages)))
    out = jnp.transpose(out.reshape(db, N_HEADS_C, ds, HEAD_DIM), (0, 2, 1, 3))
    return out.reshape(db * ds, D_C)


def _pack_bf16_pairs(h):
    c = h.shape[1] // 2
    hb = h.astype(BF16).astype(F32)
    bits = lax.bitcast_convert_type(hb, jnp.uint32)
    return (bits[:, :c] & jnp.uint32(0xFFFF0000)) | (bits[:, c:] >> 16)


def _unpack_bf16_pairs(u):
    hi = lax.bitcast_convert_type(u & jnp.uint32(0xFFFF0000), F32).astype(BF16)
    lo = lax.bitcast_convert_type(u << 16, F32).astype(BF16)
    return jnp.concatenate([hi, lo], axis=1)


def _moe_route_kernel(x_ref, g_ref, r_ref, h_ref, idx_ref, w_ref):
    hn = _rms(x_ref[...], g_ref[...])
    h_ref[...] = _pack_bf16_pairs(hn)
    logits = jnp.dot(hn, r_ref[...], preferred_element_type=F32, precision=lax.Precision.HIGHEST)
    lane = lax.broadcasted_iota(jnp.int32, logits.shape, 1)
    m1 = jnp.max(logits, axis=-1, keepdims=True)
    i1 = jnp.min(jnp.where(logits == m1, lane, N_EXPERTS), axis=-1, keepdims=True)
    rest = jnp.where(lane == i1, -jnp.inf, logits)
    m2 = jnp.max(rest, axis=-1, keepdims=True)
    i2 = jnp.min(jnp.where(rest == m2, lane, N_EXPERTS), axis=-1, keepdims=True)
    e = jnp.exp(m2 - m1)
    slot = lax.broadcasted_iota(jnp.int32, idx_ref.shape, 1)
    idx_ref[...] = jnp.where(slot == 0, i1, i2)
    w_ref[...] = jnp.where(slot == 0, 1.0 / (1.0 + e), e / (1.0 + e))


def moe_route(x, g, router, tm):
    n, d = x.shape
    return pl.pallas_call(
        _moe_route_kernel,
        grid=(n // tm,),
        in_specs=[pl.BlockSpec((tm, d), lambda i: (i, 0)),
                  pl.BlockSpec((1, d), lambda i: (0, 0)),
                  pl.BlockSpec((d, N_EXPERTS), lambda i: (0, 0))],
        out_specs=[pl.BlockSpec((tm, d // 2), lambda i: (i, 0)),
                   pl.BlockSpec((tm, TOP_K), lambda i: (i, 0)),
                   pl.BlockSpec((tm, TOP_K), lambda i: (i, 0))],
        out_shape=[jax.ShapeDtypeStruct((n, d // 2), jnp.uint32),
                   jax.ShapeDtypeStruct((n, TOP_K), jnp.int32),
                   jax.ShapeDtypeStruct((n, TOP_K), F32)],
        compiler_params=_cparams(1),
        name="moe_route",
    )(x, g.reshape(1, d), router)


def _route_plan(idx, w, tm):
    n = idx.shape[0]
    flat_e = idx.reshape(-1)
    onehot = (flat_e[:, None] == jnp.arange(N_EXPERTS, dtype=jnp.int32)[None, :]).astype(jnp.int32)
    csum = jnp.cumsum(onehot, axis=0)
    rank = jnp.take_along_axis(csum, flat_e[:, None], axis=1)[:, 0] - 1
    count = csum[-1]
    tiles_e = (count + tm - 1) // tm
    tile_end = jnp.cumsum(tiles_e)
    tile_start = tile_end - tiles_e
    pos = tile_start[flat_e] * tm + rank
    n_tiles = (TOP_K * n) // tm + N_EXPERTS
    tile = jnp.arange(n_tiles, dtype=jnp.int32)
    tile_expert = jnp.minimum(jnp.searchsorted(tile_end, tile, side="right"), N_EXPERTS - 1).astype(jnp.int32)
    tile_valid = jnp.clip(count[tile_expert] - (tile - tile_start[tile_expert]) * tm, 0, tm)
    tile_valid = jnp.where(tile < tile_end[-1], tile_valid, 0).astype(jnp.int32)
    pair = jnp.zeros((n_tiles * tm,), jnp.int32).at[pos].set(jnp.arange(TOP_K * n, dtype=jnp.int32))
    row_in_tile = jnp.arange(n_tiles * tm, dtype=jnp.int32) % tm
    gate = jnp.where(row_in_tile < jnp.repeat(tile_valid, tm), jnp.take(w.reshape(-1), pair, mode="clip"), 0.0)
    return pos.reshape(n, TOP_K), pair // TOP_K, gate.reshape(-1, 1), tile_expert, tile_valid


_MOE_CHUNKS = 1


def moe_layer(x, g, router, w_gate, w_up, w_down, tm_route, n_split):
    tm = min(_FFN_TM, x.shape[0])
    h, idx, w = moe_route(x, g, router, tm_route)
    pos, token, gate, tile_expert, tile_valid = _route_plan(idx, w, tm)
    tiles = token.shape[0] // tm
    n_chunks = next(c for c in (_MOE_CHUNKS, 2, 1) if tiles % c == 0)
    tiles_c = tiles // n_chunks
    act = None
    for c in range(n_chunks):
        ts = slice(c * tiles_c, (c + 1) * tiles_c)
        h_sorted = jnp.take(h, token[c * tiles_c * tm:(c + 1) * tiles_c * tm], axis=0, mode="clip")
        act = ffn_up(h_sorted, tile_expert[ts], tile_valid[ts], w_gate, w_up,
                     out=act, tile0=c * tiles_c, total_tiles=tiles)
    y_sorted = ffn_down(act, tile_expert, tile_valid, w_down, gate=gate)
    y0 = jnp.take(y_sorted, pos[:, 0], axis=0, mode="clip")
    y1 = jnp.take(y_sorted, pos[:, 1], axis=0, mode="clip")
    return tuple(x[r] + y0[r] + y1[r] for r in (slice(0, n_split), slice(n_split, None)))


def _row_tile(n, want):
    t = min(want, n)
    while n % t:
        t //= 2
    return t


def kernel(x_prompt, x_sample, cache_k_win, cache_v_win, state_hgrn, cache_k_fox, cache_v_fox, cache_logf_fox, page_table, rel_bias, norm_mix_e, w_in_e, q_norm_a, k_norm_a, sinks_a, lb_b, o_norm_b, w_out_e, norm_ffn_e, w_gate_e, w_up_e, w_down_e, norm_mix_o, w_in_o, fgate_bias_c, q_norm_c, k_norm_c, w_out_o, norm_ffn_o, router_o, w_gate_x, w_up_x, w_down_x):
    batch, seq, d = x_prompt.shape
    db, ds, _ = x_sample.shape
    n_p, n_s = batch * seq, db * ds
    n = n_p + n_s
    win = cache_k_win.shape[2]
    pool, page = cache_k_fox.shape[1], cache_k_fox.shape[2]
    tm_big = _row_tile(math.gcd(n_p, n_s), 1024)

    x = jnp.concatenate([x_prompt.reshape(n_p, d), x_sample.reshape(n_s, d)], axis=0)

    proj = even_project(x, norm_mix_e[0], w_in_e[0], q_norm_a[0], k_norm_a[0], lb_b, tm_big)
    oa_p = swa_prompt(proj, rel_bias, sinks_a[0], batch, seq)
    oa_s, wk_s, wv_s = swa_sample(proj, n_p, cache_k_win[0].reshape(db, win, KVD_A),
                                  cache_v_win[0].reshape(db, win, KVD_A), rel_bias, sinks_a[0], db, ds)
    ob_p, hg_p = hgrn_prompt(proj, batch, seq)
    ob_s, hg_s = hgrn_sample(proj, n_p, state_hgrn[0], db, ds)
    y = even_merge(oa_p, oa_s, ob_p, ob_s, proj, o_norm_b[0], w_out_e[0], x, tm_big // 2)
    y = dense_ffn(y, norm_ffn_e[0], w_gate_e[0], w_up_e[0], w_down_e[0])

    projo, lf = odd_project(y, norm_mix_o[0], w_in_o[0], q_norm_c[0], k_norm_c[0], fgate_bias_c[0], tm_big)
    c, c_aug = cumsum_rows(lf[:n_p].reshape(batch, seq, N_HEADS_C))
    oc_p = fox_prompt(projo, c, c_aug, batch, seq)
    oc_s = fox_sample(projo, n_p, lf[n_p:], cache_k_fox[0].reshape(pool, page * N_KV_C, HEAD_DIM),
                      cache_v_fox[0].reshape(pool, page * N_KV_C, HEAD_DIM), cache_logf_fox[0],
                      page_table, db, ds)
    y = out_project(oc_p, oc_s, w_out_o[0], y, tm_big)
    y_p, y_s = moe_layer(y, norm_ffn_o[0], router_o[0], w_gate_x[0], w_up_x[0], w_down_x[0], tm_big, n_p)

    def prompt_tail(col, width, heads):
        t = proj[:n_p, col:col + width].reshape(batch, seq, heads, HEAD_DIM)
        return t[:, seq - WINDOW:][None]

    return (
        y_p.reshape(batch, seq, d),
        y_s.reshape(db, ds, d),
        prompt_tail(E_KA, KVD_A, N_KV_A),
        prompt_tail(E_VA, KVD_A, N_KV_A),
        wk_s.reshape(1, db, win, N_KV_A, HEAD_DIM),
        wv_s.reshape(1, db, win, N_KV_A, HEAD_DIM),
        hg_p[None],
        hg_s[None],
        projo[:n_p, O_K:O_K + KVD_C].reshape(1, batch, seq, N_KV_C, HEAD_DIM),
        projo[:n_p, O_V:O_V + KVD_C].reshape(1, batch, seq, N_KV_C, HEAD_DIM),
        lf[:n_p].reshape(1, batch, seq, N_HEADS_C),
        projo[n_p:, O_K:O_K + KVD_C].reshape(1, db, ds, N_KV_C, HEAD_DIM),
        projo[n_p:, O_V:O_V + KVD_C].reshape(1, db, ds, N_KV_C, HEAD_DIM),
        lf[n_p:].reshape(1, db, ds, N_HEADS_C),
    )
```

```python
import functools
import math

import numpy as np
import jax
import jax.numpy as jnp
from jax import lax
from jax.experimental import pallas as pl
from jax.experimental.pallas import tpu as pltpu

F32 = jnp.float32
BF16 = jnp.bfloat16

HEAD_DIM = 128
N_HEADS_A, N_KV_A = 8, 2
G_A = N_HEADS_A // N_KV_A
WINDOW = 128
REL_BUCKETS, REL_MAX_DIST = 32, 128
N_HEADS_B = 8
N_HEADS_C, N_KV_C = 16, 4
G_C = N_HEADS_C // N_KV_C
N_EXPERTS, TOP_K = 8, 2
EPS = 1e-6
SCALE = HEAD_DIM ** -0.5
NEG = -1e30

D_A = N_HEADS_A * HEAD_DIM
KVD_A = N_KV_A * HEAD_DIM
D_B = N_HEADS_B * HEAD_DIM
D_C = N_HEADS_C * HEAD_DIM
KVD_C = N_KV_C * HEAD_DIM

E_QA, E_QB, E_F, E_VB, E_GB = 0, D_A, D_A + D_B, D_A + 2 * D_B, D_A + 3 * D_B
E_KA = D_A + 4 * D_B
E_VA = E_KA + KVD_A
E_OUT = E_VA + KVD_A
O_Q, O_K, O_V = 0, D_C, D_C + KVD_C
O_OUT = D_C + 2 * KVD_C

VMEM_LIMIT = 56 * 1024 * 1024


def _cparams(n_axes):
    return pltpu.CompilerParams(dimension_semantics=("arbitrary",) * n_axes,
                                vmem_limit_bytes=VMEM_LIMIT)


def _rms(x, gain):
    return x * lax.rsqrt(jnp.mean(x * x, axis=-1, keepdims=True) + EPS) * gain


def _silu(x):
    return x * (1.0 / (1.0 + jnp.exp(-x)))


def _dot(a, b):
    return jnp.dot(a, b, preferred_element_type=F32)


def _dot_nt(a, b):
    return lax.dot_general(a, b, (((1,), (1,)), ((), ())), preferred_element_type=F32)


def _dot_tn(a, b):
    return lax.dot_general(a, b, (((0,), (0,)), ((), ())), preferred_element_type=F32)


def _dot_exact01(m01, x):
    x1 = x.astype(BF16)
    r1 = x - x1.astype(F32)
    x2 = r1.astype(BF16)
    x3 = (r1 - x2.astype(F32)).astype(BF16)
    m = m01.astype(BF16)
    return _dot(m, x1) + _dot(m, x2) + _dot(m, x3)


def _dot_exact01_r(x, m01):
    x1 = x.astype(BF16)
    r1 = x - x1.astype(F32)
    x2 = r1.astype(BF16)
    x3 = (r1 - x2.astype(F32)).astype(BF16)
    m = m01.astype(BF16)
    return _dot(x1, m) + _dot(x2, m) + _dot(x3, m)


_TN = 256


def _even_out_tile(j):
    return jnp.where(j < 4, j, jnp.where(j < 6, j + 16, j - 2))


def _even_proj_kernel(x_ref, g_ref, w_ref, qn_ref, kn_ref, lb_ref, o_ref, hn_ref):
    j = pl.program_id(1)

    @pl.when(j == 0)
    def _():
        hn_ref[...] = _rms(x_ref[...], g_ref[...]).astype(BF16)

    acc = _dot(hn_ref[...], w_ref[...].astype(BF16))

    def head_norm(gain):
        for h in range(_TN // HEAD_DIM):
            sl = slice(h * HEAD_DIM, (h + 1) * HEAD_DIM)
            o_ref[:, sl] = _rms(acc[:, sl], gain)

    @pl.when(j < 4)
    def _():
        head_norm(qn_ref[...])

    @pl.when(j == 4)
    def _():
        head_norm(kn_ref[...])

    @pl.when((j == 5) | (j >= 14))
    def _():
        o_ref[...] = acc

    @pl.when((j >= 6) & (j < 10))
    def _():
        o_ref[...] = _silu(acc)

    @pl.when((j >= 10) & (j < 14))
    def _():
        lb = lb_ref[...]
        e = jnp.exp(lb - jnp.max(lb, axis=0, keepdims=True))
        lb0 = e[0:1, :] / jnp.sum(e, axis=0, keepdims=True)
        o_ref[...] = lb0 + (1.0 - lb0) * (1.0 / (1.0 + jnp.exp(-acc)))


def even_project(x, g, w_in, q_norm, k_norm, lb_b, tm):
    n, d = x.shape
    n_tiles = w_in.shape[1] // _TN
    return pl.pallas_call(
        _even_proj_kernel,
        grid=(n // tm, n_tiles),
        in_specs=[
            pl.BlockSpec((tm, d), lambda i, j: (i, 0)),
            pl.BlockSpec((1, d), lambda i, j: (0, 0)),
            pl.BlockSpec((d, _TN), lambda i, j: (0, j)),
            pl.BlockSpec((1, HEAD_DIM), lambda i, j: (0, 0)),
            pl.BlockSpec((1, HEAD_DIM), lambda i, j: (0, 0)),
            pl.BlockSpec((lb_b.shape[0], _TN), lambda i, j: (0, jnp.clip(j - 10, 0, 3))),
        ],
        out_specs=pl.BlockSpec((tm, _TN), lambda i, j: (i, _even_out_tile(j))),
        out_shape=jax.ShapeDtypeStruct((n, E_OUT), F32),
        scratch_shapes=[pltpu.VMEM((tm, d), BF16)],
        compiler_params=_cparams(2),
        name="even_project",
    )(x, g.reshape(1, d), w_in, q_norm.reshape(1, -1), k_norm.reshape(1, -1), lb_b)


def _split_rows_specs(shape_cols, tm, tiles_p):
    return [pl.BlockSpec((tm, shape_cols), lambda i, j: (jnp.minimum(i, tiles_p - 1), 0)),
            pl.BlockSpec((tm, shape_cols), lambda i, j: (jnp.maximum(i - tiles_p, 0), 0))]


def _even_merge_kernel(oap_ref, oas_ref, obp_ref, obs_ref, gb_ref, on_ref, w_ref, x_ref, o_ref, lhs_ref,
                       *, tiles_p):
    i = pl.program_id(0)

    def fill(oa_ref, ob_ref):
        lhs_ref[:, :D_A] = oa_ref[...].astype(BF16)
        for h in range(N_HEADS_B):
            sl = slice(h * HEAD_DIM, (h + 1) * HEAD_DIM)
            obn = _rms(ob_ref[:, sl], on_ref[...]) * _silu(gb_ref[:, sl])
            lhs_ref[:, D_A + h * HEAD_DIM:D_A + (h + 1) * HEAD_DIM] = obn.astype(BF16)

    @pl.when((pl.program_id(1) == 0) & (i < tiles_p))
    def _():
        fill(oap_ref, obp_ref)

    @pl.when((pl.program_id(1) == 0) & (i >= tiles_p))
    def _():
        fill(oas_ref, obs_ref)

    o_ref[...] = x_ref[...] + _dot(lhs_ref[...], w_ref[...].astype(BF16))


def even_merge(oa_p, oa_s, ob_p, ob_s, proj, o_norm, w_out, x, tm, tn=512):
    n, d = x.shape
    tiles_p = oa_p.shape[0] // tm
    return pl.pallas_call(
        functools.partial(_even_merge_kernel, tiles_p=tiles_p),
        grid=(n // tm, d // tn),
        in_specs=_split_rows_specs(D_A, tm, tiles_p) + _split_rows_specs(D_B, tm, tiles_p) + [
            pl.BlockSpec((tm, D_B), lambda i, j: (i, E_GB // D_B)),
            pl.BlockSpec((1, HEAD_DIM), lambda i, j: (0, 0)),
            pl.BlockSpec((D_A + D_B, tn), lambda i, j: (0, j)),
            pl.BlockSpec((tm, tn), lambda i, j: (i, j)),
        ],
        out_specs=pl.BlockSpec((tm, tn), lambda i, j: (i, j)),
        out_shape=jax.ShapeDtypeStruct((n, d), F32),
        scratch_shapes=[pltpu.VMEM((tm, D_A + D_B), BF16)],
        compiler_params=_cparams(2),
        name="even_merge",
    )(oa_p, oa_s, ob_p, ob_s, proj, o_norm.reshape(1, -1), w_out, x)


_FFN_TM = 512
_FFN_TF = 512
_FFN_TN = 512


def _norm_cast_kernel(x_ref, g_ref, o_ref):
    o_ref[...] = _rms(x_ref[...], g_ref[...]).astype(BF16)


def norm_cast(x, g, tm):
    n, d = x.shape
    return pl.pallas_call(
        _norm_cast_kernel,
        grid=(n // tm,),
        in_specs=[pl.BlockSpec((tm, d), lambda i: (i, 0)), pl.BlockSpec((1, d), lambda i: (0, 0))],
        out_specs=pl.BlockSpec((tm, d), lambda i: (i, 0)),
        out_shape=jax.ShapeDtypeStruct((n, d), BF16),
        compiler_params=_cparams(1),
        name="norm_cast",
    )(x, g.reshape(1, d))


def _group_start(te_ref, i):
    return (i == 0) | (te_ref[i] != te_ref[jnp.maximum(i - 1, 0)])


def _by_valid_rows(nv, tm, compute, o_ref):
    half = tm // 2

    @pl.when(nv > half)
    def _():
        o_ref[...] = compute(slice(None))

    @pl.when((nv > 0) & (nv <= half))
    def _():
        o_ref[:half, :] = compute(slice(0, half))
        o_ref[half:, :] = jnp.zeros((tm - half, o_ref.shape[1]), o_ref.dtype)

    @pl.when(nv == 0)
    def _():
        o_ref[...] = jnp.zeros_like(o_ref)


def _ffn_up_kernel(te_ref, nv_ref, x_ref, wg_ref, wu_ref, *rest, packed):
    o_ref, wg_s, wu_s = rest[-3:]
    i = pl.program_id(1)

    @pl.when(_group_start(te_ref, i))
    def _():
        wg_s[...] = wg_ref[0].astype(BF16)
        wu_s[...] = wu_ref[0].astype(BF16)

    def compute(rows):
        x = _unpack_bf16_pairs(x_ref[rows, :]) if packed else x_ref[rows, :]
        return (_silu(_dot(x, wg_s[...])) * _dot(x, wu_s[...])).astype(BF16)

    _by_valid_rows(nv_ref[i], x_ref.shape[0], compute, o_ref)


def ffn_up(x, tile_expert, tile_valid, w_gate, w_up, out=None, tile0=0, total_tiles=None):
    p = x.shape[0]
    d, d_ff = w_gate.shape[1], w_gate.shape[2]
    tm, tf = min(_FFN_TM, p), _FFN_TF
    rows_out = (total_tiles if total_tiles is not None else p // tm) * tm
    in_specs = [
        pl.BlockSpec((tm, x.shape[1]), lambda f, i, te, nv: (i, 0)),
        pl.BlockSpec((1, d, tf), lambda f, i, te, nv: (te[i], 0, f)),
        pl.BlockSpec((1, d, tf), lambda f, i, te, nv: (te[i], 0, f)),
    ]
    args = [x, w_gate, w_up]
    aliases = {}
    if out is not None:
        in_specs.append(pl.BlockSpec(memory_space=pl.ANY))
        args.append(out)
        aliases = {2 + len(args) - 1: 0}
    return pl.pallas_call(
        functools.partial(_ffn_up_kernel, packed=x.dtype == jnp.uint32),
        grid_spec=pltpu.PrefetchScalarGridSpec(
            num_scalar_prefetch=2,
            grid=(d_ff // tf, p // tm),
            in_specs=in_specs,
            out_specs=pl.BlockSpec((tm, tf), lambda f, i, te, nv: (i + tile0, f)),
            scratch_shapes=[pltpu.VMEM((d, tf), BF16), pltpu.VMEM((d, tf), BF16)],
        ),
        out_shape=jax.ShapeDtypeStruct((rows_out, d_ff), BF16),
        input_output_aliases=aliases,
        compiler_params=_cparams(2),
        name="ffn_up",
    )(tile_expert, tile_valid, *args)


def _ffn_down_kernel(te_ref, nv_ref, a_ref, wd_ref, *rest, has_gate, has_res):
    rest = list(rest)
    gate_ref = rest.pop(0) if has_gate else None
    res_ref = rest.pop(0) if has_res else None
    o_ref, wd_s = rest
    i = pl.program_id(1)

    @pl.when(_group_start(te_ref, i))
    def _():
        wd_s[...] = wd_ref[0].astype(BF16)

    def compute(rows):
        y = _dot(a_ref[rows, :], wd_s[...])
        if has_gate:
            y = y * gate_ref[rows, :]
        if has_res:
            y = res_ref[rows, :] + y
        return y

    _by_valid_rows(nv_ref[i], a_ref.shape[0], compute, o_ref)


def ffn_down(act, tile_expert, tile_valid, w_down, gate=None, res=None):
    p, d_ff = act.shape
    d = w_down.shape[2]
    tm, tn = min(_FFN_TM, p), _FFN_TN
    in_specs = [
        pl.BlockSpec((tm, d_ff), lambda n, i, te, nv: (i, 0)),
        pl.BlockSpec((1, d_ff, tn), lambda n, i, te, nv: (te[i], 0, n)),
    ]
    args = [act, w_down]
    if gate is not None:
        in_specs.append(pl.BlockSpec((tm, 1), lambda n, i, te, nv: (i, 0)))
        args.append(gate)
    if res is not None:
        in_specs.append(pl.BlockSpec((tm, tn), lambda n, i, te, nv: (i, n)))
        args.append(res)
    return pl.pallas_call(
        functools.partial(_ffn_down_kernel, has_gate=gate is not None, has_res=res is not None),
        grid_spec=pltpu.PrefetchScalarGridSpec(
            num_scalar_prefetch=2,
            grid=(d // tn, p // tm),
            in_specs=in_specs,
            out_specs=pl.BlockSpec((tm, tn), lambda n, i, te, nv: (i, n)),
            scratch_shapes=[pltpu.VMEM((d_ff, tn), BF16)],
        ),
        out_shape=jax.ShapeDtypeStruct((p, d), F32),
        compiler_params=_cparams(2),
        name="ffn_down",
    )(tile_expert, tile_valid, *args)


def dense_ffn(x, g, w_gate, w_up, w_down):
    n = x.shape[0]
    tm = min(_FFN_TM, n)
    tiles = n // tm
    te = jnp.zeros((tiles,), jnp.int32)
    nv = jnp.full((tiles,), tm, jnp.int32)
    act = ffn_up(norm_cast(x, g, tm), te, nv, w_gate[None], w_up[None])
    return ffn_down(act, te, nv, w_down[None], res=x)


def _odd_proj_kernel(x_ref, g_ref, w_ref, wf_ref, qn_ref, kn_ref, fb_ref, o_ref, lf_ref, hn_ref):
    j = pl.program_id(1)
    nq = D_C // _TN
    nk = KVD_C // _TN

    @pl.when(j == 0)
    def _():
        hn = _rms(x_ref[...], g_ref[...]).astype(BF16)
        hn_ref[...] = hn
        z = _dot(hn, wf_ref[...].astype(BF16)) + fb_ref[...]
        lf_ref[...] = jnp.minimum(z, 0.0) - jnp.log(1.0 + jnp.exp(-jnp.abs(z)))

    acc = _dot(hn_ref[...], w_ref[...].astype(BF16))

    def head_norm(gain):
        for h in range(_TN // HEAD_DIM):
            sl = slice(h * HEAD_DIM, (h + 1) * HEAD_DIM)
            o_ref[:, sl] = _rms(acc[:, sl], gain)

    @pl.when(j < nq)
    def _():
        head_norm(qn_ref[...])

    @pl.when((j >= nq) & (j < nq + nk))
    def _():
        head_norm(kn_ref[...])

    @pl.when(j >= nq + nk)
    def _():
        o_ref[...] = acc


def odd_project(x, g, w_in, q_norm, k_norm, f_bias, tm):
    n, d = x.shape
    w_f = w_in[:, O_OUT:]
    return pl.pallas_call(
        _odd_proj_kernel,
        grid=(n // tm, O_OUT // _TN),
        in_specs=[
            pl.BlockSpec((tm, d), lambda i, j: (i, 0)),
            pl.BlockSpec((1, d), lambda i, j: (0, 0)),
            pl.BlockSpec((d, _TN), lambda i, j: (0, j)),
            pl.BlockSpec((d, N_HEADS_C), lambda i, j: (0, 0)),
            pl.BlockSpec((1, HEAD_DIM), lambda i, j: (0, 0)),
            pl.BlockSpec((1, HEAD_DIM), lambda i, j: (0, 0)),
            pl.BlockSpec((1, N_HEADS_C), lambda i, j: (0, 0)),
        ],
        out_specs=[
            pl.BlockSpec((tm, _TN), lambda i, j: (i, j)),
            pl.BlockSpec((tm, N_HEADS_C), lambda i, j: (i, 0)),
        ],
        out_shape=[jax.ShapeDtypeStruct((n, O_OUT), F32),
                   jax.ShapeDtypeStruct((n, N_HEADS_C), F32)],
        scratch_shapes=[pltpu.VMEM((tm, d), BF16)],
        compiler_params=_cparams(2),
        name="odd_project",
    )(x, g.reshape(1, d), w_in, w_f, q_norm.reshape(1, -1), k_norm.reshape(1, -1),
      f_bias.reshape(1, -1))


def _out_proj_kernel(ap_ref, as_ref, w_ref, x_ref, o_ref, lhs_ref, *, tiles_p):
    i = pl.program_id(0)

    @pl.when((pl.program_id(1) == 0) & (i < tiles_p))
    def _():
        lhs_ref[...] = ap_ref[...].astype(BF16)

    @pl.when((pl.program_id(1) == 0) & (i >= tiles_p))
    def _():
        lhs_ref[...] = as_ref[...].astype(BF16)

    o_ref[...] = x_ref[...] + _dot(lhs_ref[...], w_ref[...].astype(BF16))


def out_project(a_p, a_s, w, x, tm, tn=512):
    n, d = x.shape
    k = a_p.shape[1]
    tiles_p = a_p.shape[0] // tm
    return pl.pallas_call(
        functools.partial(_out_proj_kernel, tiles_p=tiles_p),
        grid=(n // tm, d // tn),
        in_specs=_split_rows_specs(k, tm, tiles_p) + [
            pl.BlockSpec((k, tn), lambda i, j: (0, j)),
            pl.BlockSpec((tm, tn), lambda i, j: (i, j)),
        ],
        out_specs=pl.BlockSpec((tm, tn), lambda i, j: (i, j)),
        out_shape=jax.ShapeDtypeStruct((n, d), F32),
        scratch_shapes=[pltpu.VMEM((tm, k), BF16)],
        compiler_params=_cparams(2),
        name="out_project",
    )(a_p, a_s, w, x)


def _t5_bucket(dist):
    d = jnp.maximum(dist, 0)
    max_exact = REL_BUCKETS // 2
    ratio = jnp.log(jnp.maximum(d, 1).astype(F32) / max_exact) / math.log(REL_MAX_DIST / max_exact)
    large = jnp.minimum(max_exact + (ratio * (REL_BUCKETS - max_exact)).astype(jnp.int32),
                        REL_BUCKETS - 1)
    return jnp.where(d < max_exact, d, large)


def _bias_from_buckets(bucket, rel_ref, h):
    out = jnp.zeros(bucket.shape, F32)
    for b in range(REL_BUCKETS):
        out = jnp.where(bucket == b, rel_ref[b, h], out)
    return out


def _swa_prompt_kernel(q_ref, kp_ref, kc_ref, vp_ref, vc_ref, bkt_ref, rel_ref, sink_ref,
                       o_ref, bias_ref):
    n = pl.program_id(1)
    kv = pl.program_id(2)

    @pl.when((pl.program_id(0) == 0) & (n == 0) & (kv == 0))
    def _():
        for h in range(N_HEADS_A):
            bias_ref[h] = _bias_from_buckets(bkt_ref[...], rel_ref, h)

    kk = jnp.concatenate([kp_ref[...], kc_ref[...]], axis=0).astype(BF16)
    vv = jnp.concatenate([vp_ref[...], vc_ref[...]], axis=0).astype(BF16)
    row = lax.broadcasted_iota(jnp.int32, (WINDOW, 2 * WINDOW), 0)
    col = lax.broadcasted_iota(jnp.int32, (WINDOW, 2 * WINDOW), 1)
    dist = WINDOW + row - col
    valid = (dist >= 0) & (dist < WINDOW) & ((n > 0) | (col >= WINDOW))
    for g in range(G_A):
        h = kv * G_A + g
        sl = slice(g * HEAD_DIM, (g + 1) * HEAD_DIM)
        s = _dot_nt(q_ref[:, sl].astype(BF16), kk) * SCALE + bias_ref[h]
        s = jnp.where(valid, s, NEG)
        sk = sink_ref[h]
        m = jnp.maximum(jnp.max(s, axis=-1, keepdims=True), sk)
        e = jnp.exp(s - m)
        p = e / (jnp.sum(e, axis=-1, keepdims=True) + jnp.exp(sk - m))
        o_ref[:, sl] = _dot(p.astype(BF16), vv)


def swa_prompt(proj, rel_bias, sinks, batch, seq):
    nb = seq // WINDOW
    i = jnp.arange(WINDOW)[:, None]
    j = jnp.arange(2 * WINDOW)[None, :]
    bucket = _t5_bucket(WINDOW + i - j).astype(jnp.int32)
    kcol, vcol = E_KA // HEAD_DIM, E_VA // HEAD_DIM
    smem = pl.BlockSpec(memory_space=pltpu.SMEM)
    return pl.pallas_call(
        _swa_prompt_kernel,
        grid=(batch, nb, N_KV_A),
        in_specs=[
            pl.BlockSpec((WINDOW, G_A * HEAD_DIM), lambda b, n, k: (b * nb + n, k)),
            pl.BlockSpec((WINDOW, HEAD_DIM), lambda b, n, k: (b * nb + jnp.maximum(n - 1, 0), kcol + k)),
            pl.BlockSpec((WINDOW, HEAD_DIM), lambda b, n, k: (b * nb + n, kcol + k)),
            pl.BlockSpec((WINDOW, HEAD_DIM), lambda b, n, k: (b * nb + jnp.maximum(n - 1, 0), vcol + k)),
            pl.BlockSpec((WINDOW, HEAD_DIM), lambda b, n, k: (b * nb + n, vcol + k)),
            pl.BlockSpec((WINDOW, 2 * WINDOW), lambda b, n, k: (0, 0)),
            smem, smem,
        ],
        out_specs=pl.BlockSpec((WINDOW, G_A * HEAD_DIM), lambda b, n, k: (b * nb + n, k)),
        out_shape=jax.ShapeDtypeStruct((batch * seq, D_A), F32),
        scratch_shapes=[pltpu.VMEM((N_HEADS_A, WINDOW, 2 * WINDOW), F32)],
        compiler_params=_cparams(3),
        name="swa_prompt",
    )(proj, proj, proj, proj, proj, bucket, rel_bias, sinks)


_SWA_SB = 8


def _swa_sample_kernel(q_ref, kn_ref, vn_ref, bk_ref, bv_ref, bkt_ref, rel_ref, sink_ref,
                       o_ref, wk_ref, wv_ref, bias_ref, *, ds, win):
    rows = G_A * ds
    nkeys = win + ds

    @pl.when(pl.program_id(0) == 0)
    def _():
        for h in range(N_HEADS_A):
            kv, g = divmod(h, G_A)
            bias_ref[kv, g * ds:(g + 1) * ds, :] = _bias_from_buckets(bkt_ref[...], rel_ref, h)

    t = lax.broadcasted_iota(jnp.int32, (rows, nkeys), 0) % ds
    s_idx = lax.broadcasted_iota(jnp.int32, (rows, nkeys), 1)
    dist = win + t - s_idx
    valid = (dist >= 0) & (dist < WINDOW)
    g_of_row = lax.broadcasted_iota(jnp.int32, (rows, 1), 0) // ds
    for kv in range(N_KV_A):
        sk = jnp.zeros((rows, 1), F32)
        for g in range(G_A):
            sk = jnp.where(g_of_row == g, sink_ref[kv * G_A + g], sk)
        csl = slice(kv * HEAD_DIM, (kv + 1) * HEAD_DIM)
        for b in range(_SWA_SB):
            rsl = slice(b * ds, (b + 1) * ds)
            q = jnp.concatenate(
                [q_ref[rsl, (kv * G_A + g) * HEAD_DIM:(kv * G_A + g + 1) * HEAD_DIM] for g in range(G_A)],
                axis=0).astype(BF16)
            kk = jnp.concatenate([bk_ref[b, :, csl], kn_ref[rsl, csl]], axis=0).astype(BF16)
            vv = jnp.concatenate([bv_ref[b, :, csl], vn_ref[rsl, csl]], axis=0).astype(BF16)
            s = _dot_nt(q, kk) * SCALE + bias_ref[kv]
            s = jnp.where(valid, s, NEG)
            m = jnp.maximum(jnp.max(s, axis=-1, keepdims=True), sk)
            e = jnp.exp(s - m)
            p = e / (jnp.sum(e, axis=-1, keepdims=True) + jnp.exp(sk - m))
            o = _dot(p.astype(BF16), vv)
            for g in range(G_A):
                o_ref[rsl, (kv * G_A + g) * HEAD_DIM:(kv * G_A + g + 1) * HEAD_DIM] = o[g * ds:(g + 1) * ds]
    for b in range(_SWA_SB):
        rsl = slice(b * ds, (b + 1) * ds)
        wk_ref[b, :win - ds, :] = bk_ref[b, ds:, :]
        wk_ref[b, win - ds:, :] = kn_ref[rsl, :]
        wv_ref[b, :win - ds, :] = bv_ref[b, ds:, :]
        wv_ref[b, win - ds:, :] = vn_ref[rsl, :]


def swa_sample(proj, row0, cache_k, cache_v, rel_bias, sinks, db, ds):
    win = cache_k.shape[1]
    rb = _SWA_SB * ds
    r0 = row0 // rb
    bucket = _t5_bucket(win + jnp.arange(ds)[:, None] - jnp.arange(win + ds)[None, :]).astype(jnp.int32)
    smem = pl.BlockSpec(memory_space=pltpu.SMEM)
    return pl.pallas_call(
        functools.partial(_swa_sample_kernel, ds=ds, win=win),
        grid=(db // _SWA_SB,),
        in_specs=[
            pl.BlockSpec((rb, D_A), lambda i: (r0 + i, 0)),
            pl.BlockSpec((rb, KVD_A), lambda i: (r0 + i, E_KA // KVD_A)),
            pl.BlockSpec((rb, KVD_A), lambda i: (r0 + i, E_VA // KVD_A)),
            pl.BlockSpec((_SWA_SB, win, KVD_A), lambda i: (i, 0, 0)),
            pl.BlockSpec((_SWA_SB, win, KVD_A), lambda i: (i, 0, 0)),
            pl.BlockSpec((ds, win + ds), lambda i: (0, 0)),
            smem, smem,
        ],
        out_specs=[
            pl.BlockSpec((rb, D_A), lambda i: (i, 0)),
            pl.BlockSpec((_SWA_SB, win, KVD_A), lambda i: (i, 0, 0)),
            pl.BlockSpec((_SWA_SB, win, KVD_A), lambda i: (i, 0, 0)),
        ],
        out_shape=[jax.ShapeDtypeStruct((db * ds, D_A), F32),
                   jax.ShapeDtypeStruct(cache_k.shape, F32),
                   jax.ShapeDtypeStruct(cache_v.shape, F32)],
        scratch_shapes=[pltpu.VMEM((N_KV_A, G_A * ds, win + ds), F32)],
        compiler_params=_cparams(1),
        name="swa_sample",
    )(proj, proj, proj, cache_k, cache_v, bucket, rel_bias, sinks)


_HG_CHUNK = 128
_HG_LEVELS = int(math.log2(_HG_CHUNK))


def _hgrn_constants(c):
    levels = int(math.log2(c))
    t = np.arange(c)
    mats = [(t[None, :] <= t[:, None])]
    lvl = np.full((c, c), -1, np.int32)
    lvl[t, t] = levels
    for l in range(levels):
        m = c >> (l + 1)
        pivot = (t // (2 * m)) * (2 * m) + m
        mats.append(t[None, :] <= pivot[:, None])
        same = (t[:, None] // (2 * m)) == (t[None, :] // (2 * m))
        pair = same & ((t[:, None] % (2 * m)) >= m) & ((t[None, :] % (2 * m)) < m)
        lvl[pair] = l
    return np.concatenate(mats, axis=0).astype(np.float32), lvl


def _column_of(row):
    n = row.shape[1]
    r = lax.broadcasted_iota(jnp.int32, (n, n), 0)
    c = lax.broadcasted_iota(jnp.int32, (n, n), 1)
    return jnp.sum(jnp.where(r == c, jnp.broadcast_to(row, (n, n)), 0.0), axis=1, keepdims=True)


def _hgrn_prompt_kernel(q_ref, f_ref, v_ref, cm_ref, lvl_ref, o_ref, s_out_ref, s_ref):
    c = _HG_CHUNK
    ci = pl.program_id(2)

    @pl.when(ci == 0)
    def _():
        s_ref[...] = jnp.zeros_like(s_ref)

    lvl = lvl_ref[...]
    sums_all = _dot_exact01(cm_ref[...], jnp.log(f_ref[...]))
    for h in range(_HG_HP):
        hs = slice(h * HEAD_DIM, (h + 1) * HEAD_DIM)
        q = q_ref[:, hs]
        v = v_ref[:, hs].astype(BF16)
        k = 1.0 - f_ref[:, hs]
        sums = sums_all[:, hs]
        b = sums[:c]
        a = jnp.where(lvl == _HG_LEVELS, _dot_nt(q.astype(BF16), k.astype(BF16)), 0.0)
        for l in range(_HG_LEVELS):
            e = jnp.exp(-jnp.abs(b - sums[(l + 1) * c:(l + 2) * c]))
            a = a + jnp.where(lvl == l, _dot_nt((q * e).astype(BF16), (k * e).astype(BF16)), 0.0)
        s0 = s_ref[h]
        o_ref[:, hs] = _dot((q * jnp.exp(b)).astype(BF16), s0.astype(BF16)) + _dot(a.astype(BF16), v)
        b_last = b[c - 1:c, :]
        khat = (k * jnp.exp(b_last - b)).astype(BF16)
        s_new = _column_of(jnp.exp(b_last)) * s0 + _dot_tn(khat, v)
        s_ref[h] = s_new

        @pl.when(ci == pl.num_programs(2) - 1)
        def _():
            s_out_ref[0, h] = s_new


_HG_HP = 8


def hgrn_prompt(proj, batch, seq):
    c = _HG_CHUNK
    nc = seq // c
    hp = _HG_HP
    cmat, lvl = _hgrn_constants(c)
    w = hp * HEAD_DIM
    qc, fc, vc = E_QB // w, E_F // w, E_VB // w
    return pl.pallas_call(
        _hgrn_prompt_kernel,
        grid=(batch, N_HEADS_B // hp, nc),
        in_specs=[
            pl.BlockSpec((c, w), lambda b, h, i: (b * nc + i, qc + h)),
            pl.BlockSpec((c, w), lambda b, h, i: (b * nc + i, fc + h)),
            pl.BlockSpec((c, w), lambda b, h, i: (b * nc + i, vc + h)),
            pl.BlockSpec(cmat.shape, lambda b, h, i: (0, 0)),
            pl.BlockSpec(lvl.shape, lambda b, h, i: (0, 0)),
        ],
        out_specs=[
            pl.BlockSpec((c, w), lambda b, h, i: (b * nc + i, h)),
            pl.BlockSpec((1, hp, HEAD_DIM, HEAD_DIM), lambda b, h, i: (b, h, 0, 0)),
        ],
        out_shape=[jax.ShapeDtypeStruct((batch * seq, D_B), F32),
                   jax.ShapeDtypeStruct((batch, N_HEADS_B, HEAD_DIM, HEAD_DIM), F32)],
        scratch_shapes=[pltpu.VMEM((hp, HEAD_DIM, HEAD_DIM), F32)],
        compiler_params=_cparams(3),
        name="hgrn_prompt",
    )(proj, proj, proj, jnp.asarray(cmat), jnp.asarray(lvl))


_HG_SB = 8


def _hgrn_sample_kernel(q_ref, f_ref, v_ref, st_ref, o_ref, s_out_ref, *, ds):
    s_idx = lax.broadcasted_iota(jnp.int32, (ds, 1), 0)
    for sb in range(_HG_SB):
        rsl = slice(sb * ds, (sb + 1) * ds)
        q = q_ref[rsl, :]
        f = f_ref[rsl, :]
        v = v_ref[rsl, :]
        k = 1.0 - f
        lf = jnp.log(f)
        rows = [lf[0:1]]
        for t in range(1, ds):
            rows.append(rows[-1] + lf[t:t + 1])
        b = jnp.concatenate(rows, axis=0)
        s0 = st_ref[sb, 0]
        o_inter = _dot((q * jnp.exp(b)).astype(BF16), s0.astype(BF16))
        outs = []
        for t in range(ds):
            w = q[t:t + 1] * k * jnp.exp(jnp.minimum(b[t:t + 1] - b, 0.0))
            a_t = jnp.where(s_idx <= t, jnp.sum(w, axis=-1, keepdims=True), 0.0)
            outs.append(jnp.sum(a_t * v, axis=0, keepdims=True))
        o_ref[rsl, :] = o_inter + jnp.concatenate(outs, axis=0)
        b_last = b[ds - 1:ds, :]
        khat = (k * jnp.exp(b_last - b)).astype(BF16)
        s_out_ref[sb, 0] = _column_of(jnp.exp(b_last)) * s0 + _dot_tn(khat, v.astype(BF16))


def hgrn_sample(proj, row0, state, db, ds):
    rb = _HG_SB * ds
    r0 = row0 // rb
    qc, fc, vc = E_QB // HEAD_DIM, E_F // HEAD_DIM, E_VB // HEAD_DIM
    return pl.pallas_call(
        functools.partial(_hgrn_sample_kernel, ds=ds),
        grid=(db // _HG_SB, N_HEADS_B),
        in_specs=[
            pl.BlockSpec((rb, HEAD_DIM), lambda i, h: (r0 + i, qc + h)),
            pl.BlockSpec((rb, HEAD_DIM), lambda i, h: (r0 + i, fc + h)),
            pl.BlockSpec((rb, HEAD_DIM), lambda i, h: (r0 + i, vc + h)),
            pl.BlockSpec((_HG_SB, 1, HEAD_DIM, HEAD_DIM), lambda i, h: (i, h, 0, 0)),
        ],
        out_specs=[
            pl.BlockSpec((rb, HEAD_DIM), lambda i, h: (i, h)),
            pl.BlockSpec((_HG_SB, 1, HEAD_DIM, HEAD_DIM), lambda i, h: (i, h, 0, 0)),
        ],
        out_shape=[jax.ShapeDtypeStruct((db * ds, D_B), F32),
                   jax.ShapeDtypeStruct(state.shape, F32)],
        compiler_params=_cparams(2),
        name="hgrn_sample",
    )(proj, proj, proj, state)


_CS_BLK = 128


def _split3(x):
    x1 = x.astype(BF16).astype(F32)
    r = x - x1
    x2 = r.astype(BF16).astype(F32)
    return x1, x2, r - x2


_AUG_ONES = 3 * N_HEADS_C
_INV_SCALE = HEAD_DIM ** 0.5
_LOG2E = 1.4426950408889634


def _cumsum_kernel(lf_ref, tri_ref, place_ref, c_ref, aug_ref):
    nblk = lf_ref.shape[1] // _CS_BLK
    lane = lax.broadcasted_iota(jnp.int32, (_CS_BLK, HEAD_DIM), 1)
    ones = jnp.where((lane >= _AUG_ONES) & (lane < _AUG_ONES + 3), 1.0, 0.0)

    def body(i, carry):
        r = pl.ds(pl.multiple_of(i * _CS_BLK, _CS_BLK), _CS_BLK)
        c = _dot_exact01(tri_ref[...], lf_ref[0, r, :]) + carry
        c_ref[0, r, :] = c
        aug = ones
        for j, piece in enumerate(_split3(c * _INV_SCALE)):
            aug = aug + _dot(piece.astype(BF16), place_ref[j].astype(BF16))
        aug_ref[0, r, :] = aug.astype(BF16)
        return c[_CS_BLK - 1:_CS_BLK, :]

    lax.fori_loop(0, nblk, body, jnp.zeros((1, lf_ref.shape[2]), F32))


def cumsum_rows(lf):
    b, s, h = lf.shape
    t = np.arange(_CS_BLK)
    tri = jnp.asarray((t[None, :] <= t[:, None]).astype(np.float32))
    place = np.zeros((3, h, HEAD_DIM), np.float32)
    for j in range(3):
        place[j, np.arange(h), j * h + np.arange(h)] = 1.0
    return pl.pallas_call(
        _cumsum_kernel,
        grid=(b,),
        in_specs=[pl.BlockSpec((1, s, h), lambda i: (i, 0, 0)),
                  pl.BlockSpec((_CS_BLK, _CS_BLK), lambda i: (0, 0)),
                  pl.BlockSpec((3, h, HEAD_DIM), lambda i: (0, 0, 0))],
        out_specs=[pl.BlockSpec((1, s, h), lambda i: (i, 0, 0)),
                   pl.BlockSpec((1, s, HEAD_DIM), lambda i: (i, 0, 0))],
        out_shape=[jax.ShapeDtypeStruct(lf.shape, F32),
                   jax.ShapeDtypeStruct((b, s, HEAD_DIM), BF16)],
        compiler_params=_cparams(1),
        name="fox_cumsum",
    )(lf, tri, jnp.asarray(place))


_FOX_TQ = 1024
_FOX_TK = 1024


_ONES_ROWS = 8


def _fox_prompt_kernel(qi_ref, ki_ref, q_ref, k_ref, vt_ref, cq_ref, ca_ref, o_ref, qa_ref, m_ref, acc_ref):
    kv = pl.program_id(1)
    qi = qi_ref[pl.program_id(2)]
    ki = ki_ref[pl.program_id(2)]
    tq, tk = q_ref.shape[0], k_ref.shape[0]

    @pl.when(ki == 0)
    def _():
        m_ref[...] = jnp.full_like(m_ref, NEG)
        acc_ref[...] = jnp.zeros_like(acc_ref)
        lane = lax.broadcasted_iota(jnp.int32, (tq, HEAD_DIM), 1)
        for g in range(G_C):
            h = kv * G_C + g
            own = (lane == h) | (lane == N_HEADS_C + h) | (lane == 2 * N_HEADS_C + h)
            extra = jnp.where(own, -1.0, 0.0)
            for j, piece in enumerate(_split3(cq_ref[0, 0, :, g:g + 1] * _INV_SCALE)):
                extra = jnp.where(lane == _AUG_ONES + j, piece, extra)
            qa_ref[g, :, :HEAD_DIM] = q_ref[:, g * HEAD_DIM:(g + 1) * HEAD_DIM].astype(BF16)
            qa_ref[g, :, HEAD_DIM:] = extra.astype(BF16)

    def update(masked):
        kaug = jnp.concatenate([k_ref[...].astype(BF16), ca_ref[0]], axis=1)
        vt = jnp.concatenate([vt_ref[0, 0].astype(BF16), jnp.ones((_ONES_ROWS, tk), BF16)],
                             axis=0)
        if masked:
            causal = (lax.broadcasted_iota(jnp.int32, (tk, tq), 0)
                      <= lax.broadcasted_iota(jnp.int32, (tk, tq), 1))
        for g in range(G_C):
            s = _dot_nt(kaug, qa_ref[g]) * (SCALE * _LOG2E)
            if masked:
                s = jnp.where(causal, s, NEG)
            m_prev = m_ref[g]
            m_new = jnp.maximum(m_prev, jnp.max(s, axis=0, keepdims=True))
            alpha = jnp.exp2(m_prev - m_new)
            p = jnp.exp2(s - m_new)
            acc_ref[g] = alpha * acc_ref[g] + _dot(vt, p.astype(BF16))
            m_ref[g] = m_new

    @pl.when(ki < qi)
    def _():
        update(False)

    @pl.when(ki == qi)
    def _():
        update(True)

    @pl.when(ki == qi)
    def _():
        for g in range(G_C):
            acc = acc_ref[g]
            o_ref[:, g * HEAD_DIM:(g + 1) * HEAD_DIM] = jnp.transpose(
                acc[:HEAD_DIM] / acc[HEAD_DIM:HEAD_DIM + 1])


def fox_prompt(projo, c, c_aug, batch, seq):
    tq = tk = min(_FOX_TQ, seq)
    nq, nk = seq // tq, seq // tk
    c_q = jnp.transpose(c.reshape(batch, seq, N_KV_C, G_C), (0, 2, 1, 3))
    v_t = jnp.transpose(projo[:batch * seq, O_V:O_V + KVD_C].reshape(batch, seq, N_KV_C, HEAD_DIM),
                        (0, 2, 3, 1))
    kcol = O_K // HEAD_DIM
    pairs = [(i, j) for i in range(nq) for j in range(i + 1)]
    qi_tab = jnp.asarray([p[0] for p in pairs], jnp.int32)
    ki_tab = jnp.asarray([p[1] for p in pairs], jnp.int32)
    return pl.pallas_call(
        _fox_prompt_kernel,
        grid_spec=pltpu.PrefetchScalarGridSpec(
            num_scalar_prefetch=2,
            grid=(batch, N_KV_C, len(pairs)),
            in_specs=[
                pl.BlockSpec((tq, G_C * HEAD_DIM), lambda b, k, p, qi, ki: (b * nq + qi[p], k)),
                pl.BlockSpec((tk, HEAD_DIM), lambda b, k, p, qi, ki: (b * nk + ki[p], kcol + k)),
                pl.BlockSpec((1, 1, HEAD_DIM, tk), lambda b, k, p, qi, ki: (b, k, 0, ki[p])),
                pl.BlockSpec((1, 1, tq, G_C), lambda b, k, p, qi, ki: (b, k, qi[p], 0)),
                pl.BlockSpec((1, tk, HEAD_DIM), lambda b, k, p, qi, ki: (b, ki[p], 0)),
            ],
            out_specs=pl.BlockSpec((tq, G_C * HEAD_DIM), lambda b, k, p, qi, ki: (b * nq + qi[p], k)),
            scratch_shapes=[pltpu.VMEM((G_C, tq, 2 * HEAD_DIM), BF16),
                            pltpu.VMEM((G_C, 1, tq), F32),
                            pltpu.VMEM((G_C, HEAD_DIM + _ONES_ROWS, tq), F32)],
        ),
        out_shape=jax.ShapeDtypeStruct((batch * seq, D_C), F32),
        compiler_params=_cparams(3),
        name="fox_prompt",
    )(qi_tab, ki_tab, projo, projo, v_t, c_q, c_aug)


_FOX_PP = 16
_FOX_NSEQ = 1


def _fox_sample_kernel(pt_ref, q_ref, kn_ref, vn_ref, lfn_ref, tri_ref, ut1_ref, *refs, ds, page):
    del pt_ref
    pp, nseq = _FOX_PP, _FOX_NSEQ
    n_pg = nseq * pp
    k_refs, v_refs, lf_refs = refs[:n_pg], refs[n_pg:2 * n_pg], refs[2 * n_pg:3 * n_pg]
    o_ref, qb_ref, cq_ref, m_ref, l_ref, acc_ref, carry_ref = refs[3 * n_pg:]
    nrow = N_HEADS_C * ds
    rows_kv = G_C * ds
    step = pl.program_id(1)

    @pl.when(step == 0)
    def _():
        t_row = lax.broadcasted_iota(jnp.int32, (nrow, ds), 0) % ds
        s_col = lax.broadcasted_iota(jnp.int32, (nrow, ds), 1)
        for j in range(nseq):
            qb = q_ref[j].astype(BF16)
            qb_ref[j] = qb
            cn = _dot_exact01_r(lfn_ref[j], tri_ref[...])
            cq = jnp.sum(jnp.where(s_col == t_row, cn, 0.0), axis=-1, keepdims=True)
            cq_ref[j] = cq
            tok = slice(j * ds, (j + 1) * ds)
            kv_cols = [slice(kv * HEAD_DIM, (kv + 1) * HEAD_DIM) for kv in range(N_KV_C)]
            qk = jnp.concatenate(
                [_dot_nt(qb[kv * rows_kv:(kv + 1) * rows_kv], kn_ref[tok, kv_cols[kv]].astype(BF16))
                 for kv in range(N_KV_C)], axis=0)
            s = jnp.where(s_col <= t_row, qk * SCALE + cq - cn, NEG)
            m = jnp.max(s, axis=-1, keepdims=True)
            e = jnp.exp(s - m)
            m_ref[j] = m
            l_ref[j] = jnp.sum(e, axis=-1, keepdims=True)
            eb = e.astype(BF16)
            acc_ref[j] = jnp.concatenate(
                [_dot(eb[kv * rows_kv:(kv + 1) * rows_kv], vn_ref[tok, kv_cols[kv]].astype(BF16))
                 for kv in range(N_KV_C)], axis=0)
        carry_ref[...] = jnp.zeros_like(carry_ref)

    for j in range(nseq):
        kj, vj, lfj = (r[j * pp:(j + 1) * pp] for r in (k_refs, v_refs, lf_refs))
        lf_stack = jnp.concatenate([lfj[i][0] for i in range(pp)], axis=0)
        sums = _dot_exact01_r(lf_stack, ut1_ref[...])
        carry = carry_ref[j]
        dex = []
        for i in range(pp):
            hs = slice(i * N_HEADS_C, (i + 1) * N_HEADS_C)
            dex.append(sums[hs, :page] + carry)
            carry = carry + sums[hs, page:]
        carry_ref[j] = carry

        def kv_rows(refs_, kv):
            strided = pl.ds(kv, page, stride=N_KV_C)
            return jnp.concatenate([refs_[i][0, strided, :] for i in range(pp)], axis=0).astype(BF16)

        qk = jnp.concatenate(
            [_dot_nt(qb_ref[j, kv * rows_kv:(kv + 1) * rows_kv, :], kv_rows(kj, kv))
             for kv in range(N_KV_C)], axis=0)
        bias = jnp.concatenate(
            [jnp.broadcast_to(d[:, None, :], (N_HEADS_C, ds, page)).reshape(nrow, page) for d in dex], axis=1)
        s = qk * SCALE + cq_ref[j] + bias
        m_prev = m_ref[j]
        m_new = jnp.maximum(m_prev, jnp.max(s, axis=-1, keepdims=True))
        alpha = jnp.exp(m_prev - m_new)
        p = jnp.exp(s - m_new)
        l_ref[j] = alpha * l_ref[j] + jnp.sum(p, axis=-1, keepdims=True)
        m_ref[j] = m_new
        pb = p.astype(BF16)
        pv = jnp.concatenate(
            [_dot(pb[kv * rows_kv:(kv + 1) * rows_kv, :], kv_rows(vj, kv)) for kv in range(N_KV_C)], axis=0)
        acc_ref[j] = alpha * acc_ref[j] + pv

    @pl.when(step == pl.num_programs(1) - 1)
    def _():
        for j in range(nseq):
            o_ref[j] = acc_ref[j] / l_ref[j]


def fox_sample(projo, row0, lf_s, cache_k, cache_v, cache_lf, page_table, db, ds):
    pool, page = cache_lf.shape[0], cache_lf.shape[1]
    n_pages = page_table.shape[1]
    pp = _FOX_PP
    nrow = N_HEADS_C * ds
    q = projo[row0:row0 + db * ds, O_Q:O_Q + D_C].reshape(db, ds, N_HEADS_C, HEAD_DIM)
    q = jnp.transpose(q, (0, 2, 1, 3)).reshape(db, nrow, HEAD_DIM)
    lfn = jnp.transpose(lf_s.reshape(db, ds, N_HEADS_C), (0, 2, 1))
    lfn = jnp.broadcast_to(lfn[:, :, None, :], (db, N_HEADS_C, ds, ds)).reshape(db, nrow, ds)
    cache_lft = jnp.transpose(cache_lf, (0, 2, 1))
    t = np.arange(ds)
    tri = jnp.asarray((t[:, None] <= t[None, :]).astype(np.float32))
    u = np.arange(page)
    ut1 = jnp.asarray(np.concatenate([(u[:, None] > u[None, :]), np.ones((page, page), bool)],
                                     axis=1).astype(np.float32))
    nseq = _FOX_NSEQ
    r0 = row0 // (nseq * ds)

    def page_map(j, i):
        return lambda b, p, pt: (pt[b * nseq + j, n_pages - 1 - (p * pp + i)], 0, 0)

    pages = [(j, i) for j in range(nseq) for i in range(pp)]
    in_specs = [
        pl.BlockSpec((nseq, nrow, HEAD_DIM), lambda b, p, pt: (b, 0, 0)),
        pl.BlockSpec((nseq * ds, KVD_C), lambda b, p, pt: (r0 + b, O_K // KVD_C)),
        pl.BlockSpec((nseq * ds, KVD_C), lambda b, p, pt: (r0 + b, O_V // KVD_C)),
        pl.BlockSpec((nseq, nrow, ds), lambda b, p, pt: (b, 0, 0)),
        pl.BlockSpec((ds, ds), lambda b, p, pt: (0, 0)),
        pl.BlockSpec((page, 2 * page), lambda b, p, pt: (0, 0)),
    ]
    in_specs += [pl.BlockSpec((1, page * N_KV_C, HEAD_DIM), page_map(j, i)) for j, i in pages]
    in_specs += [pl.BlockSpec((1, page * N_KV_C, HEAD_DIM), page_map(j, i)) for j, i in pages]
    in_specs += [pl.BlockSpec((1, N_HEADS_C, page), page_map(j, i)) for j, i in pages]
    out = pl.pallas_call(
        functools.partial(_fox_sample_kernel, ds=ds, page=page),
        grid_spec=pltpu.PrefetchScalarGridSpec(
            num_scalar_prefetch=1,
            grid=(db // nseq, n_pages // pp),
            in_specs=in_specs,
            out_specs=pl.BlockSpec((nseq, nrow, HEAD_DIM), lambda b, p, pt: (b, 0, 0)),
            scratch_shapes=[pltpu.VMEM((nseq, nrow, HEAD_DIM), BF16), pltpu.VMEM((nseq, nrow, 1), F32),
                            pltpu.VMEM((nseq, nrow, 1), F32), pltpu.VMEM((nseq, nrow, 1), F32),
                            pltpu.VMEM((nseq, nrow, HEAD_DIM), F32),
                            pltpu.VMEM((nseq, N_HEADS_C, page), F32)],
        ),
        out_shape=jax.ShapeDtypeStruct((db, nrow, HEAD_DIM), F32),
        compiler_params=_cparams(2),
        name="fox_sample",
    )(page_table, q, projo, projo, lfn, tri, ut1, *([cache_k] * len(pages)), *([cache_v] * len(pages)),
      *([cache_lft] * len(pages)))
    out = jnp.transpose(out.reshape(db, N_HEADS_C, ds, HEAD_DIM), (0, 2, 1, 3))
    return out.reshape(db * ds, D_C)


def _pack_bf16_pairs(h):
    c = h.shape[1] // 2
    hb = h.astype(BF16).astype(F32)
    bits = lax.bitcast_convert_type(hb, jnp.uint32)
    return (bits[:, :c] & jnp.uint32(0xFFFF0000)) | (bits[:, c:] >> 16)


def _unpack_bf16_pairs(u):
    hi = lax.bitcast_convert_type(u & jnp.uint32(0xFFFF0000), F32).astype(BF16)
    lo = lax.bitcast_convert_type(u << 16, F32).astype(BF16)
    return jnp.concatenate([hi, lo], axis=1)


def _moe_route_kernel(x_ref, g_ref, r_ref, h_ref, idx_ref, w_ref):
    hn = _rms(x_ref[...], g_ref[...])
    h_ref[...] = _pack_bf16_pairs(hn)
    logits = jnp.dot(hn, r_ref[...], preferred_element_type=F32, precision=lax.Precision.HIGHEST)
    lane = lax.broadcasted_iota(jnp.int32, logits.shape, 1)
    m1 = jnp.max(logits, axis=-1, keepdims=True)
    i1 = jnp.min(jnp.where(logits == m1, lane, N_EXPERTS), axis=-1, keepdims=True)
    rest = jnp.where(lane == i1, -jnp.inf, logits)
    m2 = jnp.max(rest, axis=-1, keepdims=True)
    i2 = jnp.min(jnp.where(rest == m2, lane, N_EXPERTS), axis=-1, keepdims=True)
    e = jnp.exp(m2 - m1)
    slot = lax.broadcasted_iota(jnp.int32, idx_ref.shape, 1)
    idx_ref[...] = jnp.where(slot == 0, i1, i2)
    w_ref[...] = jnp.where(slot == 0, 1.0 / (1.0 + e), e / (1.0 + e))


def moe_route(x, g, router, tm):
    n, d = x.shape
    return pl.pallas_call(
        _moe_route_kernel,
        grid=(n // tm,),
        in_specs=[pl.BlockSpec((tm, d), lambda i: (i, 0)),
                  pl.BlockSpec((1, d), lambda i: (0, 0)),
                  pl.BlockSpec((d, N_EXPERTS), lambda i: (0, 0))],
        out_specs=[pl.BlockSpec((tm, d // 2), lambda i: (i, 0)),
                   pl.BlockSpec((tm, TOP_K), lambda i: (i, 0)),
                   pl.BlockSpec((tm, TOP_K), lambda i: (i, 0))],
        out_shape=[jax.ShapeDtypeStruct((n, d // 2), jnp.uint32),
                   jax.ShapeDtypeStruct((n, TOP_K), jnp.int32),
                   jax.ShapeDtypeStruct((n, TOP_K), F32)],
        compiler_params=_cparams(1),
        name="moe_route",
    )(x, g.reshape(1, d), router)


def _route_plan(idx, w, tm):
    n = idx.shape[0]
    flat_e = idx.reshape(-1)
    onehot = (flat_e[:, None] == jnp.arange(N_EXPERTS, dtype=jnp.int32)[None, :]).astype(jnp.int32)
    csum = jnp.cumsum(onehot, axis=0)
    rank = jnp.take_along_axis(csum, flat_e[:, None], axis=1)[:, 0] - 1
    count = csum[-1]
    tiles_e = (count + tm - 1) // tm
    tile_end = jnp.cumsum(tiles_e)
    tile_start = tile_end - tiles_e
    pos = tile_start[flat_e] * tm + rank
    n_tiles = (TOP_K * n) // tm + N_EXPERTS
    tile = jnp.arange(n_tiles, dtype=jnp.int32)
    tile_expert = jnp.minimum(jnp.searchsorted(tile_end, tile, side="right"), N_EXPERTS - 1).astype(jnp.int32)
    tile_valid = jnp.clip(count[tile_expert] - (tile - tile_start[tile_expert]) * tm, 0, tm)
    tile_valid = jnp.where(tile < tile_end[-1], tile_valid, 0).astype(jnp.int32)
    pair = jnp.zeros((n_tiles * tm,), jnp.int32).at[pos].set(jnp.arange(TOP_K * n, dtype=jnp.int32))
    row_in_tile = jnp.arange(n_tiles * tm, dtype=jnp.int32) % tm
    gate = jnp.where(row_in_tile < jnp.repeat(tile_valid, tm), jnp.take(w.reshape(-1), pair, mode="clip"), 0.0)
    return pos.reshape(n, TOP_K), pair // TOP_K, gate.reshape(-1, 1), tile_expert, tile_valid


_MOE_CHUNKS = 1


def moe_layer(x, g, router, w_gate, w_up, w_down, tm_route, n_split):
    tm = min(_FFN_TM, x.shape[0])
    h, idx, w = moe_route(x, g, router, tm_route)
    pos, token, gate, tile_expert, tile_valid = _route_plan(idx, w, tm)
    tiles = token.shape[0] // tm
    n_chunks = next(c for c in (_MOE_CHUNKS, 2, 1) if tiles % c == 0)
    tiles_c = tiles // n_chunks
    act = None
    for c in range(n_chunks):
        ts = slice(c * tiles_c, (c + 1) * tiles_c)
        h_sorted = jnp.take(h, token[c * tiles_c * tm:(c + 1) * tiles_c * tm], axis=0, mode="clip")
        act = ffn_up(h_sorted, tile_expert[ts], tile_valid[ts], w_gate, w_up,
                     out=act, tile0=c * tiles_c, total_tiles=tiles)
    y_sorted = ffn_down(act, tile_expert, tile_valid, w_down, gate=gate)
    y0 = jnp.take(y_sorted, pos[:, 0], axis=0, mode="clip")
    y1 = jnp.take(y_sorted, pos[:, 1], axis=0, mode="clip")
    return tuple(x[r] + y0[r] + y1[r] for r in (slice(0, n_split), slice(n_split, None)))


def _row_tile(n, want):
    t = min(want, n)
    while n % t:
        t //= 2
    return t


def kernel(x_prompt, x_sample, cache_k_win, cache_v_win, state_hgrn, cache_k_fox, cache_v_fox, cache_logf_fox, page_table, rel_bias, norm_mix_e, w_in_e, q_norm_a, k_norm_a, sinks_a, lb_b, o_norm_b, w_out_e, norm_ffn_e, w_gate_e, w_up_e, w_down_e, norm_mix_o, w_in_o, fgate_bias_c, q_norm_c, k_norm_c, w_out_o, norm_ffn_o, router_o, w_gate_x, w_up_x, w_down_x):
    batch, seq, d = x_prompt.shape
    db, ds, _ = x_sample.shape
    n_p, n_s = batch * seq, db * ds
    n = n_p + n_s
    win = cache_k_win.shape[2]
    pool, page = cache_k_fox.shape[1], cache_k_fox.shape[2]
    tm_big = _row_tile(math.gcd(n_p, n_s), 1024)

    x = jnp.concatenate([x_prompt.reshape(n_p, d), x_sample.reshape(n_s, d)], axis=0)

    proj = even_project(x, norm_mix_e[0], w_in_e[0], q_norm_a[0], k_norm_a[0], lb_b, tm_big)
    oa_p = swa_prompt(proj, rel_bias, sinks_a[0], batch, seq)
    oa_s, wk_s, wv_s = swa_sample(proj, n_p, cache_k_win[0].reshape(db, win, KVD_A),
                                  cache_v_win[0].reshape(db, win, KVD_A), rel_bias, sinks_a[0], db, ds)
    ob_p, hg_p = hgrn_prompt(proj, batch, seq)
    ob_s, hg_s = hgrn_sample(proj, n_p, state_hgrn[0], db, ds)
    y = even_merge(oa_p, oa_s, ob_p, ob_s, proj, o_norm_b[0], w_out_e[0], x, tm_big // 2)
    y = dense_ffn(y, norm_ffn_e[0], w_gate_e[0], w_up_e[0], w_down_e[0])

    projo, lf = odd_project(y, norm_mix_o[0], w_in_o[0], q_norm_c[0], k_norm_c[0], fgate_bias_c[0], tm_big)
    c, c_aug = cumsum_rows(lf[:n_p].reshape(batch, seq, N_HEADS_C))
    oc_p = fox_prompt(projo, c, c_aug, batch, seq)
    oc_s = fox_sample(projo, n_p, lf[n_p:], cache_k_fox[0].reshape(pool, page * N_KV_C, HEAD_DIM),
                      cache_v_fox[0].reshape(pool, page * N_KV_C, HEAD_DIM), cache_logf_fox[0],
                      page_table, db, ds)
    y = out_project(oc_p, oc_s, w_out_o[0], y, tm_big)
    y_p, y_s = moe_layer(y, norm_ffn_o[0], router_o[0], w_gate_x[0], w_up_x[0], w_down_x[0], tm_big, n_p)

    def prompt_tail(col, width, heads):
        t = proj[:n_p, col:col + width].reshape(batch, seq, heads, HEAD_DIM)
        return t[:, seq - WINDOW:][None]

    return (
        y_p.reshape(batch, seq, d),
        y_s.reshape(db, ds, d),
        prompt_tail(E_KA, KVD_A, N_KV_A),
        prompt_tail(E_VA, KVD_A, N_KV_A),
        wk_s.reshape(1, db, win, N_KV_A, HEAD_DIM),
        wv_s.reshape(1, db, win, N_KV_A, HEAD_DIM),
        hg_p[None],
        hg_s[None],
        projo[:n_p, O_K:O_K + KVD_C].reshape(1, batch, seq, N_KV_C, HEAD_DIM),
        projo[:n_p, O_V:O_V + KVD_C].reshape(1, batch, seq, N_KV_C, HEAD_DIM),
        lf[:n_p].reshape(1, batch, seq, N_HEADS_C),
        projo[n_p:, O_K:O_K + KVD_C].reshape(1, db, ds, N_KV_C, HEAD_DIM),
        projo[n_p:, O_V:O_V + KVD_C].reshape(1, db, ds, N_KV_C, HEAD_DIM),
        lf[n_p:].reshape(1, db, ds, N_HEADS_C),
    )
```
